```python
import jax, jax.numpy as jnp
from jax import lax
import numpy as np

D_MODEL = 1024
BATCH = 4
SEQ = 4096
DEPTH = 2
DEC_BATCH = 128
DEC_SEQ = 8
PAST_LEN = 8192
PAGE_SIZE = 128

N_A_LAYERS = DEPTH // 2
N_B_LAYERS = DEPTH - N_A_LAYERS
POOL_WINDOWS = (2, 4, 8, 16)
N_POOL_GROUPS = len(POOL_WINDOWS)
POOL_GROUP_DIM = D_MODEL // N_POOL_GROUPS
POOL_CTX = max(POOL_WINDOWS) - 1
HEAD_DIM = 64
N_HEADS = D_MODEL // HEAD_DIM
N_KV_HEADS = 2
Q_PER_KV = N_HEADS // N_KV_HEADS
WINDOW = 128
BLOCK = 128
WIN_BUF = min(WINDOW, PAST_LEN)
ROPE_DIM = HEAD_DIM // 4
ROPE_THETA = 500000.0
N_GROUPS = 4
E_PER_GROUP = 4
N_EXPERTS = N_GROUPS * E_PER_GROUP
TOP_K_IN_GROUP = 2
D_EXPERT = 256
D_PLE = 256
EPS = 1e-6

kernel_name = "yoco_pool_swa_sink_hmoe_step"


def rms_norm(x, g):
    xf = x.astype(jnp.float32)
    y = xf * lax.rsqrt(jnp.mean(xf * xf, axis=-1, keepdims=True) + EPS)
    return (y * g.astype(jnp.float32)).astype(x.dtype)


def partial_rope(x, pos):
    half = ROPE_DIM // 2
    inv = jnp.float32(ROPE_THETA) ** (-(jnp.arange(half, dtype=jnp.float32) * 2.0 / ROPE_DIM))
    ang = pos.astype(jnp.float32)[:, None] * inv[None, :]
    cos = jnp.cos(ang)[:, None, :]
    sin = jnp.sin(ang)[:, None, :]
    xr = x[..., :ROPE_DIM].astype(jnp.float32)
    x1, x2 = xr[..., :half], xr[..., half:]
    rot = jnp.concatenate([x1 * cos - x2 * sin, x1 * sin + x2 * cos], axis=-1).astype(x.dtype)
    return jnp.concatenate([rot, x[..., ROPE_DIM:]], axis=-1)


def multiscale_pool(h_ext, pos, pool_w, pool_scale):
    B = h_ext.shape[0]
    T = h_ext.shape[1] - POOL_CTX
    hf = h_ext.astype(jnp.float32)
    cs = jnp.concatenate([jnp.zeros_like(hf[:, :1]), jnp.cumsum(hf, axis=1)], axis=1)
    end = cs[:, POOL_CTX + 1:]
    means = []
    for g, win in enumerate(POOL_WINDOWS):
        sl = slice(g * POOL_GROUP_DIM, (g + 1) * POOL_GROUP_DIM)
        start = cs[:, POOL_CTX + 1 - win: POOL_CTX + 1 - win + T, sl]
        cnt = jnp.minimum(win, pos + 1).astype(jnp.float32)[None, :, None]
        means.append((end[..., sl] - start) / cnt)
    pooled = jnp.concatenate(means, axis=-1) - hf[:, POOL_CTX:]
    pooled = pooled.reshape(B, T, N_POOL_GROUPS, POOL_GROUP_DIM).astype(h_ext.dtype)
    mixed = jnp.einsum("btgc,gcd->btgd", pooled, pool_w).reshape(B, T, D_MODEL)
    return mixed * pool_scale


def shared_kv(x, pos, kv_norm, w_kv, k_norm):
    B, T, _ = x.shape
    h = rms_norm(x, kv_norm)
    kv = jnp.einsum("btd,de->bte", h, w_kv).reshape(B, T, 2, N_KV_HEADS, HEAD_DIM)
    k = partial_rope(rms_norm(kv[:, :, 0], k_norm), pos)
    v = kv[:, :, 1]
    return k, v


def window_attention(q, k, v, mask, sinks):
    s = jnp.einsum("bnqkgd,bnskd->bnkgqs", q, k).astype(jnp.float32) * (HEAD_DIM ** -0.5)
    s = jnp.where(mask[None, :, None, None], s, jnp.float32(-1e30))
    sink = sinks.astype(jnp.float32).reshape(1, 1, N_KV_HEADS, Q_PER_KV, 1, 1)
    sink = jnp.broadcast_to(sink, s.shape[:-1] + (1,))
    pr = jax.nn.softmax(jnp.concatenate([s, sink], axis=-1), axis=-1)[..., :-1]
    return jnp.einsum("bnkgqs,bnskd->bnqkgd", pr.astype(v.dtype), v)


def banded_attention(q, k, v, sinks):
    B, T = q.shape[0], q.shape[1]
    nb = T // BLOCK
    qb = q.reshape(B, nb, BLOCK, N_KV_HEADS, Q_PER_KV, HEAD_DIM)
    kb = k.reshape(B, nb, BLOCK, N_KV_HEADS, HEAD_DIM)
    vb = v.reshape(B, nb, BLOCK, N_KV_HEADS, HEAD_DIM)
    kk = jnp.concatenate([jnp.concatenate([jnp.zeros_like(kb[:, :1]), kb[:, :-1]], axis=1), kb], axis=2)
    vv = jnp.concatenate([jnp.concatenate([jnp.zeros_like(vb[:, :1]), vb[:, :-1]], axis=1), vb], axis=2)
    blk = jnp.arange(nb, dtype=jnp.int32)[:, None] * BLOCK
    qpos = blk + jnp.arange(BLOCK, dtype=jnp.int32)[None, :]
    kpos = blk - BLOCK + jnp.arange(2 * BLOCK, dtype=jnp.int32)[None, :]
    rel = qpos[:, :, None] - kpos[:, None, :]
    mask = (rel >= 0) & (rel < WINDOW) & (kpos[:, None, :] >= 0)
    o = window_attention(qb, kk, vv, mask, sinks)
    return o.reshape(B, T, N_HEADS * HEAD_DIM)


def buffered_attention(q, k_all, v_all, qpos, kpos, sinks):
    B, T = q.shape[0], q.shape[1]
    qb = q.reshape(B, 1, T, N_KV_HEADS, Q_PER_KV, HEAD_DIM)
    rel = qpos[:, None] - kpos[None, :]
    mask = ((rel >= 0) & (rel < WINDOW))[None]
    o = window_attention(qb, k_all[:, None], v_all[:, None], mask, sinks)
    return o.reshape(B, T, N_HEADS * HEAD_DIM)


def hier_moe(h, wg, bg, we, be, w_gate, w_up, w_down):
    shp = h.shape
    ht = h.reshape(-1, D_MODEL)
    p_grp = jax.nn.softmax(jnp.einsum("nd,dg->ng", ht, wg).astype(jnp.float32) + bg.astype(jnp.float32), axis=-1)
    g_sel = jnp.argmax(p_grp, axis=-1)
    g_w = jnp.max(p_grp, axis=-1)
    le = jnp.einsum("nd,dge->nge", ht, we).astype(jnp.float32) + be.astype(jnp.float32)
    le_sel = jnp.take_along_axis(le, g_sel[:, None, None], axis=1)[:, 0]
    top_v, top_i = lax.top_k(le_sel, TOP_K_IN_GROUP)
    w_in = jax.nn.softmax(top_v, axis=-1) * g_w[:, None]
    idx = g_sel[:, None] * E_PER_GROUP + top_i
    gates = jnp.sum(jax.nn.one_hot(idx, N_EXPERTS, dtype=jnp.float32) * w_in[..., None], axis=1)
    out = jnp.zeros(ht.shape, jnp.float32)
    for e in range(N_EXPERTS):
        a = jax.nn.silu(ht @ w_gate[e]) * (ht @ w_up[e])
        out = out + gates[:, e:e + 1] * (a @ w_down[e]).astype(jnp.float32)
    return out.astype(h.dtype).reshape(shp)


def ple_add(x, p_i, norm_g, w_gate, w_proj):
    gate = jax.nn.sigmoid(jnp.einsum("btd,de->bte", rms_norm(x, norm_g), w_gate))
    return x + gate * jnp.einsum("btc,cd->btd", p_i, w_proj)


def trunk(x, p, pos, pool_ctx, k_buf, v_buf, w):
    B, T, _ = x.shape
    new_pool = []
    k_all = v_all = kv_pos = k_state = v_state = None
    for i in range(DEPTH):
        if i == N_A_LAYERS:
            k_new, v_new = shared_kv(x, pos, w["kv_norm"], w["w_kv"], w["k_norm"])
            if k_buf is None:
                k_all, v_all = k_new, v_new
            else:
                k_all = jnp.concatenate([k_buf, k_new], axis=1)
                v_all = jnp.concatenate([v_buf, v_new], axis=1)
                kv_pos = jnp.concatenate([pos[0] - WIN_BUF + jnp.arange(WIN_BUF, dtype=jnp.int32), pos])
            k_state = k_all[:, -WIN_BUF:]
            v_state = v_all[:, -WIN_BUF:]
        h = rms_norm(x, w["norm_mix"][i])
        if i < N_A_LAYERS:
            h_ext = jnp.concatenate([pool_ctx[i], h], axis=1)
            x = x + multiscale_pool(h_ext, pos, w["pool_w"][i], w["pool_scale"][i])
            new_pool.append(h_ext[:, -POOL_CTX:])
        else:
            j = i - N_A_LAYERS
            q = jnp.einsum("btd,de->bte", h, w["w_q"][j]).reshape(B, T, N_HEADS, HEAD_DIM)
            q = partial_rope(rms_norm(q, w["q_norm"][j]), pos)
            if k_buf is None:
                o = banded_attention(q, k_all, v_all, w["sinks"][j])
            else:
                o = buffered_attention(q, k_all, v_all, pos, kv_pos, w["sinks"][j])
            x = x + jnp.einsum("bte,ed->btd", o, w["w_o"][j])
        x = x + hier_moe(rms_norm(x, w["norm_ffn"][i]), w["router_g_w"][i], w["router_g_b"][i],
                         w["router_e_w"][i], w["router_e_b"][i], w["exp_gate"][i], w["exp_up"][i],
                         w["exp_down"][i])
        x = ple_add(x, p[i], w["norm_ple"][i], w["ple_gate"][i], w["ple_proj"][i])
    return x, jnp.stack(new_pool, axis=0), k_state, v_state


def setup_inputs(seed: int = 0) -> dict:
    key = jax.random.key(seed)
    ks = jax.random.split(key, 32)

    def nrm(k, shape, scale=1.0):
        return jax.random.normal(k, shape, jnp.float32) * scale

    def gain(k, shape):
        return 1.0 + 0.02 * jax.random.normal(k, shape, jnp.float32)

    return {
        "x_prompt": nrm(ks[0], (BATCH, SEQ, D_MODEL)),
        "x_sample": nrm(ks[1], (DEC_BATCH, DEC_SEQ, D_MODEL)),
        "p_prompt": nrm(ks[2], (DEPTH, BATCH, SEQ, D_PLE)),
        "p_sample": nrm(ks[3], (DEPTH, DEC_BATCH, DEC_SEQ, D_PLE)),
        "state_pool": nrm(ks[4], (N_A_LAYERS, DEC_BATCH, POOL_CTX, D_MODEL)),
        "cache_k_win": nrm(ks[5], (DEC_BATCH, WIN_BUF, N_KV_HEADS, HEAD_DIM)),
        "cache_v_win": nrm(ks[6], (DEC_BATCH, WIN_BUF, N_KV_HEADS, HEAD_DIM)),
        "norm_mix": gain(ks[7], (DEPTH, D_MODEL)),
        "norm_ffn": gain(ks[8], (DEPTH, D_MODEL)),
        "norm_ple": gain(ks[9], (DEPTH, D_MODEL)),
        "pool_w": nrm(ks[10], (N_A_LAYERS, N_POOL_GROUPS, POOL_GROUP_DIM, POOL_GROUP_DIM), POOL_GROUP_DIM ** -0.5),
        "pool_scale": gain(ks[11], (N_A_LAYERS, D_MODEL)),
        "kv_norm": gain(ks[12], (D_MODEL,)),
        "w_kv": nrm(ks[13], (D_MODEL, 2 * N_KV_HEADS * HEAD_DIM), D_MODEL ** -0.5),
        "k_norm": gain(ks[14], (HEAD_DIM,)),
        "w_q": nrm(ks[15], (N_B_LAYERS, D_MODEL, N_HEADS * HEAD_DIM), D_MODEL ** -0.5),
        "q_norm": gain(ks[16], (N_B_LAYERS, HEAD_DIM)),
        "sinks": nrm(ks[17], (N_B_LAYERS, N_HEADS), 0.5),
        "w_o": nrm(ks[18], (N_B_LAYERS, N_HEADS * HEAD_DIM, D_MODEL), (N_HEADS * HEAD_DIM) ** -0.5),
        "router_g_w": nrm(ks[19], (DEPTH, D_MODEL, N_GROUPS), D_MODEL ** -0.5),
        "router_g_b": nrm(ks[20], (DEPTH, N_GROUPS), 0.01),
        "router_e_w": nrm(ks[21], (DEPTH, D_MODEL, N_GROUPS, E_PER_GROUP), D_MODEL ** -0.5),
        "router_e_b": nrm(ks[22], (DEPTH, N_GROUPS, E_PER_GROUP), 0.01),
        "exp_gate": nrm(ks[23], (DEPTH, N_EXPERTS, D_MODEL, D_EXPERT), D_MODEL ** -0.5),
        "exp_up": nrm(ks[24], (DEPTH, N_EXPERTS, D_MODEL, D_EXPERT), D_MODEL ** -0.5),
        "exp_down": nrm(ks[25], (DEPTH, N_EXPERTS, D_EXPERT, D_MODEL), D_EXPERT ** -0.5),
        "ple_gate": nrm(ks[26], (DEPTH, D_MODEL, D_MODEL), D_MODEL ** -0.5),
        "ple_proj": nrm(ks[27], (DEPTH, D_PLE, D_MODEL), D_PLE ** -0.5),
    }


def reference(x_prompt, x_sample, p_prompt, p_sample, state_pool, cache_k_win, cache_v_win,
              norm_mix, norm_ffn, norm_ple, pool_w, pool_scale, kv_norm, w_kv, k_norm,
              w_q, q_norm, sinks, w_o, router_g_w, router_g_b, router_e_w, router_e_b,
              exp_gate, exp_up, exp_down, ple_gate, ple_proj):
    w = {
        "norm_mix": norm_mix, "norm_ffn": norm_ffn, "norm_ple": norm_ple,
        "pool_w": pool_w, "pool_scale": pool_scale,
        "kv_norm": kv_norm, "w_kv": w_kv, "k_norm": k_norm,
        "w_q": w_q, "q_norm": q_norm, "sinks": sinks, "w_o": w_o,
        "router_g_w": router_g_w, "router_g_b": router_g_b,
        "router_e_w": router_e_w, "router_e_b": router_e_b,
        "exp_gate": exp_gate, "exp_up": exp_up, "exp_down": exp_down,
        "ple_gate": ple_gate, "ple_proj": ple_proj,
    }
    pos_p = jnp.arange(x_prompt.shape[1], dtype=jnp.int32)
    ctx_p = jnp.zeros((N_A_LAYERS, x_prompt.shape[0], POOL_CTX, D_MODEL), x_prompt.dtype)
    y_prompt, pool_prompt, k_win_prompt, v_win_prompt = trunk(
        x_prompt, p_prompt, pos_p, ctx_p, None, None, w)
    pos_s = PAST_LEN + jnp.arange(x_sample.shape[1], dtype=jnp.int32)
    y_sample, pool_sample, k_win_sample, v_win_sample = trunk(
        x_sample, p_sample, pos_s, state_pool, cache_k_win, cache_v_win, w)
    return (y_prompt, y_sample, pool_prompt, pool_sample,
            k_win_prompt, v_win_prompt, k_win_sample, v_win_sample)
```

```python
import functools

import jax
import jax.numpy as jnp
from jax import lax
from jax.experimental import pallas as pl
from jax.experimental.pallas import tpu as pltpu

D_MODEL = 1024
PAST_LEN = 8192
POOL_WINDOWS = (2, 4, 8, 16)
POOL_GROUP_DIM = D_MODEL // len(POOL_WINDOWS)
POOL_CTX = max(POOL_WINDOWS) - 1
HEAD_DIM = 64
N_HEADS = D_MODEL // HEAD_DIM
N_KV_HEADS = 2
Q_PER_KV = N_HEADS // N_KV_HEADS
WINDOW = 128
ROPE_DIM = HEAD_DIM // 4
ROPE_THETA = 500000.0
N_GROUPS = 4
E_PER_GROUP = 4
N_EXPERTS = N_GROUPS * E_PER_GROUP
D_EXPERT = 256
D_PLE = 256
EPS = 1e-6

LANES = 128
CTX_ROWS = 16
TOKEN_TILE = 1024
ATTN_TILE = 512
SAMPLE_SEQ_BLOCK = 16
VMEM_LIMIT = 56 * 1024 * 1024
NEG_BIG = -1e30
F32 = jnp.float32
BF16 = jnp.bfloat16


def _rms(x, g):
    return x * lax.rsqrt(jnp.mean(x * x, axis=-1, keepdims=True) + EPS) * g


def _dot(a, b):
    return jnp.dot(a, b, preferred_element_type=F32)


def _dot_nt(a, b):
    return lax.dot_general(a, b, (((1,), (1,)), ((), ())), preferred_element_type=F32)


def _sigmoid(x):
    return 1.0 / (1.0 + jnp.exp(-x))


def _head_norm_rope(x, gain, cos, sin_next, sin_prev):
    lane = lax.broadcasted_iota(jnp.int32, x.shape, 1)
    first = lane < HEAD_DIM
    sq = x * x
    m0 = jnp.sum(jnp.where(first, sq, 0.0), axis=-1, keepdims=True) * (1.0 / HEAD_DIM)
    m1 = jnp.sum(jnp.where(first, 0.0, sq), axis=-1, keepdims=True) * (1.0 / HEAD_DIM)
    y = x * lax.rsqrt(jnp.where(first, m0, m1) + EPS) * gain
    half = ROPE_DIM // 2
    return (y * cos + pltpu.roll(y, LANES - half, 1) * sin_next
            + pltpu.roll(y, half, 1) * sin_prev)


def _pool_mix(h, hs_read, cnt_of, x, pw_ref, pscale_ref, store):
    for g, win in enumerate(POOL_WINDOWS):
        cols = slice(g * POOL_GROUP_DIM, (g + 1) * POOL_GROUP_DIM)
        hg = h[..., cols]
        s = hg
        for j in range(1, win):
            s = s + hs_read(j, cols)
        pooled = s / cnt_of(win) - hg
        rows = pooled.reshape(-1, POOL_GROUP_DIM).astype(BF16)
        mixed = _dot(rows, pw_ref[g]).reshape(hg.shape)
        store(cols, x[..., cols] + mixed * pscale_ref[:, cols])


def _mixer0_prompt_kernel(x_ref, xprev_ref, nmix_ref, pw_ref, pscale_ref,
                          x1_ref, pool_ref, hs_ref):
    t = pl.program_id(1)
    tm = x_ref.shape[1]
    x = x_ref[0]
    h = _rms(x, nmix_ref[...])
    hp = _rms(xprev_ref[0], nmix_ref[...])
    hs_ref[0:CTX_ROWS, :] = jnp.where(t > 0, hp, 0.0)
    hs_ref[CTX_ROWS:, :] = h
    pos = t * tm + lax.broadcasted_iota(jnp.int32, (tm, 1), 0)

    def hs_read(j, cols):
        return hs_ref[CTX_ROWS - j:CTX_ROWS - j + tm, cols]

    def cnt_of(win):
        return jnp.minimum(win, pos + 1).astype(F32)

    def store(cols, val):
        x1_ref[0, :, cols] = val

    _pool_mix(h, hs_read, cnt_of, x, pw_ref, pscale_ref, store)

    @pl.when(t == pl.num_programs(1) - 1)
    def _():
        pool_ref[0] = hs_ref[tm:tm + CTX_ROWS, :]


def _mixer0_sample_kernel(x_ref, ctx_ref, nmix_ref, pw_ref, pscale_ref,
                          x1_ref, hnew_ref, hs_ref):
    x = x_ref[...]
    steps = x.shape[1]
    h = _rms(x, nmix_ref[...])
    hs_ref[:, 0:CTX_ROWS, :] = ctx_ref[...]
    hs_ref[:, CTX_ROWS:, :] = h
    hnew_ref[...] = h
    pos = PAST_LEN + lax.broadcasted_iota(jnp.int32, (1, steps, 1), 1)

    def hs_read(j, cols):
        return hs_ref[:, CTX_ROWS - j:CTX_ROWS - j + steps, cols]

    def cnt_of(win):
        return jnp.minimum(win, pos + 1).astype(F32)

    def store(cols, val):
        x1_ref[:, :, cols] = val

    _pool_mix(h, hs_read, cnt_of, x, pw_ref, pscale_ref, store)


def _mixer0_prompt(x, nmix, pw, pscale):
    b, seq, d = x.shape
    tm = TOKEN_TILE
    nt = seq // tm
    per = tm // CTX_ROWS
    const2 = lambda bi, ti: (0, 0)
    return pl.pallas_call(
        _mixer0_prompt_kernel,
        grid=(b, nt),
        in_specs=[
            pl.BlockSpec((1, tm, d), lambda bi, ti: (bi, ti, 0)),
            pl.BlockSpec((1, CTX_ROWS, d), lambda bi, ti: (bi, jnp.maximum(ti * per - 1, 0), 0)),
            pl.BlockSpec((1, d), const2),
            pl.BlockSpec(pw.shape, lambda bi, ti: (0, 0, 0)),
            pl.BlockSpec((1, d), const2),
        ],
        out_specs=[
            pl.BlockSpec((1, tm, d), lambda bi, ti: (bi, ti, 0)),
            pl.BlockSpec((1, CTX_ROWS, d), lambda bi, ti: (bi, 0, 0)),
        ],
        out_shape=[
            jax.ShapeDtypeStruct((b, seq, d), F32),
            jax.ShapeDtypeStruct((b, CTX_ROWS, d), F32),
        ],
        scratch_shapes=[pltpu.VMEM((tm + CTX_ROWS, d), F32)],
        compiler_params=pltpu.CompilerParams(
            dimension_semantics=("arbitrary", "arbitrary"), vmem_limit_bytes=VMEM_LIMIT),
        name="mixer0_prompt",
    )(x, x, nmix, pw, pscale)


def _mixer0_sample(x, ctx16, nmix, pw, pscale):
    b, steps, d = x.shape
    sb = SAMPLE_SEQ_BLOCK
    const2 = lambda i: (0, 0)
    return pl.pallas_call(
        _mixer0_sample_kernel,
        grid=(b // sb,),
        in_specs=[
            pl.BlockSpec((sb, steps, d), lambda i: (i, 0, 0)),
            pl.BlockSpec((sb, CTX_ROWS, d), lambda i: (i, 0, 0)),
            pl.BlockSpec((1, d), const2),
            pl.BlockSpec(pw.shape, lambda i: (0, 0, 0)),
            pl.BlockSpec((1, d), const2),
        ],
        out_specs=[
            pl.BlockSpec((sb, steps, d), lambda i: (i, 0, 0)),
            pl.BlockSpec((sb, steps, d), lambda i: (i, 0, 0)),
        ],
        out_shape=[
            jax.ShapeDtypeStruct((b, steps, d), F32),
            jax.ShapeDtypeStruct((b, steps, d), F32),
        ],
        scratch_shapes=[pltpu.VMEM((sb, CTX_ROWS + steps, d), F32)],
        compiler_params=pltpu.CompilerParams(
            dimension_semantics=("arbitrary",), vmem_limit_bytes=VMEM_LIMIT),
        name="mixer0_sample",
    )(x, ctx16, nmix, pw, pscale)


def _route(logits):
    lane = lax.broadcasted_iota(jnp.int32, logits.shape, 1)
    lanef = lane.astype(F32)
    no_lane = float(LANES)
    is_grp = lane < N_GROUPS
    gmax = jnp.max(jnp.where(is_grp, logits, NEG_BIG), axis=-1, keepdims=True)
    gsum = jnp.sum(jnp.where(is_grp, jnp.exp(jnp.minimum(logits - gmax, 0.0)), 0.0),
                   axis=-1, keepdims=True)
    g_w = 1.0 / gsum
    g_sel = jnp.min(jnp.where(is_grp & (logits == gmax), lanef, no_lane), axis=-1, keepdims=True)
    lo = N_GROUPS + E_PER_GROUP * g_sel
    in_grp = (lanef >= lo) & (lanef < lo + E_PER_GROUP)
    v1 = jnp.max(jnp.where(in_grp, logits, NEG_BIG), axis=-1, keepdims=True)
    i1 = jnp.min(jnp.where(in_grp & (logits == v1), lanef, no_lane), axis=-1, keepdims=True)
    rest = in_grp & (lanef != i1)
    v2 = jnp.max(jnp.where(rest, logits, NEG_BIG), axis=-1, keepdims=True)
    i2 = jnp.min(jnp.where(rest & (logits == v2), lanef, no_lane), axis=-1, keepdims=True)
    e2 = jnp.exp(v2 - v1)
    den = 1.0 + e2
    w1 = (1.0 / den) * g_w
    w2 = (e2 / den) * g_w
    return jnp.where(lanef == i1, w1, 0.0) + jnp.where(lanef == i2, w2, 0.0)


def _moe_kernel(with_kv, *refs):
    if with_kv:
        (x1_ref, p_ref, nffn_ref, rwhi_ref, rwlo_ref, rb_ref, wg_ref, wu_ref, wd_ref,
         nple_ref, pg_ref, pp_ref, kvn_ref, wkv_ref, kn_ref, cos_ref, sn_ref, sp_ref,
         out_ref, k_ref, v_ref, hn_s, acc_s, gates_s) = refs
    else:
        (x1_ref, p_ref, nffn_ref, rwhi_ref, rwlo_ref, rb_ref, wg_ref, wu_ref, wd_ref,
         nple_ref, pg_ref, pp_ref, out_ref, hn_s, acc_s, gates_s) = refs
    e = pl.program_id(1)

    @pl.when(e == 0)
    def _():
        hn = _rms(x1_ref[...], nffn_ref[...])
        hi = hn.astype(BF16)
        lo = (hn - hi.astype(F32)).astype(BF16)
        logits = (_dot(hi, rwhi_ref[...]) + _dot(lo, rwhi_ref[...])
                  + _dot(hi, rwlo_ref[...]) + rb_ref[...])
        gates_s[...] = _route(logits)
        hn_s[...] = hi
        acc_s[...] = jnp.zeros_like(acc_s)

    hn = hn_s[...]
    gt = _dot(hn, wg_ref[0])
    up = _dot(hn, wu_ref[0])
    act = (gt * _sigmoid(gt)) * up
    y = _dot(act.astype(BF16), wd_ref[0])
    lane = lax.broadcasted_iota(jnp.int32, gates_s.shape, 1)
    gcol = jnp.sum(jnp.where(lane == e + N_GROUPS, gates_s[...], 0.0), axis=-1, keepdims=True)
    acc_s[...] += y * gcol

    @pl.when(e == pl.num_programs(1) - 1)
    def _():
        x2 = x1_ref[...] + acc_s[...]
        hp = _rms(x2, nple_ref[...]).astype(BF16)
        gate = _sigmoid(_dot(hp, pg_ref[...]))
        proj = _dot(p_ref[...].astype(BF16), pp_ref[...])
        x3 = x2 + gate * proj
        out_ref[...] = x3
        if with_kv:
            hk = _rms(x3, kvn_ref[...]).astype(BF16)
            kv = _dot(hk, wkv_ref[...])
            k_ref[...] = _head_norm_rope(kv[:, :LANES], kn_ref[...], cos_ref[...],
                                         sn_ref[...], sp_ref[...])
            v_ref[...] = kv[:, LANES:]


def _moe_layer(x1, p, w, kv=None):
    n, d = x1.shape
    tm = TOKEN_TILE
    nt = n // tm
    tok = lambda i, e: (i, 0)
    const2 = lambda i, e: (0, 0)
    exp3 = lambda i, e: (e, 0, 0)
    in_specs = [
        pl.BlockSpec((tm, d), tok),
        pl.BlockSpec((tm, D_PLE), tok),
        pl.BlockSpec((1, d), const2),
        pl.BlockSpec((d, LANES), const2),
        pl.BlockSpec((d, LANES), const2),
        pl.BlockSpec((1, LANES), const2),
        pl.BlockSpec((1, d, D_EXPERT), exp3),
        pl.BlockSpec((1, d, D_EXPERT), exp3),
        pl.BlockSpec((1, D_EXPERT, d), exp3),
        pl.BlockSpec((1, d), const2),
        pl.BlockSpec((d, d), const2),
        pl.BlockSpec((D_PLE, d), const2),
    ]
    args = [x1, p, w["norm_ffn"], w["router_hi"], w["router_lo"], w["router_b"],
            w["exp_gate"], w["exp_up"], w["exp_down"], w["norm_ple"], w["ple_gate"], w["ple_proj"]]
    out_specs = [pl.BlockSpec((tm, d), tok)]
    out_shape = [jax.ShapeDtypeStruct((n, d), F32)]
    if kv is not None:
        tab_blocks = kv["cos"].shape[0] // tm
        tab = lambda i, e: (i % tab_blocks, 0)
        in_specs += [
            pl.BlockSpec((1, d), const2),
            pl.BlockSpec((d, 2 * LANES), const2),
            pl.BlockSpec((1, LANES), const2),
            pl.BlockSpec((tm, LANES), tab),
            pl.BlockSpec((tm, LANES), tab),
            pl.BlockSpec((tm, LANES), tab),
        ]
        args += [kv["kv_norm"], kv["w_kv"], kv["k_norm"], kv["cos"], kv["sin_next"], kv["sin_prev"]]
        out_specs += [pl.BlockSpec((tm, LANES), tok), pl.BlockSpec((tm, LANES), tok)]
        out_shape += [jax.ShapeDtypeStruct((n, LANES), F32), jax.ShapeDtypeStruct((n, LANES), F32)]
    return pl.pallas_call(
        functools.partial(_moe_kernel, kv is not None),
        grid=(nt, N_EXPERTS),
        in_specs=in_specs,
        out_specs=out_specs,
        out_shape=out_shape,
        scratch_shapes=[
            pltpu.VMEM((tm, d), BF16),
            pltpu.VMEM((tm, d), F32),
            pltpu.VMEM((tm, LANES), F32),
        ],
        compiler_params=pltpu.CompilerParams(
            dimension_semantics=("arbitrary", "arbitrary"), vmem_limit_bytes=VMEM_LIMIT),
        name="moe_kv" if kv is not None else "moe",
    )(*args)


def _project_q(x3, nmix_ref, wq_ref, qn_ref, cos, sin_next, sin_prev, q_s):
    h = _rms(x3, nmix_ref[...]).astype(BF16)
    q = _dot(h, wq_ref[...])
    for c in range(D_MODEL // LANES):
        cols = slice(c * LANES, (c + 1) * LANES)
        qc = _head_norm_rope(q[:, cols], qn_ref[...], cos, sin_next, sin_prev)
        q_s[:, cols] = (qc * (HEAD_DIM ** -0.5)).astype(q_s.dtype)


def _dup_heads(x):
    lane = lax.broadcasted_iota(jnp.int32, x.shape, 1)
    swapped = pltpu.roll(x, HEAD_DIM, 1)
    first = lane < HEAD_DIM
    return jnp.where(first, x, swapped), jnp.where(first, swapped, x)


def _softmax_sink(s, mask, sink):
    s = jnp.where(mask, s, NEG_BIG)
    m = jnp.maximum(jnp.max(s, axis=-1, keepdims=True), sink)
    p = jnp.exp(s - m)
    den = jnp.sum(p, axis=-1, keepdims=True) + jnp.exp(sink - m)
    return (p / den).astype(BF16)


def _attn_prompt_kernel(x3_ref, k_ref, v_ref, kprev_ref, vprev_ref, nmix_ref, wq_ref, qn_ref,
                        cos_ref, sn_ref, sp_ref, sinks_ref, wo_ref,
                        x4_ref, q_s, o_s, kd_s, vd_s):
    t = pl.program_id(1)
    tq = x3_ref.shape[1]
    x3 = x3_ref[0]
    _project_q(x3, nmix_ref, wq_ref, qn_ref, cos_ref[...], sn_ref[...], sp_ref[...], q_s)

    for src_prev, src, dst in ((kprev_ref, k_ref, kd_s), (vprev_ref, v_ref, vd_s)):
        a0, a1 = _dup_heads(src_prev[0])
        dst[0, 0:WINDOW, :] = a0.astype(BF16)
        dst[1, 0:WINDOW, :] = a1.astype(BF16)
        b0, b1 = _dup_heads(src[0])
        dst[0, WINDOW:, :] = b0.astype(BF16)
        dst[1, WINDOW:, :] = b1.astype(BF16)

    qi = lax.broadcasted_iota(jnp.int32, (WINDOW, 2 * WINDOW), 0)
    ki = lax.broadcasted_iota(jnp.int32, (WINDOW, 2 * WINDOW), 1)
    band = (ki > qi) & (ki <= qi + WINDOW)
    lane = lax.broadcasted_iota(jnp.int32, (WINDOW, LANES), 1)
    first = lane < HEAD_DIM

    def q_block(j, carry):
        r0 = pl.multiple_of(j * WINDOW, WINDOW)
        mask = band & (ki >= (1 - j) * WINDOW - t * tq)
        for c in range(D_MODEL // LANES):
            g = (2 * c) // Q_PER_KV
            cols = slice(c * LANES, (c + 1) * LANES)
            qc = q_s[pl.ds(r0, WINDOW), cols]
            keys = kd_s[g, pl.ds(r0, 2 * WINDOW), :]
            vals = vd_s[g, pl.ds(r0, 2 * WINDOW), :]
            outs = []
            for par in range(2):
                qh = jnp.where(first, qc, 0) if par == 0 else jnp.where(first, 0, qc)
                s = _dot_nt(qh.astype(BF16), keys)
                pr = _softmax_sink(s, mask, sinks_ref[2 * c + par])
                outs.append(_dot(pr, vals))
            o_s[pl.ds(r0, WINDOW), cols] = jnp.where(first, outs[0], outs[1]).astype(BF16)
        return carry

    lax.fori_loop(0, tq // WINDOW, q_block, 0)
    x4_ref[0] = x3 + _dot(o_s[...], wo_ref[...])


def _attn_prompt(x3, k, v, w, tabs):
    b, seq, d = x3.shape
    tq = ATTN_TILE
    nt = seq // tq
    per = tq // WINDOW
    tile = lambda bi, ti: (bi, ti, 0)
    prev = lambda bi, ti: (bi, jnp.maximum(ti * per - 1, 0), 0)
    const2 = lambda bi, ti: (0, 0)
    tab = lambda bi, ti: (ti, 0)
    return pl.pallas_call(
        _attn_prompt_kernel,
        grid=(b, nt),
        in_specs=[
            pl.BlockSpec((1, tq, d), tile),
            pl.BlockSpec((1, tq, LANES), tile),
            pl.BlockSpec((1, tq, LANES), tile),
            pl.BlockSpec((1, WINDOW, LANES), prev),
            pl.BlockSpec((1, WINDOW, LANES), prev),
            pl.BlockSpec((1, d), const2),
            pl.BlockSpec((d, d), const2),
            pl.BlockSpec((1, LANES), const2),
            pl.BlockSpec((tq, LANES), tab),
            pl.BlockSpec((tq, LANES), tab),
            pl.BlockSpec((tq, LANES), tab),
            pl.BlockSpec(memory_space=pltpu.SMEM),
            pl.BlockSpec((d, d), const2),
        ],
        out_specs=pl.BlockSpec((1, tq, d), tile),
        out_shape=jax.ShapeDtypeStruct((b, seq, d), F32),
        scratch_shapes=[
            pltpu.VMEM((tq, d), BF16),
            pltpu.VMEM((tq, d), BF16),
            pltpu.VMEM((N_KV_HEADS, tq + WINDOW, LANES), BF16),
            pltpu.VMEM((N_KV_HEADS, tq + WINDOW, LANES), BF16),
        ],
        compiler_params=pltpu.CompilerParams(
            dimension_semantics=("arbitrary", "arbitrary"), vmem_limit_bytes=VMEM_LIMIT),
        name="attn_prompt",
    )(x3, k, v, k, v, w["norm_mix1"], w["w_q"], w["q_norm"], tabs["cos"], tabs["sin_next"],
      tabs["sin_prev"], w["sinks"], w["w_o"])


def _attn_sample_kernel(x3_ref, knew_ref, vnew_ref, kc_ref, vc_ref, nmix_ref, wq_ref, qn_ref,
                        cos_ref, sn_ref, sp_ref, sinks_ref, wo_ref,
                        x4_ref, q_s, o_s):
    rows = x3_ref.shape[0]
    steps = knew_ref.shape[0] // kc_ref.shape[0]
    x3 = x3_ref[...]
    _project_q(x3, nmix_ref, wq_ref, qn_ref, cos_ref[...], sn_ref[...], sp_ref[...], q_s)

    stack = Q_PER_KV * steps
    ri = lax.broadcasted_iota(jnp.int32, (stack, 2 * WINDOW), 0)
    ki = lax.broadcasted_iota(jnp.int32, (stack, 2 * WINDOW), 1)
    tq = ri % steps
    mask = ((ki < WINDOW) & (ki > tq)) | ((ki >= WINDOW) & ((ki - WINDOW) <= tq))
    head_in_group = lax.broadcasted_iota(jnp.int32, (stack, 1), 0) // steps
    sink_cols = []
    for g in range(N_KV_HEADS):
        col = jnp.zeros((stack, 1), F32)
        for hh in range(Q_PER_KV):
            col = jnp.where(head_in_group == hh, sinks_ref[g * Q_PER_KV + hh], col)
        sink_cols.append(col)
    lane = lax.broadcasted_iota(jnp.int32, (steps, LANES), 1)
    first = lane < HEAD_DIM
    pad = jnp.zeros((WINDOW - steps, LANES), F32)

    def one_seq(b, carry):
        r0 = pl.multiple_of(b * steps, steps)
        kdup = _dup_heads(jnp.concatenate([kc_ref[b], knew_ref[pl.ds(r0, steps), :], pad], axis=0))
        vdup = _dup_heads(jnp.concatenate([vc_ref[b], vnew_ref[pl.ds(r0, steps), :], pad], axis=0))
        for g in range(N_KV_HEADS):
            slabs = []
            for c in range(g * Q_PER_KV // 2, (g + 1) * Q_PER_KV // 2):
                qc = q_s[pl.ds(r0, steps), c * LANES:(c + 1) * LANES]
                slabs.append(jnp.where(first, qc, 0.0))
                slabs.append(jnp.where(first, 0.0, qc))
            qg = jnp.concatenate(slabs, axis=0).astype(BF16)
            s = _dot_nt(qg, kdup[g].astype(BF16))
            pr = _softmax_sink(s, mask, sink_cols[g])
            o = _dot(pr, vdup[g].astype(BF16))
            for i in range(Q_PER_KV // 2):
                c = g * Q_PER_KV // 2 + i
                o_s[pl.ds(r0, steps), c * LANES:(c + 1) * LANES] = jnp.where(
                    first, o[(2 * i) * steps:(2 * i + 1) * steps],
                    o[(2 * i + 1) * steps:(2 * i + 2) * steps])
        return carry

    lax.fori_loop(0, rows // steps, one_seq, 0)
    x4_ref[...] = x3 + _dot(o_s[...].astype(BF16), wo_ref[...])


def _attn_sample(x3, knew, vnew, kcache, vcache, w, tabs, steps):
    n, d = x3.shape
    sb = SAMPLE_SEQ_BLOCK
    rows = sb * steps
    tok = lambda i: (i, 0)
    const2 = lambda i: (0, 0)
    return pl.pallas_call(
        _attn_sample_kernel,
        grid=(n // rows,),
        in_specs=[
            pl.BlockSpec((rows, d), tok),
            pl.BlockSpec((rows, LANES), tok),
            pl.BlockSpec((rows, LANES), tok),
            pl.BlockSpec((sb, WINDOW, LANES), lambda i: (i, 0, 0)),
            pl.BlockSpec((sb, WINDOW, LANES), lambda i: (i, 0, 0)),
            pl.BlockSpec((1, d), const2),
            pl.BlockSpec((d, d), const2),
            pl.BlockSpec((1, LANES), const2),
            pl.BlockSpec((rows, LANES), tok),
            pl.BlockSpec((rows, LANES), tok),
            pl.BlockSpec((rows, LANES), tok),
            pl.BlockSpec(memory_space=pltpu.SMEM),
            pl.BlockSpec((d, d), const2),
        ],
        out_specs=pl.BlockSpec((rows, d), tok),
        out_shape=jax.ShapeDtypeStruct((n, d), F32),
        scratch_shapes=[
            pltpu.VMEM((rows, d), F32),
            pltpu.VMEM((rows, d), F32),
        ],
        compiler_params=pltpu.CompilerParams(
            dimension_semantics=("arbitrary",), vmem_limit_bytes=VMEM_LIMIT),
        name="attn_sample",
    )(x3, knew, vnew, kcache, vcache, w["norm_mix1"], w["w_q"], w["q_norm"], tabs["cos"],
      tabs["sin_next"], tabs["sin_prev"], w["sinks"], w["w_o"])


def _rope_tables(pos):
    half = ROPE_DIM // 2
    inv = jnp.float32(ROPE_THETA) ** (-(jnp.arange(half, dtype=jnp.float32) * 2.0 / ROPE_DIM))
    ang = pos.astype(jnp.float32)[:, None] * inv[None, :]
    cos, sin = jnp.cos(ang), jnp.sin(ang)
    ones = jnp.ones((pos.shape[0], HEAD_DIM - ROPE_DIM), F32)
    zeros_h = jnp.zeros((pos.shape[0], half), F32)
    zeros_r = jnp.zeros((pos.shape[0], HEAD_DIM - ROPE_DIM), F32)
    c = jnp.concatenate([cos, cos, ones], axis=1)
    s_next = jnp.concatenate([-sin, zeros_h, zeros_r], axis=1)
    s_prev = jnp.concatenate([zeros_h, sin, zeros_r], axis=1)
    two = lambda a: jnp.concatenate([a, a], axis=1)
    return {"cos": two(c), "sin_next": two(s_next), "sin_prev": two(s_prev)}


def _split_bf16(w):
    hi = w.astype(BF16)
    return hi, (w - hi.astype(F32)).astype(BF16)


def kernel(x_prompt, x_sample, p_prompt, p_sample, state_pool, cache_k_win, cache_v_win, norm_mix, norm_ffn, norm_ple, pool_w, pool_scale, kv_norm, w_kv, k_norm, w_q, q_norm, sinks, w_o, router_g_w, router_g_b, router_e_w, router_e_b, exp_gate, exp_up, exp_down, ple_gate, ple_proj):
    b, seq, d = x_prompt.shape
    sb, steps, _ = x_sample.shape
    depth = norm_mix.shape[0]
    row = lambda v: v.reshape(1, -1)

    layers = []
    for i in range(depth):
        rw = jnp.concatenate([router_g_w[i], router_e_w[i].reshape(d, N_EXPERTS)], axis=1)
        rw = jnp.pad(rw, ((0, 0), (0, LANES - rw.shape[1])))
        rb = jnp.concatenate([router_g_b[i], router_e_b[i].reshape(N_EXPERTS)])
        rb = jnp.pad(rb, (0, LANES - rb.shape[0]))
        hi, lo = _split_bf16(rw)
        layers.append({
            "norm_ffn": row(norm_ffn[i]), "router_hi": hi, "router_lo": lo, "router_b": row(rb),
            "exp_gate": exp_gate[i].astype(BF16), "exp_up": exp_up[i].astype(BF16),
            "exp_down": exp_down[i].astype(BF16), "norm_ple": row(norm_ple[i]),
            "ple_gate": ple_gate[i].astype(BF16), "ple_proj": ple_proj[i].astype(BF16),
        })
    two = lambda a: jnp.concatenate([a, a])
    attn_w = {
        "norm_mix1": row(norm_mix[1]), "w_q": w_q[0].astype(BF16), "q_norm": row(two(q_norm[0])),
        "sinks": sinks[0], "w_o": w_o[0].astype(BF16),
    }
    pw = pool_w[0].astype(BF16)
    nmix0 = row(norm_mix[0])
    pscale = row(pool_scale[0])

    tabs_p = _rope_tables(jnp.arange(seq, dtype=jnp.int32))
    tabs_s = _rope_tables(jnp.tile(PAST_LEN + jnp.arange(steps, dtype=jnp.int32), sb))
    kv_w = {"kv_norm": row(kv_norm), "w_kv": w_kv.astype(BF16), "k_norm": row(two(k_norm))}

    x1p, pool16 = _mixer0_prompt(x_prompt, nmix0, pw, pscale)
    x3p, kp, vp = _moe_layer(x1p.reshape(b * seq, d), p_prompt[0].reshape(b * seq, D_PLE),
                             layers[0], dict(kv_w, **tabs_p))
    kp = kp.reshape(b, seq, LANES)
    vp = vp.reshape(b, seq, LANES)
    x4p = _attn_prompt(x3p.reshape(b, seq, d), kp, vp, attn_w, tabs_p)
    y_prompt = _moe_layer(x4p.reshape(b * seq, d), p_prompt[1].reshape(b * seq, D_PLE),
                          layers[1])[0].reshape(b, seq, d)
    pool_prompt = pool16[None, :, CTX_ROWS - POOL_CTX:, :]
    k_win_prompt = kp[:, seq - WINDOW:].reshape(b, WINDOW, N_KV_HEADS, HEAD_DIM)
    v_win_prompt = vp[:, seq - WINDOW:].reshape(b, WINDOW, N_KV_HEADS, HEAD_DIM)

    ctx16 = jnp.pad(state_pool[0], ((0, 0), (CTX_ROWS - POOL_CTX, 0), (0, 0)))
    x1s, hnew = _mixer0_sample(x_sample, ctx16, nmix0, pw, pscale)
    x3s, ks, vs = _moe_layer(x1s.reshape(sb * steps, d), p_sample[0].reshape(sb * steps, D_PLE),
                             layers[0], dict(kv_w, **tabs_s))
    kc = cache_k_win.reshape(sb, WINDOW, LANES)
    vc = cache_v_win.reshape(sb, WINDOW, LANES)
    x4s = _attn_sample(x3s, ks, vs, kc, vc, attn_w, tabs_s, steps)
    y_sample = _moe_layer(x4s, p_sample[1].reshape(sb * steps, D_PLE),
                          layers[1])[0].reshape(sb, steps, d)
    pool_sample = jnp.concatenate([state_pool[0, :, steps:], hnew], axis=1)[None]
    k_win_sample = jnp.concatenate([kc[:, steps:], ks.reshape(sb, steps, LANES)], axis=1)
    v_win_sample = jnp.concatenate([vc[:, steps:], vs.reshape(sb, steps, LANES)], axis=1)
    k_win_sample = k_win_sample.reshape(sb, WINDOW, N_KV_HEADS, HEAD_DIM)
    v_win_sample = v_win_sample.reshape(sb, WINDOW, N_KV_HEADS, HEAD_DIM)

    return (y_prompt, y_sample, pool_prompt, pool_sample,
            k_win_prompt, v_win_prompt, k_win_sample, v_win_sample)
```

```python
import functools

import jax
import jax.numpy as jnp
from jax import lax
from jax.experimental import pallas as pl
from jax.experimental.pallas import tpu as pltpu

D_MODEL = 1024
PAST_LEN = 8192
POOL_WINDOWS = (2, 4, 8, 16)
POOL_GROUP_DIM = D_MODEL // len(POOL_WINDOWS)
POOL_CTX = max(POOL_WINDOWS) - 1
HEAD_DIM = 64
N_HEADS = D_MODEL // HEAD_DIM
N_KV_HEADS = 2
Q_PER_KV = N_HEADS // N_KV_HEADS
WINDOW = 128
ROPE_DIM = HEAD_DIM // 4
ROPE_THETA = 500000.0
N_GROUPS = 4
E_PER_GROUP = 4
N_EXPERTS = N_GROUPS * E_PER_GROUP
D_EXPERT = 256
D_PLE = 256
EPS = 1e-6

LANES = 128
SUBLANES = 8
CTX_ROWS = 16
TOKEN_TILE = 1024
ATTN_TILE = 512
SAMPLE_SEQ_BLOCK = 16
VMEM_LIMIT = 56 * 1024 * 1024
NEG_BIG = -1e30
F32 = jnp.float32
BF16 = jnp.bfloat16

EXPERT_LANE0 = SUBLANES
MOE_SUBTILE = 256
SEG_ALIGN = 2 * SUBLANES
SORT_ROWS = 768
EXPERT_CHUNK = 128
STAGE_ROWS = 1152
assert SORT_ROWS >= 2 * MOE_SUBTILE + N_EXPERTS * (SEG_ALIGN - 1) and SORT_ROWS % LANES == 0
assert STAGE_ROWS >= TOKEN_TILE + (TOKEN_TILE // MOE_SUBTILE) * (SEG_ALIGN - 1)
assert STAGE_ROWS % EXPERT_CHUNK == 0 and TOKEN_TILE % MOE_SUBTILE == 0


def _rms(x, g):
    return x * lax.rsqrt(jnp.mean(x * x, axis=-1, keepdims=True) + EPS) * g


def _dot(a, b):
    return jnp.dot(a, b, preferred_element_type=F32)


def _dot_nt(a, b):
    return lax.dot_general(a, b, (((1,), (1,)), ((), ())), preferred_element_type=F32)


def _dot_tn(a, b):
    return lax.dot_general(a, b, (((0,), (0,)), ((), ())), preferred_element_type=F32)


def _sigmoid(x):
    return 1.0 / (1.0 + jnp.exp(-x))


def _head_norm_rope(x, gain, cos, sin_next, sin_prev):
    lane = lax.broadcasted_iota(jnp.int32, x.shape, 1)
    first = lane < HEAD_DIM
    sq = x * x
    m0 = jnp.sum(jnp.where(first, sq, 0.0), axis=-1, keepdims=True) * (1.0 / HEAD_DIM)
    m1 = jnp.sum(jnp.where(first, 0.0, sq), axis=-1, keepdims=True) * (1.0 / HEAD_DIM)
    y = x * lax.rsqrt(jnp.where(first, m0, m1) + EPS) * gain
    half = ROPE_DIM // 2
    return (y * cos + pltpu.roll(y, LANES - half, 1) * sin_next
            + pltpu.roll(y, half, 1) * sin_prev)


def _pool_mix(h, hs_read, cnt_of, x, pw_ref, pscale_ref, store):
    for g, win in enumerate(POOL_WINDOWS):
        cols = slice(g * POOL_GROUP_DIM, (g + 1) * POOL_GROUP_DIM)
        hg = h[..., cols]
        s = hg
        for j in range(1, win):
            s = s + hs_read(j, cols)
        pooled = s / cnt_of(win) - hg
        rows = pooled.reshape(-1, POOL_GROUP_DIM).astype(BF16)
        mixed = _dot(rows, pw_ref[g]).reshape(hg.shape)
        store(cols, x[..., cols] + mixed * pscale_ref[:, cols])


def _mixer0_prompt_kernel(x_ref, xprev_ref, nmix_ref, pw_ref, pscale_ref,
                          x1_ref, pool_ref, hs_ref):
    t = pl.program_id(1)
    tm = x_ref.shape[1]
    x = x_ref[0]
    h = _rms(x, nmix_ref[...])
    hp = _rms(xprev_ref[0], nmix_ref[...])
    hs_ref[0:CTX_ROWS, :] = jnp.where(t > 0, hp, 0.0)
    hs_ref[CTX_ROWS:, :] = h
    pos = t * tm + lax.broadcasted_iota(jnp.int32, (tm, 1), 0)

    def hs_read(j, cols):
        return hs_ref[CTX_ROWS - j:CTX_ROWS - j + tm, cols]

    def cnt_of(win):
        return jnp.minimum(win, pos + 1).astype(F32)

    def store(cols, val):
        x1_ref[0, :, cols] = val

    _pool_mix(h, hs_read, cnt_of, x, pw_ref, pscale_ref, store)

    @pl.when(t == pl.num_programs(1) - 1)
    def _():
        pool_ref[0] = hs_ref[tm:tm + CTX_ROWS, :]


def _mixer0_sample_kernel(x_ref, ctx_ref, nmix_ref, pw_ref, pscale_ref,
                          x1_ref, hnew_ref, hs_ref):
    x = x_ref[...]
    steps = x.shape[1]
    h = _rms(x, nmix_ref[...])
    hs_ref[:, 0:CTX_ROWS, :] = ctx_ref[...]
    hs_ref[:, CTX_ROWS:, :] = h
    hnew_ref[...] = h
    pos = PAST_LEN + lax.broadcasted_iota(jnp.int32, (1, steps, 1), 1)

    def hs_read(j, cols):
        return hs_ref[:, CTX_ROWS - j:CTX_ROWS - j + steps, cols]

    def cnt_of(win):
        return jnp.minimum(win, pos + 1).astype(F32)

    def store(cols, val):
        x1_ref[:, :, cols] = val

    _pool_mix(h, hs_read, cnt_of, x, pw_ref, pscale_ref, store)


def _mixer0_prompt(x, nmix, pw, pscale):
    b, seq, d = x.shape
    tm = TOKEN_TILE
    nt = seq // tm
    per = tm // CTX_ROWS
    const2 = lambda bi, ti: (0, 0)
    return pl.pallas_call(
        _mixer0_prompt_kernel,
        grid=(b, nt),
        in_specs=[
            pl.BlockSpec((1, tm, d), lambda bi, ti: (bi, ti, 0)),
            pl.BlockSpec((1, CTX_ROWS, d), lambda bi, ti: (bi, jnp.maximum(ti * per - 1, 0), 0)),
            pl.BlockSpec((1, d), const2),
            pl.BlockSpec(pw.shape, lambda bi, ti: (0, 0, 0)),
            pl.BlockSpec((1, d), const2),
        ],
        out_specs=[
            pl.BlockSpec((1, tm, d), lambda bi, ti: (bi, ti, 0)),
            pl.BlockSpec((1, CTX_ROWS, d), lambda bi, ti: (bi, 0, 0)),
        ],
        out_shape=[
            jax.ShapeDtypeStruct((b, seq, d), F32),
            jax.ShapeDtypeStruct((b, CTX_ROWS, d), F32),
        ],
        scratch_shapes=[pltpu.VMEM((tm + CTX_ROWS, d), F32)],
        compiler_params=pltpu.CompilerParams(
            dimension_semantics=("arbitrary", "arbitrary"), vmem_limit_bytes=VMEM_LIMIT),
        name="mixer0_prompt",
    )(x, x, nmix, pw, pscale)


def _mixer0_sample(x, ctx16, nmix, pw, pscale):
    b, steps, d = x.shape
    sb = SAMPLE_SEQ_BLOCK
    const2 = lambda i: (0, 0)
    return pl.pallas_call(
        _mixer0_sample_kernel,
        grid=(b // sb,),
        in_specs=[
            pl.BlockSpec((sb, steps, d), lambda i: (i, 0, 0)),
            pl.BlockSpec((sb, CTX_ROWS, d), lambda i: (i, 0, 0)),
            pl.BlockSpec((1, d), const2),
            pl.BlockSpec(pw.shape, lambda i: (0, 0, 0)),
            pl.BlockSpec((1, d), const2),
        ],
        out_specs=[
            pl.BlockSpec((sb, steps, d), lambda i: (i, 0, 0)),
            pl.BlockSpec((sb, steps, d), lambda i: (i, 0, 0)),
        ],
        out_shape=[
            jax.ShapeDtypeStruct((b, steps, d), F32),
            jax.ShapeDtypeStruct((b, steps, d), F32),
        ],
        scratch_shapes=[pltpu.VMEM((sb, CTX_ROWS + steps, d), F32)],
        compiler_params=pltpu.CompilerParams(
            dimension_semantics=("arbitrary",), vmem_limit_bytes=VMEM_LIMIT),
        name="mixer0_sample",
    )(x, ctx16, nmix, pw, pscale)


def _route_rows(glog, elog):
    sub8 = lax.broadcasted_iota(jnp.int32, glog.shape, 0).astype(F32)
    is_grp = sub8 < N_GROUPS
    gmax = jnp.max(jnp.where(is_grp, glog, NEG_BIG), axis=0, keepdims=True)
    gsum = jnp.sum(jnp.where(is_grp, jnp.exp(jnp.minimum(glog - gmax, 0.0)), 0.0),
                   axis=0, keepdims=True)
    g_w = 1.0 / gsum
    g_sel = jnp.min(jnp.where(is_grp & (glog == gmax), sub8, float(N_GROUPS)),
                    axis=0, keepdims=True)
    sub16 = lax.broadcasted_iota(jnp.int32, elog.shape, 0).astype(F32)
    lo = E_PER_GROUP * g_sel
    in_grp = (sub16 >= lo) & (sub16 < lo + E_PER_GROUP)
    none = float(N_EXPERTS)
    v1 = jnp.max(jnp.where(in_grp, elog, NEG_BIG), axis=0, keepdims=True)
    i1 = jnp.min(jnp.where(in_grp & (elog == v1), sub16, none), axis=0, keepdims=True)
    rest = in_grp & (sub16 != i1)
    v2 = jnp.max(jnp.where(rest, elog, NEG_BIG), axis=0, keepdims=True)
    i2 = jnp.min(jnp.where(rest & (elog == v2), sub16, none), axis=0, keepdims=True)
    e2 = jnp.exp(v2 - v1)
    den = 1.0 + e2
    return i1, i2, (1.0 / den) * g_w, (e2 / den) * g_w


def _dispatch(x1_ref, nffn_ref, rwhi_ref, rwlo_ref, rb_ref,
              xs_s, perm_s, gsort_s, meta_v, meta_sm, sem):
    tm = x1_ref.shape[0]
    sub = MOE_SUBTILE
    hn = _rms(x1_ref[...], nffn_ref[...])
    hi = hn.astype(BF16)
    lo = (hn - hi.astype(F32)).astype(BF16)
    logits = (_dot(hi, rwhi_ref[...]) + _dot(lo, rwhi_ref[...])
              + _dot(hi, rwlo_ref[...]) + rb_ref[...])
    lt = logits.T
    i1, i2, w1, w2 = _route_rows(lt[0:SUBLANES], lt[EXPERT_LANE0:EXPERT_LANE0 + N_EXPERTS])

    sub16 = lax.broadcasted_iota(jnp.int32, (N_EXPERTS, 2 * sub), 0).astype(F32)
    jr = lax.broadcasted_iota(jnp.int32, (2 * sub, 2 * sub), 0)
    jc = lax.broadcasted_iota(jnp.int32, (2 * sub, 2 * sub), 1)
    earlier = jnp.where(jr < jc, 1.0, 0.0).astype(BF16)
    rsub = lax.broadcasted_iota(jnp.int32, (SORT_ROWS, sub), 0).astype(F32)
    esub = lax.broadcasted_iota(jnp.int32, (N_EXPERTS, LANES), 0)
    mlane = lax.broadcasted_iota(jnp.int32, (N_EXPERTS, LANES), 1)
    meta = jnp.zeros((N_EXPERTS, LANES), F32)
    for s in range(tm // sub):
        sl = slice(s * sub, (s + 1) * sub)
        sel = jnp.concatenate([i1[:, sl], i2[:, sl]], axis=1)
        at = jnp.where(sub16 == sel, 1.0, 0.0)
        cnt = jnp.sum(at, axis=1, keepdims=True)
        ca = jnp.floor((cnt + (SEG_ALIGN - 1)) * (1.0 / SEG_ALIGN)) * SEG_ALIGN
        cab = jnp.broadcast_to(ca, (N_EXPERTS, LANES))
        inc = cab
        for k in (1, 2, 4, 8):
            inc = inc + jnp.where(esub >= k, pltpu.roll(inc, k, 0), 0.0)
        seg0 = inc - cab
        rank = _dot(at.astype(BF16), earlier)
        dest = jnp.sum(at * (seg0[:, 0:1] + rank), axis=0, keepdims=True)
        p1 = rsub == dest[:, :sub]
        p2 = rsub == dest[:, sub:]
        perm = jnp.where(p1 | p2, 1.0, 0.0).astype(BF16)
        gsort_s[s] = jnp.sum(jnp.where(p1, w1[:, sl], 0.0) + jnp.where(p2, w2[:, sl], 0.0),
                             axis=1, keepdims=True)
        perm_s[s] = perm
        xs_s[s * SORT_ROWS:(s + 1) * SORT_ROWS, :] = _dot(perm, hi[sl, :]).astype(BF16)
        meta = jnp.where(mlane == 2 * s, seg0, meta)
        meta = jnp.where(mlane == 2 * s + 1, cab * (1.0 / SEG_ALIGN), meta)
    meta_v[...] = meta.astype(jnp.int32)
    cp = pltpu.make_async_copy(meta_v, meta_sm, sem)
    cp.start()
    cp.wait()


def _copy_blocks(src_ref, src0, dst_ref, dst0, nblocks):
    def body(b, carry):
        so = pl.multiple_of(src0 + b * SEG_ALIGN, SEG_ALIGN)
        do = pl.multiple_of(dst0 + b * SEG_ALIGN, SEG_ALIGN)
        dst_ref[pl.ds(do, SEG_ALIGN), :] = src_ref[pl.ds(so, SEG_ALIGN), :]
        return carry
    lax.fori_loop(0, nblocks, body, 0)


def _expert_step(e, nsub, xs_s, stage_s, meta_sm, wg_ref, wu_ref, wd_ref):
    segs = []
    rows = 0
    for s in range(nsub):
        src = meta_sm[e, 2 * s] + s * SORT_ROWS
        nblocks = meta_sm[e, 2 * s + 1]
        _copy_blocks(xs_s, src, stage_s, rows, nblocks)
        segs.append((src, rows, nblocks))
        rows = rows + nblocks * SEG_ALIGN

    def chunk(c, carry):
        r0 = pl.multiple_of(c * EXPERT_CHUNK, EXPERT_CHUNK)
        x = stage_s[pl.ds(r0, EXPERT_CHUNK), :]
        gt = _dot(x, wg_ref[0])
        up = _dot(x, wu_ref[0])
        act = (gt * _sigmoid(gt)) * up
        stage_s[pl.ds(r0, EXPERT_CHUNK), :] = _dot(act.astype(BF16), wd_ref[0]).astype(BF16)
        return carry

    lax.fori_loop(0, (rows + EXPERT_CHUNK - 1) // EXPERT_CHUNK, chunk, 0)
    for src, dst, nblocks in segs:
        _copy_blocks(stage_s, dst, xs_s, src, nblocks)


def _moe_kernel(with_kv, *refs):
    if with_kv:
        (x1_ref, p_ref, nffn_ref, rwhi_ref, rwlo_ref, rb_ref, wg_ref, wu_ref, wd_ref,
         nple_ref, pg_ref, pp_ref, kvn_ref, wkv_ref, kn_ref, cos_ref, sn_ref, sp_ref,
         out_ref, k_ref, v_ref, xs_s, stage_s, perm_s, gsort_s, meta_v, meta_sm, sem) = refs
    else:
        (x1_ref, p_ref, nffn_ref, rwhi_ref, rwlo_ref, rb_ref, wg_ref, wu_ref, wd_ref,
         nple_ref, pg_ref, pp_ref,
         out_ref, xs_s, stage_s, perm_s, gsort_s, meta_v, meta_sm, sem) = refs
    i = pl.program_id(0)
    e = pl.program_id(1)
    nsub = x1_ref.shape[0] // MOE_SUBTILE

    @pl.when((i == 0) & (e == 0))
    def _():
        stage_s[...] = jnp.zeros_like(stage_s)

    @pl.when(e == 0)
    def _():
        _dispatch(x1_ref, nffn_ref, rwhi_ref, rwlo_ref, rb_ref,
                  xs_s, perm_s, gsort_s, meta_v, meta_sm, sem)

    _expert_step(e, nsub, xs_s, stage_s, meta_sm, wg_ref, wu_ref, wd_ref)

    @pl.when(e == pl.num_programs(1) - 1)
    def _():
        for s in range(nsub):
            sl = slice(s * MOE_SUBTILE, (s + 1) * MOE_SUBTILE)
            ys = xs_s[s * SORT_ROWS:(s + 1) * SORT_ROWS, :].astype(F32) * gsort_s[s]
            out_ref[sl, :] = x1_ref[sl, :] + _dot_tn(perm_s[s], ys.astype(BF16))
        x2 = out_ref[...]
        hp = _rms(x2, nple_ref[...]).astype(BF16)
        gate = _sigmoid(_dot(hp, pg_ref[...]))
        proj = _dot(p_ref[...].astype(BF16), pp_ref[...])
        x3 = x2 + gate * proj
        out_ref[...] = x3
        if with_kv:
            hk = _rms(x3, kvn_ref[...]).astype(BF16)
            kv = _dot(hk, wkv_ref[...])
            k_ref[...] = _head_norm_rope(kv[:, :LANES], kn_ref[...], cos_ref[...],
                                         sn_ref[...], sp_ref[...])
            v_ref[...] = kv[:, LANES:]


def _moe_layer(x1, p, w, kv=None):
    n, d = x1.shape
    tm = TOKEN_TILE
    nt = n // tm
    nsub = tm // MOE_SUBTILE
    tok = lambda i, e: (i, 0)
    const2 = lambda i, e: (0, 0)
    exp3 = lambda i, e: (e, 0, 0)
    in_specs = [
        pl.BlockSpec((tm, d), tok),
        pl.BlockSpec((tm, D_PLE), tok),
        pl.BlockSpec((1, d), const2),
        pl.BlockSpec((d, LANES), const2),
        pl.BlockSpec((d, LANES), const2),
        pl.BlockSpec((1, LANES), const2),
        pl.BlockSpec((1, d, D_EXPERT), exp3),
        pl.BlockSpec((1, d, D_EXPERT), exp3),
        pl.BlockSpec((1, D_EXPERT, d), exp3),
        pl.BlockSpec((1, d), const2),
        pl.BlockSpec((d, d), const2),
        pl.BlockSpec((D_PLE, d), const2),
    ]
    args = [x1, p, w["norm_ffn"], w["router_hi"], w["router_lo"], w["router_b"],
            w["exp_gate"], w["exp_up"], w["exp_down"], w["norm_ple"], w["ple_gate"], w["ple_proj"]]
    out_specs = [pl.BlockSpec((tm, d), tok)]
    out_shape = [jax.ShapeDtypeStruct((n, d), F32)]
    if kv is not None:
        tab_blocks = kv["cos"].shape[0] // tm
        tab = lambda i, e: (i % tab_blocks, 0)
        in_specs += [
            pl.BlockSpec((1, d), const2),
            pl.BlockSpec((d, 2 * LANES), const2),
            pl.BlockSpec((1, LANES), const2),
            pl.BlockSpec((tm, LANES), tab),
            pl.BlockSpec((tm, LANES), tab),
            pl.BlockSpec((tm, LANES), tab),
        ]
        args += [kv["kv_norm"], kv["w_kv"], kv["k_norm"], kv["cos"], kv["sin_next"], kv["sin_prev"]]
        out_specs += [pl.BlockSpec((tm, LANES), tok), pl.BlockSpec((tm, LANES), tok)]
        out_shape += [jax.ShapeDtypeStruct((n, LANES), F32), jax.ShapeDtypeStruct((n, LANES), F32)]
    return pl.pallas_call(
        functools.partial(_moe_kernel, kv is not None),
        grid=(nt, N_EXPERTS),
        in_specs=in_specs,
        out_specs=out_specs,
        out_shape=out_shape,
        scratch_shapes=[
            pltpu.VMEM((nsub * SORT_ROWS, d), BF16),
            pltpu.VMEM((STAGE_ROWS, d), BF16),
            pltpu.VMEM((nsub, SORT_ROWS, MOE_SUBTILE), BF16),
            pltpu.VMEM((nsub, SORT_ROWS, 1), F32),
            pltpu.VMEM((N_EXPERTS, LANES), jnp.int32),
            pltpu.SMEM((N_EXPERTS, LANES), jnp.int32),
            pltpu.SemaphoreType.DMA,
        ],
        compiler_params=pltpu.CompilerParams(
            dimension_semantics=("arbitrary", "arbitrary"), vmem_limit_bytes=VMEM_LIMIT),
        name="moe_kv" if kv is not None else "moe",
    )(*args)


def _project_q(x3, nmix_ref, wq_ref, qn_ref, cos, sin_next, sin_prev, q_s):
    h = _rms(x3, nmix_ref[...]).astype(BF16)
    q = _dot(h, wq_ref[...])
    for c in range(D_MODEL // LANES):
        cols = slice(c * LANES, (c + 1) * LANES)
        qc = _head_norm_rope(q[:, cols], qn_ref[...], cos, sin_next, sin_prev)
        q_s[:, cols] = (qc * (HEAD_DIM ** -0.5)).astype(q_s.dtype)


def _dup_heads(x):
    lane = lax.broadcasted_iota(jnp.int32, x.shape, 1)
    swapped = pltpu.roll(x, HEAD_DIM, 1)
    first = lane < HEAD_DIM
    return jnp.where(first, x, swapped), jnp.where(first, swapped, x)


def _softmax_sink(s, mask, sink):
    s = jnp.where(mask, s, NEG_BIG)
    m = jnp.maximum(jnp.max(s, axis=-1, keepdims=True), sink)
    p = jnp.exp(s - m)
    den = jnp.sum(p, axis=-1, keepdims=True) + jnp.exp(sink - m)
    return (p / den).astype(BF16)


def _attn_prompt_kernel(x3_ref, k_ref, v_ref, kprev_ref, vprev_ref, nmix_ref, wq_ref, qn_ref,
                        cos_ref, sn_ref, sp_ref, sinks_ref, wo_ref,
                        x4_ref, q_s, o_s, kd_s, vd_s):
    t = pl.program_id(1)
    tq = x3_ref.shape[1]
    x3 = x3_ref[0]
    _project_q(x3, nmix_ref, wq_ref, qn_ref, cos_ref[...], sn_ref[...], sp_ref[...], q_s)

    for src_prev, src, dst in ((kprev_ref, k_ref, kd_s), (vprev_ref, v_ref, vd_s)):
        a0, a1 = _dup_heads(src_prev[0])
        dst[0, 0:WINDOW, :] = a0.astype(BF16)
        dst[1, 0:WINDOW, :] = a1.astype(BF16)
        b0, b1 = _dup_heads(src[0])
        dst[0, WINDOW:, :] = b0.astype(BF16)
        dst[1, WINDOW:, :] = b1.astype(BF16)

    qi = lax.broadcasted_iota(jnp.int32, (WINDOW, 2 * WINDOW), 0)
    ki = lax.broadcasted_iota(jnp.int32, (WINDOW, 2 * WINDOW), 1)
    band = (ki > qi) & (ki <= qi + WINDOW)
    lane = lax.broadcasted_iota(jnp.int32, (WINDOW, LANES), 1)
    first = lane < HEAD_DIM

    def q_block(j, carry):
        r0 = pl.multiple_of(j * WINDOW, WINDOW)
        mask = band & (ki >= (1 - j) * WINDOW - t * tq)
        for c in range(D_MODEL // LANES):
            g = (2 * c) // Q_PER_KV
            cols = slice(c * LANES, (c + 1) * LANES)
            qc = q_s[pl.ds(r0, WINDOW), cols]
            keys = kd_s[g, pl.ds(r0, 2 * WINDOW), :]
            vals = vd_s[g, pl.ds(r0, 2 * WINDOW), :]
            outs = []
            for par in range(2):
                qh = jnp.where(first, qc, 0) if par == 0 else jnp.where(first, 0, qc)
                s = _dot_nt(qh.astype(BF16), keys)
                pr = _softmax_sink(s, mask, sinks_ref[2 * c + par])
                outs.append(_dot(pr, vals))
            o_s[pl.ds(r0, WINDOW), cols] = jnp.where(first, outs[0], outs[1]).astype(BF16)
        return carry

    lax.fori_loop(0, tq // WINDOW, q_block, 0)
    x4_ref[0] = x3 + _dot(o_s[...], wo_ref[...])


def _attn_prompt(x3, k, v, w, tabs):
    b, seq, d = x3.shape
    tq = ATTN_TILE
    nt = seq // tq
    per = tq // WINDOW
    tile = lambda bi, ti: (bi, ti, 0)
    prev = lambda bi, ti: (bi, jnp.maximum(ti * per - 1, 0), 0)
    const2 = lambda bi, ti: (0, 0)
    tab = lambda bi, ti: (ti, 0)
    return pl.pallas_call(
        _attn_prompt_kernel,
        grid=(b, nt),
        in_specs=[
            pl.BlockSpec((1, tq, d), tile),
            pl.BlockSpec((1, tq, LANES), tile),
            pl.BlockSpec((1, tq, LANES), tile),
            pl.BlockSpec((1, WINDOW, LANES), prev),
            pl.BlockSpec((1, WINDOW, LANES), prev),
            pl.BlockSpec((1, d), const2),
            pl.BlockSpec((d, d), const2),
            pl.BlockSpec((1, LANES), const2),
            pl.BlockSpec((tq, LANES), tab),
            pl.BlockSpec((tq, LANES), tab),
            pl.BlockSpec((tq, LANES), tab),
            pl.BlockSpec(memory_space=pltpu.SMEM),
            pl.BlockSpec((d, d), const2),
        ],
        out_specs=pl.BlockSpec((1, tq, d), tile),
        out_shape=jax.ShapeDtypeStruct((b, seq, d), F32),
        scratch_shapes=[
            pltpu.VMEM((tq, d), BF16),
            pltpu.VMEM((tq, d), BF16),
            pltpu.VMEM((N_KV_HEADS, tq + WINDOW, LANES), BF16),
            pltpu.VMEM((N_KV_HEADS, tq + WINDOW, LANES), BF16),
        ],
        compiler_params=pltpu.CompilerParams(
            dimension_semantics=("arbitrary", "arbitrary"), vmem_limit_bytes=VMEM_LIMIT),
        name="attn_prompt",
    )(x3, k, v, k, v, w["norm_mix1"], w["w_q"], w["q_norm"], tabs["cos"], tabs["sin_next"],
      tabs["sin_prev"], w["sinks"], w["w_o"])


def _attn_sample_kernel(x3_ref, knew_ref, vnew_ref, kc_ref, vc_ref, nmix_ref, wq_ref, qn_ref,
                        cos_ref, sn_ref, sp_ref, sinks_ref, wo_ref,
                        x4_ref, q_s, o_s):
    rows = x3_ref.shape[0]
    steps = knew_ref.shape[0] // kc_ref.shape[0]
    x3 = x3_ref[...]
    _project_q(x3, nmix_ref, wq_ref, qn_ref, cos_ref[...], sn_ref[...], sp_ref[...], q_s)

    stack = Q_PER_KV * steps
    ri = lax.broadcasted_iota(jnp.int32, (stack, 2 * WINDOW), 0)
    ki = lax.broadcasted_iota(jnp.int32, (stack, 2 * WINDOW), 1)
    tq = ri % steps
    mask = ((ki < WINDOW) & (ki > tq)) | ((ki >= WINDOW) & ((ki - WINDOW) <= tq))
    head_in_group = lax.broadcasted_iota(jnp.int32, (stack, 1), 0) // steps
    sink_cols = []
    for g in range(N_KV_HEADS):
        col = jnp.zeros((stack, 1), F32)
        for hh in range(Q_PER_KV):
            col = jnp.where(head_in_group == hh, sinks_ref[g * Q_PER_KV + hh], col)
        sink_cols.append(col)
    lane = lax.broadcasted_iota(jnp.int32, (steps, LANES), 1)
    first = lane < HEAD_DIM
    pad = jnp.zeros((WINDOW - steps, LANES), F32)

    def one_seq(b, carry):
        r0 = pl.multiple_of(b * steps, steps)
        kdup = _dup_heads(jnp.concatenate([kc_ref[b], knew_ref[pl.ds(r0, steps), :], pad], axis=0))
        vdup = _dup_heads(jnp.concatenate([vc_ref[b], vnew_ref[pl.ds(r0, steps), :], pad], axis=0))
        for g in range(N_KV_HEADS):
            slabs = []
            for c in range(g * Q_PER_KV // 2, (g + 1) * Q_PER_KV // 2):
                qc = q_s[pl.ds(r0, steps), c * LANES:(c + 1) * LANES]
                slabs.append(jnp.where(first, qc, 0.0))
                slabs.append(jnp.where(first, 0.0, qc))
            qg = jnp.concatenate(slabs, axis=0).astype(BF16)
            s = _dot_nt(qg, kdup[g].astype(BF16))
            pr = _softmax_sink(s, mask, sink_cols[g])
            o = _dot(pr, vdup[g].astype(BF16))
            for i in range(Q_PER_KV // 2):
                c = g * Q_PER_KV // 2 + i
                o_s[pl.ds(r0, steps), c * LANES:(c + 1) * LANES] = jnp.where(
                    first, o[(2 * i) * steps:(2 * i + 1) * steps],
                    o[(2 * i + 1) * steps:(2 * i + 2) * steps])
        return carry

    lax.fori_loop(0, rows // steps, one_seq, 0)
    x4_ref[...] = x3 + _dot(o_s[...].astype(BF16), wo_ref[...])


def _attn_sample(x3, knew, vnew, kcache, vcache, w, tabs, steps):
    n, d = x3.shape
    sb = SAMPLE_SEQ_BLOCK
    rows = sb * steps
    tok = lambda i: (i, 0)
    const2 = lambda i: (0, 0)
    return pl.pallas_call(
        _attn_sample_kernel,
        grid=(n // rows,),
        in_specs=[
            pl.BlockSpec((rows, d), tok),
            pl.BlockSpec((rows, LANES), tok),
            pl.BlockSpec((rows, LANES), tok),
            pl.BlockSpec((sb, WINDOW, LANES), lambda i: (i, 0, 0)),
            pl.BlockSpec((sb, WINDOW, LANES), lambda i: (i, 0, 0)),
            pl.BlockSpec((1, d), const2),
            pl.BlockSpec((d, d), const2),
            pl.BlockSpec((1, LANES), const2),
            pl.BlockSpec((rows, LANES), tok),
            pl.BlockSpec((rows, LANES), tok),
            pl.BlockSpec((rows, LANES), tok),
            pl.BlockSpec(memory_space=pltpu.SMEM),
            pl.BlockSpec((d, d), const2),
        ],
        out_specs=pl.BlockSpec((rows, d), tok),
        out_shape=jax.ShapeDtypeStruct((n, d), F32),
        scratch_shapes=[
            pltpu.VMEM((rows, d), F32),
            pltpu.VMEM((rows, d), F32),
        ],
        compiler_params=pltpu.CompilerParams(
            dimension_semantics=("arbitrary",), vmem_limit_bytes=VMEM_LIMIT),
        name="attn_sample",
    )(x3, knew, vnew, kcache, vcache, w["norm_mix1"], w["w_q"], w["q_norm"], tabs["cos"],
      tabs["sin_next"], tabs["sin_prev"], w["sinks"], w["w_o"])


def _rope_tables(pos):
    half = ROPE_DIM // 2
    inv = jnp.float32(ROPE_THETA) ** (-(jnp.arange(half, dtype=jnp.float32) * 2.0 / ROPE_DIM))
    ang = pos.astype(jnp.float32)[:, None] * inv[None, :]
    cos, sin = jnp.cos(ang), jnp.sin(ang)
    ones = jnp.ones((pos.shape[0], HEAD_DIM - ROPE_DIM), F32)
    zeros_h = jnp.zeros((pos.shape[0], half), F32)
    zeros_r = jnp.zeros((pos.shape[0], HEAD_DIM - ROPE_DIM), F32)
    c = jnp.concatenate([cos, cos, ones], axis=1)
    s_next = jnp.concatenate([-sin, zeros_h, zeros_r], axis=1)
    s_prev = jnp.concatenate([zeros_h, sin, zeros_r], axis=1)
    two = lambda a: jnp.concatenate([a, a], axis=1)
    return {"cos": two(c), "sin_next": two(s_next), "sin_prev": two(s_prev)}


def _split_bf16(w):
    hi = w.astype(BF16)
    return hi, (w - hi.astype(F32)).astype(BF16)


def kernel(x_prompt, x_sample, p_prompt, p_sample, state_pool, cache_k_win, cache_v_win, norm_mix, norm_ffn, norm_ple, pool_w, pool_scale, kv_norm, w_kv, k_norm, w_q, q_norm, sinks, w_o, router_g_w, router_g_b, router_e_w, router_e_b, exp_gate, exp_up, exp_down, ple_gate, ple_proj):
    b, seq, d = x_prompt.shape
    sb, steps, _ = x_sample.shape
    depth = norm_mix.shape[0]
    row = lambda v: v.reshape(1, -1)

    layers = []
    for i in range(depth):
        rw = jnp.concatenate([router_g_w[i], jnp.zeros((d, EXPERT_LANE0 - N_GROUPS), F32),
                              router_e_w[i].reshape(d, N_EXPERTS)], axis=1)
        rw = jnp.pad(rw, ((0, 0), (0, LANES - rw.shape[1])))
        rb = jnp.concatenate([router_g_b[i], jnp.zeros((EXPERT_LANE0 - N_GROUPS,), F32),
                              router_e_b[i].reshape(N_EXPERTS)])
        rb = jnp.pad(rb, (0, LANES - rb.shape[0]))
        hi, lo = _split_bf16(rw)
        layers.append({
            "norm_ffn": row(norm_ffn[i]), "router_hi": hi, "router_lo": lo, "router_b": row(rb),
            "exp_gate": exp_gate[i].astype(BF16), "exp_up": exp_up[i].astype(BF16),
            "exp_down": exp_down[i].astype(BF16), "norm_ple": row(norm_ple[i]),
            "ple_gate": ple_gate[i].astype(BF16), "ple_proj": ple_proj[i].astype(BF16),
        })
    two = lambda a: jnp.concatenate([a, a])
    attn_w = {
        "norm_mix1": row(norm_mix[1]), "w_q": w_q[0].astype(BF16), "q_norm": row(two(q_norm[0])),
        "sinks": sinks[0], "w_o": w_o[0].astype(BF16),
    }
    pw = pool_w[0].astype(BF16)
    nmix0 = row(norm_mix[0])
    pscale = row(pool_scale[0])

    tabs_p = _rope_tables(jnp.arange(seq, dtype=jnp.int32))
    tabs_s = _rope_tables(jnp.tile(PAST_LEN + jnp.arange(steps, dtype=jnp.int32), sb))
    kv_w = {"kv_norm": row(kv_norm), "w_kv": w_kv.astype(BF16), "k_norm": row(two(k_norm))}

    x1p, pool16 = _mixer0_prompt(x_prompt, nmix0, pw, pscale)
    x3p, kp, vp = _moe_layer(x1p.reshape(b * seq, d), p_prompt[0].reshape(b * seq, D_PLE),
                             layers[0], dict(kv_w, **tabs_p))
    kp = kp.reshape(b, seq, LANES)
    vp = vp.reshape(b, seq, LANES)
    x4p = _attn_prompt(x3p.reshape(b, seq, d), kp, vp, attn_w, tabs_p)
    y_prompt = _moe_layer(x4p.reshape(b * seq, d), p_prompt[1].reshape(b * seq, D_PLE),
                          layers[1])[0].reshape(b, seq, d)
    pool_prompt = pool16[None, :, CTX_ROWS - POOL_CTX:, :]
    k_win_prompt = kp[:, seq - WINDOW:].reshape(b, WINDOW, N_KV_HEADS, HEAD_DIM)
    v_win_prompt = vp[:, seq - WINDOW:].reshape(b, WINDOW, N_KV_HEADS, HEAD_DIM)

    ctx16 = jnp.pad(state_pool[0], ((0, 0), (CTX_ROWS - POOL_CTX, 0), (0, 0)))
    x1s, hnew = _mixer0_sample(x_sample, ctx16, nmix0, pw, pscale)
    x3s, ks, vs = _moe_layer(x1s.reshape(sb * steps, d), p_sample[0].reshape(sb * steps, D_PLE),
                             layers[0], dict(kv_w, **tabs_s))
    kc = cache_k_win.reshape(sb, WINDOW, LANES)
    vc = cache_v_win.reshape(sb, WINDOW, LANES)
    x4s = _attn_sample(x3s, ks, vs, kc, vc, attn_w, tabs_s, steps)
    y_sample = _moe_layer(x4s, p_sample[1].reshape(sb * steps, D_PLE),
                          layers[1])[0].reshape(sb, steps, d)
    pool_sample = jnp.concatenate([state_pool[0, :, steps:], hnew], axis=1)[None]
    k_win_sample = jnp.concatenate([kc[:, steps:], ks.reshape(sb, steps, LANES)], axis=1)
    v_win_sample = jnp.concatenate([vc[:, steps:], vs.reshape(sb, steps, LANES)], axis=1)
    k_win_sample = k_win_sample.reshape(sb, WINDOW, N_KV_HEADS, HEAD_DIM)
    v_win_sample = v_win_sample.reshape(sb, WINDOW, N_KV_HEADS, HEAD_DIM)

    return (y_prompt, y_sample, pool_prompt, pool_sample,
            k_win_prompt, v_win_prompt, k_win_sample, v_win_sample)
```

```python
import functools

import jax
import jax.numpy as jnp
from jax import lax
from jax.experimental import pallas as pl
from jax.experimental.pallas import tpu as pltpu

D_MODEL = 1024
PAST_LEN = 8192
POOL_WINDOWS = (2, 4, 8, 16)
POOL_GROUP_DIM = D_MODEL // len(POOL_WINDOWS)
POOL_CTX = max(POOL_WINDOWS) - 1
HEAD_DIM = 64
N_HEADS = D_MODEL // HEAD_DIM
N_KV_HEADS = 2
Q_PER_KV = N_HEADS // N_KV_HEADS
WINDOW = 128
ROPE_DIM = HEAD_DIM // 4
ROPE_THETA = 500000.0
N_GROUPS = 4
E_PER_GROUP = 4
N_EXPERTS = N_GROUPS * E_PER_GROUP
D_EXPERT = 256
D_PLE = 256
EPS = 1e-6

LANES = 128
SUBLANES = 8
CTX_ROWS = 16
TOKEN_TILE = 1024
ATTN_TILE = 512
SAMPLE_SEQ_BLOCK = 16
VMEM_LIMIT = 56 * 1024 * 1024
NEG_BIG = -1e30
F32 = jnp.float32
BF16 = jnp.bfloat16

EXPERT_LANE0 = SUBLANES
MOE_SUBTILE = 256
SEG_ALIGN = 2 * SUBLANES
SORT_ROWS = 768
EXPERT_CHUNK = 256
EXPERT_STEP = 32
EXPERTS_PER_STEP = 4
STAGE_ROWS = 1152
assert SORT_ROWS >= 2 * MOE_SUBTILE + N_EXPERTS * (SEG_ALIGN - 1) and SORT_ROWS % LANES == 0
assert STAGE_ROWS >= TOKEN_TILE + (TOKEN_TILE // MOE_SUBTILE) * (SEG_ALIGN - 1) + EXPERT_STEP
assert STAGE_ROWS % EXPERT_STEP == 0 and TOKEN_TILE % MOE_SUBTILE == 0
assert EXPERT_CHUNK % EXPERT_STEP == 0 and N_EXPERTS % EXPERTS_PER_STEP == 0


def _rms(x, g):
    return x * lax.rsqrt(jnp.mean(x * x, axis=-1, keepdims=True) + EPS) * g


def _dot(a, b):
    return jnp.dot(a, b, preferred_element_type=F32)


def _dot_nt(a, b):
    return lax.dot_general(a, b, (((1,), (1,)), ((), ())), preferred_element_type=F32)


def _dot_tn(a, b):
    return lax.dot_general(a, b, (((0,), (0,)), ((), ())), preferred_element_type=F32)


def _split_bf16(x):
    hi = x.astype(BF16)
    return hi, (x - hi.astype(F32)).astype(BF16)


def _dot3(a_hi, a_lo, b):
    b_hi, b_lo = _split_bf16(b)
    return _dot(a_hi, b_hi) + _dot(a_lo, b_hi) + _dot(a_hi, b_lo)


def _sigmoid(x):
    return 1.0 / (1.0 + jnp.exp(-x))


def _head_norm_rope(x, gain, cos, sin_next, sin_prev):
    lane = lax.broadcasted_iota(jnp.int32, x.shape, 1)
    first = lane < HEAD_DIM
    sq = x * x
    m0 = jnp.sum(jnp.where(first, sq, 0.0), axis=-1, keepdims=True) * (1.0 / HEAD_DIM)
    m1 = jnp.sum(jnp.where(first, 0.0, sq), axis=-1, keepdims=True) * (1.0 / HEAD_DIM)
    y = x * lax.rsqrt(jnp.where(first, m0, m1) + EPS) * gain
    half = ROPE_DIM // 2
    return (y * cos + pltpu.roll(y, LANES - half, 1) * sin_next
            + pltpu.roll(y, half, 1) * sin_prev)


def _pool_mix(h, hs_read, cnt_of, x, pw_ref, pscale_ref, store):
    for g, win in enumerate(POOL_WINDOWS):
        cols = slice(g * POOL_GROUP_DIM, (g + 1) * POOL_GROUP_DIM)
        hg = h[..., cols]
        s = hg
        for j in range(1, win):
            s = s + hs_read(j, cols)
        pooled = (s / cnt_of(win) - hg).reshape(-1, POOL_GROUP_DIM)
        hi, lo = _split_bf16(pooled)
        mixed = _dot3(hi, lo, pw_ref[g])
        store(cols, x[..., cols] + mixed.reshape(hg.shape) * pscale_ref[:, cols])


def _mixer0_prompt_kernel(x_ref, xprev_ref, nmix_ref, pw_ref, pscale_ref,
                          x1_ref, pool_ref, hs_ref):
    t = pl.program_id(1)
    tm = x_ref.shape[1]
    x = x_ref[0]
    h = _rms(x, nmix_ref[...])
    hp = _rms(xprev_ref[0], nmix_ref[...])
    hs_ref[0:CTX_ROWS, :] = jnp.where(t > 0, hp, 0.0)
    hs_ref[CTX_ROWS:, :] = h
    pos = t * tm + lax.broadcasted_iota(jnp.int32, (tm, 1), 0)

    def hs_read(j, cols):
        return hs_ref[CTX_ROWS - j:CTX_ROWS - j + tm, cols]

    def cnt_of(win):
        return jnp.minimum(win, pos + 1).astype(F32)

    def store(cols, val):
        x1_ref[0, :, cols] = val

    _pool_mix(h, hs_read, cnt_of, x, pw_ref, pscale_ref, store)

    @pl.when(t == pl.num_programs(1) - 1)
    def _():
        pool_ref[0] = hs_ref[tm:tm + CTX_ROWS, :]


def _mixer0_sample_kernel(x_ref, ctx_ref, nmix_ref, pw_ref, pscale_ref,
                          x1_ref, hnew_ref, hs_ref):
    x = x_ref[...]
    steps = x.shape[1]
    h = _rms(x, nmix_ref[...])
    hs_ref[:, 0:CTX_ROWS, :] = ctx_ref[...]
    hs_ref[:, CTX_ROWS:, :] = h
    hnew_ref[...] = h
    pos = PAST_LEN + lax.broadcasted_iota(jnp.int32, (1, steps, 1), 1)

    def hs_read(j, cols):
        return hs_ref[:, CTX_ROWS - j:CTX_ROWS - j + steps, cols]

    def cnt_of(win):
        return jnp.minimum(win, pos + 1).astype(F32)

    def store(cols, val):
        x1_ref[:, :, cols] = val

    _pool_mix(h, hs_read, cnt_of, x, pw_ref, pscale_ref, store)


def _mixer0_prompt(x, nmix, pw, pscale):
    b, seq, d = x.shape
    tm = TOKEN_TILE
    nt = seq // tm
    per = tm // CTX_ROWS
    const2 = lambda bi, ti: (0, 0)
    return pl.pallas_call(
        _mixer0_prompt_kernel,
        grid=(b, nt),
        in_specs=[
            pl.BlockSpec((1, tm, d), lambda bi, ti: (bi, ti, 0)),
            pl.BlockSpec((1, CTX_ROWS, d), lambda bi, ti: (bi, jnp.maximum(ti * per - 1, 0), 0)),
            pl.BlockSpec((1, d), const2),
            pl.BlockSpec(pw.shape, lambda bi, ti: (0, 0, 0)),
            pl.BlockSpec((1, d), const2),
        ],
        out_specs=[
            pl.BlockSpec((1, tm, d), lambda bi, ti: (bi, ti, 0)),
            pl.BlockSpec((1, CTX_ROWS, d), lambda bi, ti: (bi, 0, 0)),
        ],
        out_shape=[
            jax.ShapeDtypeStruct((b, seq, d), F32),
            jax.ShapeDtypeStruct((b, CTX_ROWS, d), F32),
        ],
        scratch_shapes=[pltpu.VMEM((tm + CTX_ROWS, d), F32)],
        compiler_params=pltpu.CompilerParams(
            dimension_semantics=("arbitrary", "arbitrary"), vmem_limit_bytes=VMEM_LIMIT),
        name="mixer0_prompt",
    )(x, x, nmix, pw, pscale)


def _mixer0_sample(x, ctx16, nmix, pw, pscale):
    b, steps, d = x.shape
    sb = SAMPLE_SEQ_BLOCK
    const2 = lambda i: (0, 0)
    return pl.pallas_call(
        _mixer0_sample_kernel,
        grid=(b // sb,),
        in_specs=[
            pl.BlockSpec((sb, steps, d), lambda i: (i, 0, 0)),
            pl.BlockSpec((sb, CTX_ROWS, d), lambda i: (i, 0, 0)),
            pl.BlockSpec((1, d), const2),
            pl.BlockSpec(pw.shape, lambda i: (0, 0, 0)),
            pl.BlockSpec((1, d), const2),
        ],
        out_specs=[
            pl.BlockSpec((sb, steps, d), lambda i: (i, 0, 0)),
            pl.BlockSpec((sb, steps, d), lambda i: (i, 0, 0)),
        ],
        out_shape=[
            jax.ShapeDtypeStruct((b, steps, d), F32),
            jax.ShapeDtypeStruct((b, steps, d), F32),
        ],
        scratch_shapes=[pltpu.VMEM((sb, CTX_ROWS + steps, d), F32)],
        compiler_params=pltpu.CompilerParams(
            dimension_semantics=("arbitrary",), vmem_limit_bytes=VMEM_LIMIT),
        name="mixer0_sample",
    )(x, ctx16, nmix, pw, pscale)


def _route_rows(glog, elog):
    sub8 = lax.broadcasted_iota(jnp.int32, glog.shape, 0).astype(F32)
    is_grp = sub8 < N_GROUPS
    gmax = jnp.max(jnp.where(is_grp, glog, NEG_BIG), axis=0, keepdims=True)
    gsum = jnp.sum(jnp.where(is_grp, jnp.exp(jnp.minimum(glog - gmax, 0.0)), 0.0),
                   axis=0, keepdims=True)
    g_w = 1.0 / gsum
    g_sel = jnp.min(jnp.where(is_grp & (glog == gmax), sub8, float(N_GROUPS)),
                    axis=0, keepdims=True)
    sub16 = lax.broadcasted_iota(jnp.int32, elog.shape, 0).astype(F32)
    lo = E_PER_GROUP * g_sel
    in_grp = (sub16 >= lo) & (sub16 < lo + E_PER_GROUP)
    none = float(N_EXPERTS)
    v1 = jnp.max(jnp.where(in_grp, elog, NEG_BIG), axis=0, keepdims=True)
    i1 = jnp.min(jnp.where(in_grp & (elog == v1), sub16, none), axis=0, keepdims=True)
    rest = in_grp & (sub16 != i1)
    v2 = jnp.max(jnp.where(rest, elog, NEG_BIG), axis=0, keepdims=True)
    i2 = jnp.min(jnp.where(rest & (elog == v2), sub16, none), axis=0, keepdims=True)
    e2 = jnp.exp(v2 - v1)
    den = 1.0 + e2
    return i1, i2, (1.0 / den) * g_w, (e2 / den) * g_w


def _dispatch(x1_ref, nffn_ref, rw_ref, rb_ref,
              xs_s, perm_s, gsort_s, meta_v, meta_sm, sem):
    tm = x1_ref.shape[0]
    sub = MOE_SUBTILE
    hi, lo = _split_bf16(_rms(x1_ref[...], nffn_ref[...]))
    logits = _dot3(hi, lo, rw_ref[...]) + rb_ref[...]
    lt = logits.T
    i1, i2, w1, w2 = _route_rows(lt[0:SUBLANES], lt[EXPERT_LANE0:EXPERT_LANE0 + N_EXPERTS])

    sub16 = lax.broadcasted_iota(jnp.int32, (N_EXPERTS, 2 * sub), 0).astype(F32)
    jr = lax.broadcasted_iota(jnp.int32, (2 * sub, 2 * sub), 0)
    jc = lax.broadcasted_iota(jnp.int32, (2 * sub, 2 * sub), 1)
    earlier = jnp.where(jr < jc, 1.0, 0.0).astype(BF16)
    rsub = lax.broadcasted_iota(jnp.int32, (SORT_ROWS, sub), 0).astype(F32)
    esub = lax.broadcasted_iota(jnp.int32, (N_EXPERTS, LANES), 0)
    mlane = lax.broadcasted_iota(jnp.int32, (N_EXPERTS, LANES), 1)
    meta = jnp.zeros((N_EXPERTS, LANES), F32)
    for s in range(tm // sub):
        sl = slice(s * sub, (s + 1) * sub)
        sel = jnp.concatenate([i1[:, sl], i2[:, sl]], axis=1)
        at = jnp.where(sub16 == sel, 1.0, 0.0)
        cnt = jnp.sum(at, axis=1, keepdims=True)
        ca = jnp.floor((cnt + (SEG_ALIGN - 1)) * (1.0 / SEG_ALIGN)) * SEG_ALIGN
        cab = jnp.broadcast_to(ca, (N_EXPERTS, LANES))
        inc = cab
        for k in (1, 2, 4, 8):
            inc = inc + jnp.where(esub >= k, pltpu.roll(inc, k, 0), 0.0)
        seg0 = inc - cab
        rank = _dot(at.astype(BF16), earlier)
        dest = jnp.sum(at * (seg0[:, 0:1] + rank), axis=0, keepdims=True)
        p1 = rsub == dest[:, :sub]
        p2 = rsub == dest[:, sub:]
        perm = jnp.where(p1 | p2, 1.0, 0.0).astype(BF16)
        gsort_s[s] = jnp.sum(jnp.where(p1, w1[:, sl], 0.0) + jnp.where(p2, w2[:, sl], 0.0),
                             axis=1, keepdims=True)
        perm_s[s] = perm
        xs_s[s * SORT_ROWS:(s + 1) * SORT_ROWS, :] = _dot(perm, hi[sl, :]).astype(BF16)
        meta = jnp.where(mlane == 2 * s, seg0, meta)
        meta = jnp.where(mlane == 2 * s + 1, cab * (1.0 / SEG_ALIGN), meta)
    meta_v[...] = meta.astype(jnp.int32)
    cp = pltpu.make_async_copy(meta_v, meta_sm, sem)
    cp.start()
    cp.wait()


def _copy_blocks(src_ref, src0, dst_ref, dst0, nblocks):
    def body(b, carry):
        so = pl.multiple_of(src0 + b * SEG_ALIGN, SEG_ALIGN)
        do = pl.multiple_of(dst0 + b * SEG_ALIGN, SEG_ALIGN)
        dst_ref[pl.ds(do, SEG_ALIGN), :] = src_ref[pl.ds(so, SEG_ALIGN), :]
        return carry
    lax.fori_loop(0, nblocks, body, 0)


def _expert_step(e, nsub, xs_s, stage_s, meta_sm, wg, wu, wd):
    segs = []
    rows = 0
    for s in range(nsub):
        src = meta_sm[e, 2 * s] + s * SORT_ROWS
        nblocks = meta_sm[e, 2 * s + 1]
        _copy_blocks(xs_s, src, stage_s, rows, nblocks)
        segs.append((src, rows, nblocks))
        rows = rows + nblocks * SEG_ALIGN

    def mlp(r0, m):
        x = stage_s[pl.ds(r0, m), :]
        gt = _dot(x, wg[...])
        up = _dot(x, wu[...])
        act = (gt * _sigmoid(gt)) * up
        stage_s[pl.ds(r0, m), :] = _dot(act.astype(BF16), wd[...]).astype(BF16)

    def chunk(c, carry):
        mlp(pl.multiple_of(c * EXPERT_CHUNK, EXPERT_CHUNK), EXPERT_CHUNK)
        return carry

    full = rows // EXPERT_CHUNK
    lax.fori_loop(0, full, chunk, 0)
    tail0 = pl.multiple_of(full * EXPERT_CHUNK, EXPERT_CHUNK)
    tail = (rows - full * EXPERT_CHUNK + EXPERT_STEP - 1) // EXPERT_STEP
    for k in range(1, EXPERT_CHUNK // EXPERT_STEP + 1):
        @pl.when(tail == k)
        def _():
            mlp(tail0, k * EXPERT_STEP)

    for src, dst, nblocks in segs:
        _copy_blocks(stage_s, dst, xs_s, src, nblocks)


def _moe_kernel(with_kv, *refs):
    if with_kv:
        (x1_ref, p_ref, nffn_ref, rw_ref, rb_ref, wg_ref, wu_ref, wd_ref,
         nple_ref, pg_ref, pp_ref, kvn_ref, wkv_ref, kn_ref, cos_ref, sn_ref, sp_ref,
         out_ref, k_ref, v_ref, xs_s, stage_s, perm_s, gsort_s, meta_v, meta_sm, sem) = refs
    else:
        (x1_ref, p_ref, nffn_ref, rw_ref, rb_ref, wg_ref, wu_ref, wd_ref,
         nple_ref, pg_ref, pp_ref,
         out_ref, xs_s, stage_s, perm_s, gsort_s, meta_v, meta_sm, sem) = refs
    i = pl.program_id(0)
    step = pl.program_id(1)
    nsub = x1_ref.shape[0] // MOE_SUBTILE

    @pl.when((i == 0) & (step == 0))
    def _():
        stage_s[...] = jnp.zeros_like(stage_s)

    @pl.when(step == 0)
    def _():
        _dispatch(x1_ref, nffn_ref, rw_ref, rb_ref,
                  xs_s, perm_s, gsort_s, meta_v, meta_sm, sem)

    for j in range(EXPERTS_PER_STEP):
        _expert_step(step * EXPERTS_PER_STEP + j, nsub, xs_s, stage_s, meta_sm,
                     wg_ref.at[j], wu_ref.at[j], wd_ref.at[j])

    @pl.when(step == pl.num_programs(1) - 1)
    def _():
        for s in range(nsub):
            sl = slice(s * MOE_SUBTILE, (s + 1) * MOE_SUBTILE)
            ys = xs_s[s * SORT_ROWS:(s + 1) * SORT_ROWS, :].astype(F32) * gsort_s[s]
            out_ref[sl, :] = x1_ref[sl, :] + _dot_tn(perm_s[s], ys.astype(BF16))
        x2 = out_ref[...]
        hp = _rms(x2, nple_ref[...]).astype(BF16)
        gate = _sigmoid(_dot(hp, pg_ref[...]))
        proj = _dot(p_ref[...].astype(BF16), pp_ref[...])
        x3 = x2 + gate * proj
        out_ref[...] = x3
        if with_kv:
            hk = _rms(x3, kvn_ref[...]).astype(BF16)
            kv = _dot(hk, wkv_ref[...])
            k_ref[...] = _head_norm_rope(kv[:, :LANES], kn_ref[...], cos_ref[...],
                                         sn_ref[...], sp_ref[...])
            v_ref[...] = kv[:, LANES:]


def _moe_layer(x1, p, w, kv=None):
    n, d = x1.shape
    tm = TOKEN_TILE
    nt = n // tm
    nsub = tm // MOE_SUBTILE
    tok = lambda i, e: (i, 0)
    const2 = lambda i, e: (0, 0)
    exp3 = lambda i, e: (e, 0, 0)
    in_specs = [
        pl.BlockSpec((tm, d), tok),
        pl.BlockSpec((tm, D_PLE), tok),
        pl.BlockSpec((1, d), const2),
        pl.BlockSpec((d, LANES), const2),
        pl.BlockSpec((1, LANES), const2),
        pl.BlockSpec((EXPERTS_PER_STEP, d, D_EXPERT), exp3),
        pl.BlockSpec((EXPERTS_PER_STEP, d, D_EXPERT), exp3),
        pl.BlockSpec((EXPERTS_PER_STEP, D_EXPERT, d), exp3),
        pl.BlockSpec((1, d), const2),
        pl.BlockSpec((d, d), const2),
        pl.BlockSpec((D_PLE, d), const2),
    ]
    args = [x1, p, w["norm_ffn"], w["router_w"], w["router_b"],
            w["exp_gate"], w["exp_up"], w["exp_down"], w["norm_ple"], w["ple_gate"], w["ple_proj"]]
    out_specs = [pl.BlockSpec((tm, d), tok)]
    out_shape = [jax.ShapeDtypeStruct((n, d), F32)]
    if kv is not None:
        tab_blocks = kv["cos"].shape[0] // tm
        tab = lambda i, e: (i % tab_blocks, 0)
        in_specs += [
            pl.BlockSpec((1, d), const2),
            pl.BlockSpec((d, 2 * LANES), const2),
            pl.BlockSpec((1, LANES), const2),
            pl.BlockSpec((tm, LANES), tab),
            pl.BlockSpec((tm, LANES), tab),
            pl.BlockSpec((tm, LANES), tab),
        ]
        args += [kv["kv_norm"], kv["w_kv"], kv["k_norm"], kv["cos"], kv["sin_next"], kv["sin_prev"]]
        out_specs += [pl.BlockSpec((tm, LANES), tok), pl.BlockSpec((tm, LANES), tok)]
        out_shape += [jax.ShapeDtypeStruct((n, LANES), F32), jax.ShapeDtypeStruct((n, LANES), F32)]
    return pl.pallas_call(
        functools.partial(_moe_kernel, kv is not None),
        grid=(nt, N_EXPERTS // EXPERTS_PER_STEP),
        in_specs=in_specs,
        out_specs=out_specs,
        out_shape=out_shape,
        scratch_shapes=[
            pltpu.VMEM((nsub * SORT_ROWS, d), BF16),
            pltpu.VMEM((STAGE_ROWS, d), BF16),
            pltpu.VMEM((nsub, SORT_ROWS, MOE_SUBTILE), BF16),
            pltpu.VMEM((nsub, SORT_ROWS, 1), F32),
            pltpu.VMEM((N_EXPERTS, LANES), jnp.int32),
            pltpu.SMEM((N_EXPERTS, LANES), jnp.int32),
            pltpu.SemaphoreType.DMA,
        ],
        compiler_params=pltpu.CompilerParams(
            dimension_semantics=("arbitrary", "arbitrary"), vmem_limit_bytes=VMEM_LIMIT),
        name="moe_kv" if kv is not None else "moe",
    )(*args)


def _project_q(x3, nmix_ref, wq_ref, qn_ref, cos, sin_next, sin_prev, q_s):
    h = _rms(x3, nmix_ref[...]).astype(BF16)
    q = _dot(h, wq_ref[...])
    for c in range(D_MODEL // LANES):
        cols = slice(c * LANES, (c + 1) * LANES)
        qc = _head_norm_rope(q[:, cols], qn_ref[...], cos, sin_next, sin_prev)
        q_s[:, cols] = (qc * (HEAD_DIM ** -0.5)).astype(q_s.dtype)


def _dup_heads(x):
    lane = lax.broadcasted_iota(jnp.int32, x.shape, 1)
    swapped = pltpu.roll(x, HEAD_DIM, 1)
    first = lane < HEAD_DIM
    return jnp.where(first, x, swapped), jnp.where(first, swapped, x)


def _softmax_sink(s, mask, sink):
    s = jnp.where(mask, s, NEG_BIG)
    m = jnp.maximum(jnp.max(s, axis=-1, keepdims=True), sink)
    p = jnp.exp(s - m)
    den = jnp.sum(p, axis=-1, keepdims=True) + jnp.exp(sink - m)
    return (p / den).astype(BF16)


def _attn_prompt_kernel(x3_ref, k_ref, v_ref, kprev_ref, vprev_ref, nmix_ref, wq_ref, qn_ref,
                        cos_ref, sn_ref, sp_ref, sinks_ref, wo_ref,
                        x4_ref, q_s, o_s, kd_s, vd_s):
    t = pl.program_id(1)
    tq = x3_ref.shape[1]
    x3 = x3_ref[0]
    _project_q(x3, nmix_ref, wq_ref, qn_ref, cos_ref[...], sn_ref[...], sp_ref[...], q_s)

    for src_prev, src, dst in ((kprev_ref, k_ref, kd_s), (vprev_ref, v_ref, vd_s)):
        a0, a1 = _dup_heads(src_prev[0])
        dst[0, 0:WINDOW, :] = a0.astype(BF16)
        dst[1, 0:WINDOW, :] = a1.astype(BF16)
        b0, b1 = _dup_heads(src[0])
        dst[0, WINDOW:, :] = b0.astype(BF16)
        dst[1, WINDOW:, :] = b1.astype(BF16)

    qi = lax.broadcasted_iota(jnp.int32, (WINDOW, 2 * WINDOW), 0)
    ki = lax.broadcasted_iota(jnp.int32, (WINDOW, 2 * WINDOW), 1)
    band = (ki > qi) & (ki <= qi + WINDOW)
    lane = lax.broadcasted_iota(jnp.int32, (WINDOW, LANES), 1)
    first = lane < HEAD_DIM

    def q_block(j, carry):
        r0 = pl.multiple_of(j * WINDOW, WINDOW)
        mask = band & (ki >= (1 - j) * WINDOW - t * tq)
        for c in range(D_MODEL // LANES):
            g = (2 * c) // Q_PER_KV
            cols = slice(c * LANES, (c + 1) * LANES)
            qc = q_s[pl.ds(r0, WINDOW), cols]
            keys = kd_s[g, pl.ds(r0, 2 * WINDOW), :]
            vals = vd_s[g, pl.ds(r0, 2 * WINDOW), :]
            outs = []
            for par in range(2):
                qh = jnp.where(first, qc, 0) if par == 0 else jnp.where(first, 0, qc)
                s = _dot_nt(qh.astype(BF16), keys)
                pr = _softmax_sink(s, mask, sinks_ref[2 * c + par])
                outs.append(_dot(pr, vals))
            o_s[pl.ds(r0, WINDOW), cols] = jnp.where(first, outs[0], outs[1]).astype(BF16)
        return carry

    lax.fori_loop(0, tq // WINDOW, q_block, 0)
    x4_ref[0] = x3 + _dot(o_s[...], wo_ref[...])


def _attn_prompt(x3, k, v, w, tabs):
    b, seq, d = x3.shape
    tq = ATTN_TILE
    nt = seq // tq
    per = tq // WINDOW
    tile = lambda bi, ti: (bi, ti, 0)
    prev = lambda bi, ti: (bi, jnp.maximum(ti * per - 1, 0), 0)
    const2 = lambda bi, ti: (0, 0)
    tab = lambda bi, ti: (ti, 0)
    return pl.pallas_call(
        _attn_prompt_kernel,
        grid=(b, nt),
        in_specs=[
            pl.BlockSpec((1, tq, d), tile),
            pl.BlockSpec((1, tq, LANES), tile),
            pl.BlockSpec((1, tq, LANES), tile),
            pl.BlockSpec((1, WINDOW, LANES), prev),
            pl.BlockSpec((1, WINDOW, LANES), prev),
            pl.BlockSpec((1, d), const2),
            pl.BlockSpec((d, d), const2),
            pl.BlockSpec((1, LANES), const2),
            pl.BlockSpec((tq, LANES), tab),
            pl.BlockSpec((tq, LANES), tab),
            pl.BlockSpec((tq, LANES), tab),
            pl.BlockSpec(memory_space=pltpu.SMEM),
            pl.BlockSpec((d, d), const2),
        ],
        out_specs=pl.BlockSpec((1, tq, d), tile),
        out_shape=jax.ShapeDtypeStruct((b, seq, d), F32),
        scratch_shapes=[
            pltpu.VMEM((tq, d), BF16),
            pltpu.VMEM((tq, d), BF16),
            pltpu.VMEM((N_KV_HEADS, tq + WINDOW, LANES), BF16),
            pltpu.VMEM((N_KV_HEADS, tq + WINDOW, LANES), BF16),
        ],
        compiler_params=pltpu.CompilerParams(
            dimension_semantics=("arbitrary", "arbitrary"), vmem_limit_bytes=VMEM_LIMIT),
        name="attn_prompt",
    )(x3, k, v, k, v, w["norm_mix1"], w["w_q"], w["q_norm"], tabs["cos"], tabs["sin_next"],
      tabs["sin_prev"], w["sinks"], w["w_o"])


def _attn_sample_kernel(x3_ref, knew_ref, vnew_ref, kc_ref, vc_ref, nmix_ref, wq_ref, qn_ref,
                        cos_ref, sn_ref, sp_ref, sinks_ref, wo_ref,
                        x4_ref, q_s, o_s):
    rows = x3_ref.shape[0]
    steps = knew_ref.shape[0] // kc_ref.shape[0]
    x3 = x3_ref[...]
    _project_q(x3, nmix_ref, wq_ref, qn_ref, cos_ref[...], sn_ref[...], sp_ref[...], q_s)

    stack = Q_PER_KV * steps
    ri = lax.broadcasted_iota(jnp.int32, (stack, 2 * WINDOW), 0)
    ki = lax.broadcasted_iota(jnp.int32, (stack, 2 * WINDOW), 1)
    tq = ri % steps
    mask = ((ki < WINDOW) & (ki > tq)) | ((ki >= WINDOW) & ((ki - WINDOW) <= tq))
    head_in_group = lax.broadcasted_iota(jnp.int32, (stack, 1), 0) // steps
    sink_cols = []
    for g in range(N_KV_HEADS):
        col = jnp.zeros((stack, 1), F32)
        for hh in range(Q_PER_KV):
            col = jnp.where(head_in_group == hh, sinks_ref[g * Q_PER_KV + hh], col)
        sink_cols.append(col)
    lane = lax.broadcasted_iota(jnp.int32, (steps, LANES), 1)
    first = lane < HEAD_DIM
    pad = jnp.zeros((WINDOW - steps, LANES), F32)

    def one_seq(b, carry):
        r0 = pl.multiple_of(b * steps, steps)
        kdup = _dup_heads(jnp.concatenate([kc_ref[b], knew_ref[pl.ds(r0, steps), :], pad], axis=0))
        vdup = _dup_heads(jnp.concatenate([vc_ref[b], vnew_ref[pl.ds(r0, steps), :], pad], axis=0))
        for g in range(N_KV_HEADS):
            slabs = []
            for c in range(g * Q_PER_KV // 2, (g + 1) * Q_PER_KV // 2):
                qc = q_s[pl.ds(r0, steps), c * LANES:(c + 1) * LANES]
                slabs.append(jnp.where(first, qc, 0.0))
                slabs.append(jnp.where(first, 0.0, qc))
            qg = jnp.concatenate(slabs, axis=0).astype(BF16)
            s = _dot_nt(qg, kdup[g].astype(BF16))
            pr = _softmax_sink(s, mask, sink_cols[g])
            o = _dot(pr, vdup[g].astype(BF16))
            for i in range(Q_PER_KV // 2):
                c = g * Q_PER_KV // 2 + i
                o_s[pl.ds(r0, steps), c * LANES:(c + 1) * LANES] = jnp.where(
                    first, o[(2 * i) * steps:(2 * i + 1) * steps],
                    o[(2 * i + 1) * steps:(2 * i + 2) * steps])
        return carry

    lax.fori_loop(0, rows // steps, one_seq, 0)
    x4_ref[...] = x3 + _dot(o_s[...].astype(BF16), wo_ref[...])


def _attn_sample(x3, knew, vnew, kcache, vcache, w, tabs, steps):
    n, d = x3.shape
    sb = SAMPLE_SEQ_BLOCK
    rows = sb * steps
    tok = lambda i: (i, 0)
    const2 = lambda i: (0, 0)
    return pl.pallas_call(
        _attn_sample_kernel,
        grid=(n // rows,),
        in_specs=[
            pl.BlockSpec((rows, d), tok),
            pl.BlockSpec((rows, LANES), tok),
            pl.BlockSpec((rows, LANES), tok),
            pl.BlockSpec((sb, WINDOW, LANES), lambda i: (i, 0, 0)),
            pl.BlockSpec((sb, WINDOW, LANES), lambda i: (i, 0, 0)),
            pl.BlockSpec((1, d), const2),
            pl.BlockSpec((d, d), const2),
            pl.BlockSpec((1, LANES), const2),
            pl.BlockSpec((rows, LANES), tok),
            pl.BlockSpec((rows, LANES), tok),
            pl.BlockSpec((rows, LANES), tok),
            pl.BlockSpec(memory_space=pltpu.SMEM),
            pl.BlockSpec((d, d), const2),
        ],
        out_specs=pl.BlockSpec((rows, d), tok),
        out_shape=jax.ShapeDtypeStruct((n, d), F32),
        scratch_shapes=[
            pltpu.VMEM((rows, d), F32),
            pltpu.VMEM((rows, d), F32),
        ],
        compiler_params=pltpu.CompilerParams(
            dimension_semantics=("arbitrary",), vmem_limit_bytes=VMEM_LIMIT),
        name="attn_sample",
    )(x3, knew, vnew, kcache, vcache, w["norm_mix1"], w["w_q"], w["q_norm"], tabs["cos"],
      tabs["sin_next"], tabs["sin_prev"], w["sinks"], w["w_o"])


def _rope_tables(pos):
    half = ROPE_DIM // 2
    inv = jnp.float32(ROPE_THETA) ** (-(jnp.arange(half, dtype=jnp.float32) * 2.0 / ROPE_DIM))
    ang = pos.astype(jnp.float32)[:, None] * inv[None, :]
    cos, sin = jnp.cos(ang), jnp.sin(ang)
    ones = jnp.ones((pos.shape[0], HEAD_DIM - ROPE_DIM), F32)
    zeros_h = jnp.zeros((pos.shape[0], half), F32)
    zeros_r = jnp.zeros((pos.shape[0], HEAD_DIM - ROPE_DIM), F32)
    c = jnp.concatenate([cos, cos, ones], axis=1)
    s_next = jnp.concatenate([-sin, zeros_h, zeros_r], axis=1)
    s_prev = jnp.concatenate([zeros_h, sin, zeros_r], axis=1)
    two = lambda a: jnp.concatenate([a, a], axis=1)
    return {"cos": two(c), "sin_next": two(s_next), "sin_prev": two(s_prev)}


def kernel(x_prompt, x_sample, p_prompt, p_sample, state_pool, cache_k_win, cache_v_win, norm_mix, norm_ffn, norm_ple, pool_w, pool_scale, kv_norm, w_kv, k_norm, w_q, q_norm, sinks, w_o, router_g_w, router_g_b, router_e_w, router_e_b, exp_gate, exp_up, exp_down, ple_gate, ple_proj):
    b, seq, d = x_prompt.shape
    sb, steps, _ = x_sample.shape
    depth = norm_mix.shape[0]
    row = lambda v: v.reshape(1, -1)

    layers = []
    for i in range(depth):
        rw = jnp.concatenate([router_g_w[i], jnp.zeros((d, EXPERT_LANE0 - N_GROUPS), F32),
                              router_e_w[i].reshape(d, N_EXPERTS)], axis=1)
        rw = jnp.pad(rw, ((0, 0), (0, LANES - rw.shape[1])))
        rb = jnp.concatenate([router_g_b[i], jnp.zeros((EXPERT_LANE0 - N_GROUPS,), F32),
                              router_e_b[i].reshape(N_EXPERTS)])
        rb = jnp.pad(rb, (0, LANES - rb.shape[0]))
        layers.append({
            "norm_ffn": row(norm_ffn[i]), "router_w": rw, "router_b": row(rb),
            "exp_gate": exp_gate[i].astype(BF16), "exp_up": exp_up[i].astype(BF16),
            "exp_down": exp_down[i].astype(BF16), "norm_ple": row(norm_ple[i]),
            "ple_gate": ple_gate[i].astype(BF16), "ple_proj": ple_proj[i].astype(BF16),
        })
    two = lambda a: jnp.concatenate([a, a])
    attn_w = {
        "norm_mix1": row(norm_mix[1]), "w_q": w_q[0].astype(BF16), "q_norm": row(two(q_norm[0])),
        "sinks": sinks[0], "w_o": w_o[0].astype(BF16),
    }
    pw = pool_w[0]
    nmix0 = row(norm_mix[0])
    pscale = row(pool_scale[0])

    tabs_p = _rope_tables(jnp.arange(seq, dtype=jnp.int32))
    tabs_s = _rope_tables(jnp.tile(PAST_LEN + jnp.arange(steps, dtype=jnp.int32), sb))
    kv_w = {"kv_norm": row(kv_norm), "w_kv": w_kv.astype(BF16), "k_norm": row(two(k_norm))}

    x1p, pool16 = _mixer0_prompt(x_prompt, nmix0, pw, pscale)
    x3p, kp, vp = _moe_layer(x1p.reshape(b * seq, d), p_prompt[0].reshape(b * seq, D_PLE),
                             layers[0], dict(kv_w, **tabs_p))
    kp = kp.reshape(b, seq, LANES)
    vp = vp.reshape(b, seq, LANES)
    x4p = _attn_prompt(x3p.reshape(b, seq, d), kp, vp, attn_w, tabs_p)
    y_prompt = _moe_layer(x4p.reshape(b * seq, d), p_prompt[1].reshape(b * seq, D_PLE),
                          layers[1])[0].reshape(b, seq, d)
    pool_prompt = pool16[None, :, CTX_ROWS - POOL_CTX:, :]
    k_win_prompt = kp[:, seq - WINDOW:].reshape(b, WINDOW, N_KV_HEADS, HEAD_DIM)
    v_win_prompt = vp[:, seq - WINDOW:].reshape(b, WINDOW, N_KV_HEADS, HEAD_DIM)

    ctx16 = jnp.pad(state_pool[0], ((0, 0), (CTX_ROWS - POOL_CTX, 0), (0, 0)))
    x1s, hnew = _mixer0_sample(x_sample, ctx16, nmix0, pw, pscale)
    x3s, ks, vs = _moe_layer(x1s.reshape(sb * steps, d), p_sample[0].reshape(sb * steps, D_PLE),
                             layers[0], dict(kv_w, **tabs_s))
    kc = cache_k_win.reshape(sb, WINDOW, LANES)
    vc = cache_v_win.reshape(sb, WINDOW, LANES)
    x4s = _attn_sample(x3s, ks, vs, kc, vc, attn_w, tabs_s, steps)
    y_sample = _moe_layer(x4s, p_sample[1].reshape(sb * steps, D_PLE),
                          layers[1])[0].reshape(sb, steps, d)
    pool_sample = jnp.concatenate([state_pool[0, :, steps:], hnew], axis=1)[None]
    k_win_sample = jnp.concatenate([kc[:, steps:], ks.reshape(sb, steps, LANES)], axis=1)
    v_win_sample = jnp.concatenate([vc[:, steps:], vs.reshape(sb, steps, LANES)], axis=1)
    k_win_sample = k_win_sample.reshape(sb, WINDOW, N_KV_HEADS, HEAD_DIM)
    v_win_sample = v_win_sample.reshape(sb, WINDOW, N_KV_HEADS, HEAD_DIM)

    return (y_prompt, y_sample, pool_prompt, pool_sample,
            k_win_prompt, v_win_prompt, k_win_sample, v_win_sample)
```

```python
import functools

import jax
import jax.numpy as jnp
from jax import lax
from jax.experimental import pallas as pl
from jax.experimental.pallas import tpu as pltpu

D_MODEL = 1024
PAST_LEN = 8192
POOL_WINDOWS = (2, 4, 8, 16)
POOL_GROUP_DIM = D_MODEL // len(POOL_WINDOWS)
POOL_CTX = max(POOL_WINDOWS) - 1
HEAD_DIM = 64
N_HEADS = D_MODEL // HEAD_DIM
N_KV_HEADS = 2
Q_PER_KV = N_HEADS // N_KV_HEADS
WINDOW = 128
ROPE_DIM = HEAD_DIM // 4
ROPE_THETA = 500000.0
N_GROUPS = 4
E_PER_GROUP = 4
N_EXPERTS = N_GROUPS * E_PER_GROUP
D_EXPERT = 256
D_PLE = 256
EPS = 1e-6

LANES = 128
SUBLANES = 8
CTX_ROWS = 16
TOKEN_TILE = 1024
ATTN_TILE = 512
SAMPLE_SEQ_BLOCK = 16
SAMPLE_SEQ_UNROLL = 2
VMEM_LIMIT = 56 * 1024 * 1024
NEG_BIG = -1e30
F32 = jnp.float32
BF16 = jnp.bfloat16

EXPERT_LANE0 = SUBLANES
MOE_SUBTILE = 256
SEG_ALIGN = 2 * SUBLANES
SORT_ROWS = 768
EXPERT_CHUNK = 256
EXPERT_STEP = 32
EXPERTS_PER_STEP = 4
STAGE_ROWS = 1152
assert SORT_ROWS >= 2 * MOE_SUBTILE + N_EXPERTS * (SEG_ALIGN - 1) and SORT_ROWS % LANES == 0
assert STAGE_ROWS >= TOKEN_TILE + (TOKEN_TILE // MOE_SUBTILE) * (SEG_ALIGN - 1) + EXPERT_STEP
assert STAGE_ROWS % EXPERT_STEP == 0 and TOKEN_TILE % MOE_SUBTILE == 0
assert EXPERT_CHUNK % EXPERT_STEP == 0 and N_EXPERTS % EXPERTS_PER_STEP == 0


def _rms(x, g):
    return x * lax.rsqrt(jnp.mean(x * x, axis=-1, keepdims=True) + EPS) * g


def _dot(a, b):
    return jnp.dot(a, b, preferred_element_type=F32)


def _dot_nt(a, b):
    return lax.dot_general(a, b, (((1,), (1,)), ((), ())), preferred_element_type=F32)


def _dot_tn(a, b):
    return lax.dot_general(a, b, (((0,), (0,)), ((), ())), preferred_element_type=F32)


def _split_bf16(x):
    hi = x.astype(BF16)
    return hi, (x - hi.astype(F32)).astype(BF16)


def _dot3(a_hi, a_lo, b):
    b_hi, b_lo = _split_bf16(b)
    return _dot(a_hi, b_hi) + _dot(a_lo, b_hi) + _dot(a_hi, b_lo)


def _sigmoid(x):
    return 1.0 / (1.0 + jnp.exp(-x))


def _rope_parts(rope_ref):
    return rope_ref[:, 0:LANES], rope_ref[:, LANES:2 * LANES], rope_ref[:, 2 * LANES:3 * LANES]


def _head_norm_rope(x, gain, cos, sin_next, sin_prev):
    lane = lax.broadcasted_iota(jnp.int32, x.shape, 1)
    first = lane < HEAD_DIM
    sq = x * x
    m0 = jnp.sum(jnp.where(first, sq, 0.0), axis=-1, keepdims=True) * (1.0 / HEAD_DIM)
    m1 = jnp.sum(jnp.where(first, 0.0, sq), axis=-1, keepdims=True) * (1.0 / HEAD_DIM)
    y = x * lax.rsqrt(jnp.where(first, m0, m1) + EPS) * gain
    half = ROPE_DIM // 2
    return (y * cos + pltpu.roll(y, LANES - half, 1) * sin_next
            + pltpu.roll(y, half, 1) * sin_prev)


def _pool_mix(h, hs_read, cnt_of, x, pw_ref, pscale_ref, store):
    for g, win in enumerate(POOL_WINDOWS):
        cols = slice(g * POOL_GROUP_DIM, (g + 1) * POOL_GROUP_DIM)
        hg = h[..., cols]
        s = hg
        for j in range(1, win):
            s = s + hs_read(j, cols)
        pooled = (s / cnt_of(win) - hg).reshape(-1, POOL_GROUP_DIM)
        hi, lo = _split_bf16(pooled)
        mixed = _dot3(hi, lo, pw_ref[g])
        store(cols, x[..., cols] + mixed.reshape(hg.shape) * pscale_ref[:, cols])


def _mixer0_prompt_kernel(x_ref, xprev_ref, nmix_ref, pw_ref, pscale_ref,
                          x1_ref, pool_ref, hs_ref):
    t = pl.program_id(1)
    tm = x_ref.shape[1]
    x = x_ref[0]
    h = _rms(x, nmix_ref[...])
    hp = _rms(xprev_ref[0], nmix_ref[...])
    hs_ref[0:CTX_ROWS, :] = jnp.where(t > 0, hp, 0.0)
    hs_ref[CTX_ROWS:, :] = h
    pos = t * tm + lax.broadcasted_iota(jnp.int32, (tm, 1), 0)

    def hs_read(j, cols):
        return hs_ref[CTX_ROWS - j:CTX_ROWS - j + tm, cols]

    def cnt_of(win):
        return jnp.minimum(win, pos + 1).astype(F32)

    def store(cols, val):
        x1_ref[0, :, cols] = val

    _pool_mix(h, hs_read, cnt_of, x, pw_ref, pscale_ref, store)

    @pl.when(t == pl.num_programs(1) - 1)
    def _():
        pool_ref[0] = hs_ref[tm:tm + CTX_ROWS, :]


def _mixer0_sample_kernel(x_ref, ctx_ref, nmix_ref, pw_ref, pscale_ref,
                          x1_ref, hnew_ref, hs_ref):
    x = x_ref[...]
    steps = x.shape[1]
    h = _rms(x, nmix_ref[...])
    hs_ref[:, CTX_ROWS - POOL_CTX:CTX_ROWS, :] = ctx_ref[...]
    hs_ref[:, CTX_ROWS:, :] = h
    hnew_ref[...] = h
    pos = PAST_LEN + lax.broadcasted_iota(jnp.int32, (1, steps, 1), 1)

    def hs_read(j, cols):
        return hs_ref[:, CTX_ROWS - j:CTX_ROWS - j + steps, cols]

    def cnt_of(win):
        return jnp.minimum(win, pos + 1).astype(F32)

    def store(cols, val):
        x1_ref[:, :, cols] = val

    _pool_mix(h, hs_read, cnt_of, x, pw_ref, pscale_ref, store)


def _mixer0_prompt(x, nmix, pw, pscale):
    b, seq, d = x.shape
    tm = TOKEN_TILE
    nt = seq // tm
    per = tm // CTX_ROWS
    const2 = lambda bi, ti: (0, 0)
    return pl.pallas_call(
        _mixer0_prompt_kernel,
        grid=(b, nt),
        in_specs=[
            pl.BlockSpec((1, tm, d), lambda bi, ti: (bi, ti, 0)),
            pl.BlockSpec((1, CTX_ROWS, d), lambda bi, ti: (bi, jnp.maximum(ti * per - 1, 0), 0)),
            pl.BlockSpec((1, d), const2),
            pl.BlockSpec(pw.shape, lambda bi, ti: (0, 0, 0)),
            pl.BlockSpec((1, d), const2),
        ],
        out_specs=[
            pl.BlockSpec((1, tm, d), lambda bi, ti: (bi, ti, 0)),
            pl.BlockSpec((1, CTX_ROWS, d), lambda bi, ti: (bi, 0, 0)),
        ],
        out_shape=[
            jax.ShapeDtypeStruct((b, seq, d), F32),
            jax.ShapeDtypeStruct((b, CTX_ROWS, d), F32),
        ],
        scratch_shapes=[pltpu.VMEM((tm + CTX_ROWS, d), F32)],
        compiler_params=pltpu.CompilerParams(
            dimension_semantics=("arbitrary", "arbitrary"), vmem_limit_bytes=VMEM_LIMIT),
        name="mixer0_prompt",
    )(x, x, nmix, pw, pscale)


def _mixer0_sample(x, ctx, nmix, pw, pscale):
    b, steps, d = x.shape
    sb = SAMPLE_SEQ_BLOCK
    const2 = lambda i: (0, 0)
    return pl.pallas_call(
        _mixer0_sample_kernel,
        grid=(b // sb,),
        in_specs=[
            pl.BlockSpec((sb, steps, d), lambda i: (i, 0, 0)),
            pl.BlockSpec((sb, POOL_CTX, d), lambda i: (i, 0, 0)),
            pl.BlockSpec((1, d), const2),
            pl.BlockSpec(pw.shape, lambda i: (0, 0, 0)),
            pl.BlockSpec((1, d), const2),
        ],
        out_specs=[
            pl.BlockSpec((sb, steps, d), lambda i: (i, 0, 0)),
            pl.BlockSpec((sb, steps, d), lambda i: (i, 0, 0)),
        ],
        out_shape=[
            jax.ShapeDtypeStruct((b, steps, d), F32),
            jax.ShapeDtypeStruct((b, steps, d), F32),
        ],
        scratch_shapes=[pltpu.VMEM((sb, CTX_ROWS + steps, d), F32)],
        compiler_params=pltpu.CompilerParams(
            dimension_semantics=("arbitrary",), vmem_limit_bytes=VMEM_LIMIT),
        name="mixer0_sample",
    )(x, ctx, nmix, pw, pscale)


def _route_rows(glog, elog):
    sub8 = lax.broadcasted_iota(jnp.int32, glog.shape, 0).astype(F32)
    is_grp = sub8 < N_GROUPS
    gmax = jnp.max(jnp.where(is_grp, glog, NEG_BIG), axis=0, keepdims=True)
    gsum = jnp.sum(jnp.where(is_grp, jnp.exp(jnp.minimum(glog - gmax, 0.0)), 0.0),
                   axis=0, keepdims=True)
    g_w = 1.0 / gsum
    g_sel = jnp.min(jnp.where(is_grp & (glog == gmax), sub8, float(N_GROUPS)),
                    axis=0, keepdims=True)
    sub16 = lax.broadcasted_iota(jnp.int32, elog.shape, 0).astype(F32)
    lo = E_PER_GROUP * g_sel
    in_grp = (sub16 >= lo) & (sub16 < lo + E_PER_GROUP)
    none = float(N_EXPERTS)
    v1 = jnp.max(jnp.where(in_grp, elog, NEG_BIG), axis=0, keepdims=True)
    i1 = jnp.min(jnp.where(in_grp & (elog == v1), sub16, none), axis=0, keepdims=True)
    rest = in_grp & (sub16 != i1)
    v2 = jnp.max(jnp.where(rest, elog, NEG_BIG), axis=0, keepdims=True)
    i2 = jnp.min(jnp.where(rest & (elog == v2), sub16, none), axis=0, keepdims=True)
    e2 = jnp.exp(v2 - v1)
    den = 1.0 + e2
    return i1, i2, (1.0 / den) * g_w, (e2 / den) * g_w


def _dispatch(x1_ref, nffn_ref, rw_ref, rb_ref,
              xs_s, perm_s, gsort_s, meta_v, meta_sm, sem):
    tm = x1_ref.shape[0]
    sub = MOE_SUBTILE
    hi, lo = _split_bf16(_rms(x1_ref[...], nffn_ref[...]))
    w_hi, w_lo = _split_bf16(rw_ref[...])
    wide = _dot(hi, jnp.concatenate([w_hi, w_lo], axis=1))
    logits = wide[:, :LANES] + wide[:, LANES:] + _dot(lo, w_hi) + rb_ref[...]
    lt = logits.T
    i1, i2, w1, w2 = _route_rows(lt[0:SUBLANES], lt[EXPERT_LANE0:EXPERT_LANE0 + N_EXPERTS])

    sub16 = lax.broadcasted_iota(jnp.int32, (N_EXPERTS, 2 * sub), 0).astype(F32)
    jr = lax.broadcasted_iota(jnp.int32, (2 * sub, 2 * sub), 0)
    jc = lax.broadcasted_iota(jnp.int32, (2 * sub, 2 * sub), 1)
    earlier = jnp.where(jr < jc, 1.0, 0.0).astype(BF16)
    rsub = lax.broadcasted_iota(jnp.int32, (SORT_ROWS, sub), 0).astype(F32)
    esub = lax.broadcasted_iota(jnp.int32, (N_EXPERTS, LANES), 0)
    mlane = lax.broadcasted_iota(jnp.int32, (N_EXPERTS, LANES), 1)
    meta = jnp.zeros((N_EXPERTS, LANES), F32)
    for s in range(tm // sub):
        sl = slice(s * sub, (s + 1) * sub)
        sel = jnp.concatenate([i1[:, sl], i2[:, sl]], axis=1)
        at = jnp.where(sub16 == sel, 1.0, 0.0)
        cnt = jnp.sum(at, axis=1, keepdims=True)
        ca = jnp.floor((cnt + (SEG_ALIGN - 1)) * (1.0 / SEG_ALIGN)) * SEG_ALIGN
        cab = jnp.broadcast_to(ca, (N_EXPERTS, LANES))
        inc = cab
        for k in (1, 2, 4, 8):
            inc = inc + jnp.where(esub >= k, pltpu.roll(inc, k, 0), 0.0)
        seg0 = inc - cab
        rank = _dot(at.astype(BF16), earlier)
        dest = jnp.sum(at * (seg0[:, 0:1] + rank), axis=0, keepdims=True)
        p1 = rsub == dest[:, :sub]
        p2 = rsub == dest[:, sub:]
        perm = jnp.where(p1 | p2, 1.0, 0.0).astype(BF16)
        gsort_s[s] = jnp.sum(jnp.where(p1, w1[:, sl], 0.0) + jnp.where(p2, w2[:, sl], 0.0),
                             axis=1, keepdims=True)
        perm_s[s] = perm
        xs_s[s * SORT_ROWS:(s + 1) * SORT_ROWS, :] = _dot(perm, hi[sl, :]).astype(BF16)
        meta = jnp.where(mlane == 2 * s, seg0, meta)
        meta = jnp.where(mlane == 2 * s + 1, cab * (1.0 / SEG_ALIGN), meta)
    meta_v[...] = meta.astype(jnp.int32)
    cp = pltpu.make_async_copy(meta_v, meta_sm, sem)
    cp.start()
    cp.wait()


def _copy_blocks(src_ref, src0, dst_ref, dst0, nblocks):
    def body(b, carry):
        so = pl.multiple_of(src0 + b * SEG_ALIGN, SEG_ALIGN)
        do = pl.multiple_of(dst0 + b * SEG_ALIGN, SEG_ALIGN)
        dst_ref[pl.ds(do, SEG_ALIGN), :] = src_ref[pl.ds(so, SEG_ALIGN), :]
        return carry
    lax.fori_loop(0, nblocks, body, 0)


def _expert_step(e, nsub, xs_s, stage_s, meta_sm, wg, wu, wd):
    segs = []
    rows = 0
    for s in range(nsub):
        src = meta_sm[e, 2 * s] + s * SORT_ROWS
        nblocks = meta_sm[e, 2 * s + 1]
        _copy_blocks(xs_s, src, stage_s, rows, nblocks)
        segs.append((src, rows, nblocks))
        rows = rows + nblocks * SEG_ALIGN

    def mlp(r0, m):
        x = stage_s[pl.ds(r0, m), :]
        gt = _dot(x, wg[...])
        up = _dot(x, wu[...])
        act = (gt * _sigmoid(gt)) * up
        stage_s[pl.ds(r0, m), :] = _dot(act.astype(BF16), wd[...]).astype(BF16)

    def chunk(c, carry):
        mlp(pl.multiple_of(c * EXPERT_CHUNK, EXPERT_CHUNK), EXPERT_CHUNK)
        return carry

    full = rows // EXPERT_CHUNK
    lax.fori_loop(0, full, chunk, 0)
    tail0 = pl.multiple_of(full * EXPERT_CHUNK, EXPERT_CHUNK)
    tail = (rows - full * EXPERT_CHUNK + EXPERT_STEP - 1) // EXPERT_STEP
    for k in range(1, EXPERT_CHUNK // EXPERT_STEP + 1):
        @pl.when(tail == k)
        def _():
            mlp(tail0, k * EXPERT_STEP)

    for src, dst, nblocks in segs:
        _copy_blocks(stage_s, dst, xs_s, src, nblocks)


def _moe_kernel(with_kv, *refs):
    if with_kv:
        (x1_ref, p_ref, nffn_ref, rw_ref, rb_ref, wg_ref, wu_ref, wd_ref,
         nple_ref, pg_ref, pp_ref, kvn_ref, wkv_ref, kn_ref, rope_ref,
         out_ref, k_ref, v_ref, xs_s, stage_s, perm_s, gsort_s, meta_v, meta_sm, sem) = refs
    else:
        (x1_ref, p_ref, nffn_ref, rw_ref, rb_ref, wg_ref, wu_ref, wd_ref,
         nple_ref, pg_ref, pp_ref,
         out_ref, xs_s, stage_s, perm_s, gsort_s, meta_v, meta_sm, sem) = refs
    i = pl.program_id(0)
    step = pl.program_id(1)
    nsub = x1_ref.shape[0] // MOE_SUBTILE

    @pl.when((i == 0) & (step == 0))
    def _():
        stage_s[...] = jnp.zeros_like(stage_s)

    @pl.when(step == 0)
    def _():
        _dispatch(x1_ref, nffn_ref, rw_ref, rb_ref,
                  xs_s, perm_s, gsort_s, meta_v, meta_sm, sem)

    for j in range(EXPERTS_PER_STEP):
        _expert_step(step * EXPERTS_PER_STEP + j, nsub, xs_s, stage_s, meta_sm,
                     wg_ref.at[j], wu_ref.at[j], wd_ref.at[j])

    @pl.when(step == pl.num_programs(1) - 1)
    def _():
        for s in range(nsub):
            sl = slice(s * MOE_SUBTILE, (s + 1) * MOE_SUBTILE)
            ys = xs_s[s * SORT_ROWS:(s + 1) * SORT_ROWS, :].astype(F32) * gsort_s[s]
            out_ref[sl, :] = x1_ref[sl, :] + _dot_tn(perm_s[s], ys.astype(BF16))
        x2 = out_ref[...]
        hp = _rms(x2, nple_ref[...]).astype(BF16)
        gate = _sigmoid(_dot(hp, pg_ref[...]))
        proj = _dot(p_ref[...].astype(BF16), pp_ref[...])
        x3 = x2 + gate * proj
        out_ref[...] = x3
        if with_kv:
            hk = _rms(x3, kvn_ref[...]).astype(BF16)
            kv = _dot(hk, wkv_ref[...])
            k_ref[...] = _head_norm_rope(kv[:, :LANES], kn_ref[...], *_rope_parts(rope_ref))
            v_ref[...] = kv[:, LANES:]


def _moe_layer(x1, p_all, layer, w, kv=None):
    n, d = x1.shape
    tm = TOKEN_TILE
    nt = n // tm
    nsub = tm // MOE_SUBTILE
    tok = lambda i, e: (i, 0)
    const2 = lambda i, e: (0, 0)
    lay3 = lambda i, e: (layer, 0, 0)
    exp4 = lambda i, e: (layer, e, 0, 0)
    in_specs = [
        pl.BlockSpec((tm, d), tok),
        pl.BlockSpec((None, tm, D_PLE), lambda i, e: (layer, i, 0)),
        pl.BlockSpec((None, 1, d), lay3),
        pl.BlockSpec((None, d, LANES), lay3),
        pl.BlockSpec((None, 1, LANES), lay3),
        pl.BlockSpec((None, EXPERTS_PER_STEP, d, D_EXPERT), exp4),
        pl.BlockSpec((None, EXPERTS_PER_STEP, d, D_EXPERT), exp4),
        pl.BlockSpec((None, EXPERTS_PER_STEP, D_EXPERT, d), exp4),
        pl.BlockSpec((None, 1, d), lay3),
        pl.BlockSpec((None, d, d), lay3),
        pl.BlockSpec((None, D_PLE, d), lay3),
    ]
    args = [x1, p_all, w["norm_ffn"], w["router_w"], w["router_b"],
            w["exp_gate"], w["exp_up"], w["exp_down"], w["norm_ple"], w["ple_gate"], w["ple_proj"]]
    out_specs = [pl.BlockSpec((tm, d), tok)]
    out_shape = [jax.ShapeDtypeStruct((n, d), F32)]
    if kv is not None:
        tab_blocks = kv["rope"].shape[0] // tm
        in_specs += [
            pl.BlockSpec((1, d), const2),
            pl.BlockSpec((d, 2 * LANES), const2),
            pl.BlockSpec((1, LANES), const2),
            pl.BlockSpec((tm, 3 * LANES), lambda i, e: (i % tab_blocks, 0)),
        ]
        args += [kv["kv_norm"], kv["w_kv"], kv["k_norm"], kv["rope"]]
        out_specs += [pl.BlockSpec((tm, LANES), tok), pl.BlockSpec((tm, LANES), tok)]
        out_shape += [jax.ShapeDtypeStruct((n, LANES), F32), jax.ShapeDtypeStruct((n, LANES), F32)]
    return pl.pallas_call(
        functools.partial(_moe_kernel, kv is not None),
        grid=(nt, N_EXPERTS // EXPERTS_PER_STEP),
        in_specs=in_specs,
        out_specs=out_specs,
        out_shape=out_shape,
        scratch_shapes=[
            pltpu.VMEM((nsub * SORT_ROWS, d), BF16),
            pltpu.VMEM((STAGE_ROWS, d), BF16),
            pltpu.VMEM((nsub, SORT_ROWS, MOE_SUBTILE), BF16),
            pltpu.VMEM((nsub, SORT_ROWS, 1), F32),
            pltpu.VMEM((N_EXPERTS, LANES), jnp.int32),
            pltpu.SMEM((N_EXPERTS, LANES), jnp.int32),
            pltpu.SemaphoreType.DMA,
        ],
        compiler_params=pltpu.CompilerParams(
            dimension_semantics=("arbitrary", "arbitrary"), vmem_limit_bytes=VMEM_LIMIT),
        name="moe_kv" if kv is not None else "moe",
    )(*args)


def _project_q(x3, nmix_ref, wq_ref, qn_ref, cos, sin_next, sin_prev, q_s):
    h = _rms(x3, nmix_ref[...]).astype(BF16)
    q = _dot(h, wq_ref[...])
    for c in range(D_MODEL // LANES):
        cols = slice(c * LANES, (c + 1) * LANES)
        qc = _head_norm_rope(q[:, cols], qn_ref[...], cos, sin_next, sin_prev)
        q_s[:, cols] = (qc * (HEAD_DIM ** -0.5)).astype(q_s.dtype)


def _both_orders(x):
    return x.astype(BF16), pltpu.roll(x, HEAD_DIM, 1).astype(BF16)


def _kv_order(par, g):
    return 0 if par == g else 1


def _softmax_sink(s, mask, sink):
    s = jnp.where(mask, s, NEG_BIG)
    m = jnp.maximum(jnp.max(s, axis=-1, keepdims=True), sink)
    p = jnp.exp(s - m)
    den = jnp.sum(p, axis=-1, keepdims=True) + jnp.exp(sink - m)
    return p.astype(BF16), 1.0 / den


def _attn_prompt_kernel(x3_ref, k_ref, v_ref, kprev_ref, vprev_ref, nmix_ref, wq_ref, qn_ref,
                        rope_ref, sinks_ref, wo_ref,
                        x4_ref, q_s, o_s, kd_s, vd_s):
    t = pl.program_id(1)
    tq = x3_ref.shape[1]
    x3 = x3_ref[0]
    _project_q(x3, nmix_ref, wq_ref, qn_ref, *_rope_parts(rope_ref), q_s)

    for src_prev, src, dst in ((kprev_ref, k_ref, kd_s), (vprev_ref, v_ref, vd_s)):
        dst[0, 0:WINDOW, :], dst[1, 0:WINDOW, :] = _both_orders(src_prev[0])
        dst[0, WINDOW:, :], dst[1, WINDOW:, :] = _both_orders(src[0])

    qi = lax.broadcasted_iota(jnp.int32, (WINDOW, 2 * WINDOW), 0)
    ki = lax.broadcasted_iota(jnp.int32, (WINDOW, 2 * WINDOW), 1)
    band = (ki > qi) & (ki <= qi + WINDOW)
    lane = lax.broadcasted_iota(jnp.int32, (WINDOW, LANES), 1)
    first = lane < HEAD_DIM

    def q_block(j, carry):
        r0 = pl.multiple_of(j * WINDOW, WINDOW)
        mask = band & (ki >= (1 - j) * WINDOW - t * tq)
        for c in range(D_MODEL // LANES):
            g = (2 * c) // Q_PER_KV
            cols = slice(c * LANES, (c + 1) * LANES)
            qc = q_s[pl.ds(r0, WINDOW), cols]
            outs = []
            for par in range(2):
                order = _kv_order(par, g)
                qh = jnp.where(first, qc, 0) if par == 0 else jnp.where(first, 0, qc)
                s = _dot_nt(qh, kd_s[order, pl.ds(r0, 2 * WINDOW), :])
                pr, inv = _softmax_sink(s, mask, sinks_ref[2 * c + par])
                outs.append(_dot(pr, vd_s[order, pl.ds(r0, 2 * WINDOW), :]) * inv)
            o_s[pl.ds(r0, WINDOW), cols] = jnp.where(first, outs[0], outs[1]).astype(BF16)
        return carry

    lax.fori_loop(0, tq // WINDOW, q_block, 0)
    x4_ref[0] = x3 + _dot(o_s[...], wo_ref[...])


def _attn_prompt(x3, k, v, w, rope):
    b, seq, d = x3.shape
    tq = ATTN_TILE
    nt = seq // tq
    per = tq // WINDOW
    tile = lambda bi, ti: (bi, ti, 0)
    prev = lambda bi, ti: (bi, jnp.maximum(ti * per - 1, 0), 0)
    const2 = lambda bi, ti: (0, 0)
    tab = lambda bi, ti: (ti, 0)
    return pl.pallas_call(
        _attn_prompt_kernel,
        grid=(b, nt),
        in_specs=[
            pl.BlockSpec((1, tq, d), tile),
            pl.BlockSpec((1, tq, LANES), tile),
            pl.BlockSpec((1, tq, LANES), tile),
            pl.BlockSpec((1, WINDOW, LANES), prev),
            pl.BlockSpec((1, WINDOW, LANES), prev),
            pl.BlockSpec((1, d), const2),
            pl.BlockSpec((d, d), const2),
            pl.BlockSpec((1, LANES), const2),
            pl.BlockSpec((tq, 3 * LANES), tab),
            pl.BlockSpec(memory_space=pltpu.SMEM),
            pl.BlockSpec((d, d), const2),
        ],
        out_specs=pl.BlockSpec((1, tq, d), tile),
        out_shape=jax.ShapeDtypeStruct((b, seq, d), F32),
        scratch_shapes=[
            pltpu.VMEM((tq, d), BF16),
            pltpu.VMEM((tq, d), BF16),
            pltpu.VMEM((N_KV_HEADS, tq + WINDOW, LANES), BF16),
            pltpu.VMEM((N_KV_HEADS, tq + WINDOW, LANES), BF16),
        ],
        compiler_params=pltpu.CompilerParams(
            dimension_semantics=("arbitrary", "arbitrary"), vmem_limit_bytes=VMEM_LIMIT),
        name="attn_prompt",
    )(x3, k, v, k, v, w["norm_mix1"], w["w_q"], w["q_norm"], rope, w["sinks"], w["w_o"])


def _attn_sample_kernel(x3_ref, knew_ref, vnew_ref, kc_ref, vc_ref, nmix_ref, wq_ref, qn_ref,
                        rope_ref, sinks_ref, wo_ref,
                        x4_ref, q_s, o_s):
    rows = x3_ref.shape[0]
    steps = knew_ref.shape[0] // kc_ref.shape[0]
    x3 = x3_ref[...]
    _project_q(x3, nmix_ref, wq_ref, qn_ref, *_rope_parts(rope_ref), q_s)

    n_chunks = D_MODEL // LANES
    stack = n_chunks * steps
    ri = lax.broadcasted_iota(jnp.int32, (stack, 2 * WINDOW), 0)
    ki = lax.broadcasted_iota(jnp.int32, (stack, 2 * WINDOW), 1)
    tq = ri % steps
    mask = ((ki < WINDOW) & (ki > tq)) | ((ki >= WINDOW) & ((ki - WINDOW) <= tq))
    def par_of(order, c):
        g = (2 * c) // Q_PER_KV
        return g if order == 0 else 1 - g

    chunk_of_row = lax.broadcasted_iota(jnp.int32, (stack, 1), 0) // steps
    sink_cols = []
    for order in range(2):
        col = jnp.zeros((stack, 1), F32)
        for c in range(n_chunks):
            col = jnp.where(chunk_of_row == c, sinks_ref[2 * c + par_of(order, c)], col)
        sink_cols.append(col)
    lane = lax.broadcasted_iota(jnp.int32, (steps, LANES), 1)
    first = lane < HEAD_DIM
    pad = jnp.zeros((WINDOW - steps, LANES), F32)

    def one_seq(b):
        r0 = pl.multiple_of(b * steps, steps)
        keys = _both_orders(jnp.concatenate([kc_ref[b], knew_ref[pl.ds(r0, steps), :], pad], axis=0))
        vals = _both_orders(jnp.concatenate([vc_ref[b], vnew_ref[pl.ds(r0, steps), :], pad], axis=0))
        qcs = [q_s[pl.ds(r0, steps), c * LANES:(c + 1) * LANES] for c in range(n_chunks)]
        outs = []
        for order in range(2):
            slabs = [jnp.where(first, qc, 0.0) if par_of(order, c) == 0 else jnp.where(first, 0.0, qc)
                     for c, qc in enumerate(qcs)]
            s = _dot_nt(jnp.concatenate(slabs, axis=0).astype(BF16), keys[order])
            pr, inv = _softmax_sink(s, mask, sink_cols[order])
            outs.append(_dot(pr, vals[order]) * inv)
        for c in range(n_chunks):
            lo_half = outs[0] if par_of(0, c) == 0 else outs[1]
            hi_half = outs[1] if par_of(0, c) == 0 else outs[0]
            o_s[pl.ds(r0, steps), c * LANES:(c + 1) * LANES] = jnp.where(
                first, lo_half[c * steps:(c + 1) * steps], hi_half[c * steps:(c + 1) * steps])

    def seq_pair(i, carry):
        for u in range(SAMPLE_SEQ_UNROLL):
            one_seq(i * SAMPLE_SEQ_UNROLL + u)
        return carry

    lax.fori_loop(0, rows // steps // SAMPLE_SEQ_UNROLL, seq_pair, 0)
    x4_ref[...] = x3 + _dot(o_s[...].astype(BF16), wo_ref[...])


def _attn_sample(x3, knew, vnew, kcache, vcache, w, rope, steps):
    n, d = x3.shape
    sb = SAMPLE_SEQ_BLOCK
    rows = sb * steps
    tok = lambda i: (i, 0)
    const2 = lambda i: (0, 0)
    return pl.pallas_call(
        _attn_sample_kernel,
        grid=(n // rows,),
        in_specs=[
            pl.BlockSpec((rows, d), tok),
            pl.BlockSpec((rows, LANES), tok),
            pl.BlockSpec((rows, LANES), tok),
            pl.BlockSpec((sb, WINDOW, LANES), lambda i: (i, 0, 0)),
            pl.BlockSpec((sb, WINDOW, LANES), lambda i: (i, 0, 0)),
            pl.BlockSpec((1, d), const2),
            pl.BlockSpec((d, d), const2),
            pl.BlockSpec((1, LANES), const2),
            pl.BlockSpec((rows, 3 * LANES), tok),
            pl.BlockSpec(memory_space=pltpu.SMEM),
            pl.BlockSpec((d, d), const2),
        ],
        out_specs=pl.BlockSpec((rows, d), tok),
        out_shape=jax.ShapeDtypeStruct((n, d), F32),
        scratch_shapes=[
            pltpu.VMEM((rows, d), F32),
            pltpu.VMEM((rows, d), F32),
        ],
        compiler_params=pltpu.CompilerParams(
            dimension_semantics=("arbitrary",), vmem_limit_bytes=VMEM_LIMIT),
        name="attn_sample",
    )(x3, knew, vnew, kcache, vcache, w["norm_mix1"], w["w_q"], w["q_norm"], rope,
      w["sinks"], w["w_o"])


def _rope_table(pos):
    half = ROPE_DIM // 2
    inv = jnp.float32(ROPE_THETA) ** (-(jnp.arange(half, dtype=jnp.float32) * 2.0 / ROPE_DIM))
    ang = pos.astype(jnp.float32)[:, None] * inv[None, :]
    cos, sin = jnp.cos(ang), jnp.sin(ang)
    ones = jnp.ones((pos.shape[0], HEAD_DIM - ROPE_DIM), F32)
    zeros_h = jnp.zeros((pos.shape[0], half), F32)
    zeros_r = jnp.zeros((pos.shape[0], HEAD_DIM - ROPE_DIM), F32)
    c = [cos, cos, ones]
    s_next = [-sin, zeros_h, zeros_r]
    s_prev = [zeros_h, sin, zeros_r]
    return jnp.concatenate(c + c + s_next + s_next + s_prev + s_prev, axis=1)


def kernel(x_prompt, x_sample, p_prompt, p_sample, state_pool, cache_k_win, cache_v_win, norm_mix, norm_ffn, norm_ple, pool_w, pool_scale, kv_norm, w_kv, k_norm, w_q, q_norm, sinks, w_o, router_g_w, router_g_b, router_e_w, router_e_b, exp_gate, exp_up, exp_down, ple_gate, ple_proj):
    b, seq, d = x_prompt.shape
    sb, steps, _ = x_sample.shape
    depth = norm_mix.shape[0]
    row = lambda v: v.reshape(1, -1)

    gap_w = jnp.zeros((depth, d, EXPERT_LANE0 - N_GROUPS), F32)
    tail_w = jnp.zeros((depth, d, LANES - EXPERT_LANE0 - N_EXPERTS), F32)
    router_w = jnp.concatenate(
        [router_g_w, gap_w, router_e_w.reshape(depth, d, N_EXPERTS), tail_w], axis=2)
    router_b = jnp.concatenate(
        [router_g_b, gap_w[:, 0], router_e_b.reshape(depth, N_EXPERTS), tail_w[:, 0]], axis=1)
    moe_w = {
        "norm_ffn": norm_ffn.reshape(depth, 1, d), "router_w": router_w,
        "router_b": router_b.reshape(depth, 1, LANES),
        "exp_gate": exp_gate.astype(BF16), "exp_up": exp_up.astype(BF16),
        "exp_down": exp_down.astype(BF16), "norm_ple": norm_ple.reshape(depth, 1, d),
        "ple_gate": ple_gate.astype(BF16), "ple_proj": ple_proj.astype(BF16),
    }
    two = lambda a: jnp.concatenate([a, a])
    attn_w = {
        "norm_mix1": row(norm_mix[1]), "w_q": w_q[0].astype(BF16), "q_norm": row(two(q_norm[0])),
        "sinks": sinks[0], "w_o": w_o[0].astype(BF16),
    }
    pw = pool_w[0]
    nmix0 = row(norm_mix[0])
    pscale = row(pool_scale[0])

    rope_p = _rope_table(jnp.arange(seq, dtype=jnp.int32))
    rope_s = jnp.tile(_rope_table(PAST_LEN + jnp.arange(steps, dtype=jnp.int32)), (sb, 1))
    kv_w = {"kv_norm": row(kv_norm), "w_kv": w_kv.astype(BF16), "k_norm": row(two(k_norm))}

    pp_all = p_prompt.reshape(depth, b * seq, D_PLE)
    x1p, pool16 = _mixer0_prompt(x_prompt, nmix0, pw, pscale)
    x3p, kp, vp = _moe_layer(x1p.reshape(b * seq, d), pp_all, 0, moe_w, dict(kv_w, rope=rope_p))
    kp = kp.reshape(b, seq, LANES)
    vp = vp.reshape(b, seq, LANES)
    x4p = _attn_prompt(x3p.reshape(b, seq, d), kp, vp, attn_w, rope_p)
    y_prompt = _moe_layer(x4p.reshape(b * seq, d), pp_all, 1, moe_w)[0].reshape(b, seq, d)
    pool_prompt = pool16[None, :, CTX_ROWS - POOL_CTX:, :]
    k_win_prompt = kp[:, seq - WINDOW:].reshape(b, WINDOW, N_KV_HEADS, HEAD_DIM)
    v_win_prompt = vp[:, seq - WINDOW:].reshape(b, WINDOW, N_KV_HEADS, HEAD_DIM)

    ps_all = p_sample.reshape(depth, sb * steps, D_PLE)
    x1s, hnew = _mixer0_sample(x_sample, state_pool[0], nmix0, pw, pscale)
    x3s, ks, vs = _moe_layer(x1s.reshape(sb * steps, d), ps_all, 0, moe_w, dict(kv_w, rope=rope_s))
    kc = cache_k_win.reshape(sb, WINDOW, LANES)
    vc = cache_v_win.reshape(sb, WINDOW, LANES)
    x4s = _attn_sample(x3s, ks, vs, kc, vc, attn_w, rope_s, steps)
    y_sample = _moe_layer(x4s, ps_all, 1, moe_w)[0].reshape(sb, steps, d)
    pool_sample = jnp.concatenate([state_pool[0, :, steps:], hnew], axis=1)[None]
    k_win_sample = jnp.concatenate([kc[:, steps:], ks.reshape(sb, steps, LANES)], axis=1)
    v_win_sample = jnp.concatenate([vc[:, steps:], vs.reshape(sb, steps, LANES)], axis=1)
    k_win_sample = k_win_sample.reshape(sb, WINDOW, N_KV_HEADS, HEAD_DIM)
    v_win_sample = v_win_sample.reshape(sb, WINDOW, N_KV_HEADS, HEAD_DIM)

    return (y_prompt, y_sample, pool_prompt, pool_sample,
            k_win_prompt, v_win_prompt, k_win_sample, v_win_sample)
```

```python
import functools

import jax
import jax.numpy as jnp
from jax import lax
from jax.experimental import pallas as pl
from jax.experimental.pallas import tpu as pltpu

D_MODEL = 1024
PAST_LEN = 8192
POOL_WINDOWS = (2, 4, 8, 16)
POOL_GROUP_DIM = D_MODEL // len(POOL_WINDOWS)
POOL_CTX = max(POOL_WINDOWS) - 1
assert all(win == 2 ** (g + 1) for g, win in enumerate(POOL_WINDOWS))
HEAD_DIM = 64
N_HEADS = D_MODEL // HEAD_DIM
N_KV_HEADS = 2
Q_PER_KV = N_HEADS // N_KV_HEADS
WINDOW = 128
ROPE_DIM = HEAD_DIM // 4
ROPE_THETA = 500000.0
N_GROUPS = 4
E_PER_GROUP = 4
N_EXPERTS = N_GROUPS * E_PER_GROUP
D_EXPERT = 256
D_PLE = 256
EPS = 1e-6

LANES = 128
SUBLANES = 8
CTX_ROWS = 16
TOKEN_TILE = 1024
ATTN_TILE = 512
SAMPLE_SEQ_BLOCK = 16
SAMPLE_SEQ_UNROLL = 2
VMEM_LIMIT = 56 * 1024 * 1024
NEG_BIG = -1e30
F32 = jnp.float32
BF16 = jnp.bfloat16

EXPERT_LANE0 = SUBLANES
MOE_SUBTILE = 256
SEG_ALIGN = 2 * SUBLANES
SORT_ROWS = 768
EXPERT_CHUNK = 256
EXPERT_STEP = 32
EXPERTS_PER_STEP = 4
STAGE_ROWS = 1152
assert SORT_ROWS >= 2 * MOE_SUBTILE + N_EXPERTS * (SEG_ALIGN - 1) and SORT_ROWS % LANES == 0
assert STAGE_ROWS >= TOKEN_TILE + (TOKEN_TILE // MOE_SUBTILE) * (SEG_ALIGN - 1) + EXPERT_STEP
assert STAGE_ROWS % EXPERT_STEP == 0 and TOKEN_TILE % MOE_SUBTILE == 0
assert EXPERT_CHUNK % EXPERT_STEP == 0 and N_EXPERTS % EXPERTS_PER_STEP == 0


def _rms(x, g):
    return x * lax.rsqrt(jnp.mean(x * x, axis=-1, keepdims=True) + EPS) * g


def _dot(a, b):
    return jnp.dot(a, b, preferred_element_type=F32)


def _dot_nt(a, b):
    return lax.dot_general(a, b, (((1,), (1,)), ((), ())), preferred_element_type=F32)


def _dot_tn(a, b):
    return lax.dot_general(a, b, (((0,), (0,)), ((), ())), preferred_element_type=F32)


def _split_bf16(x):
    hi = x.astype(BF16)
    return hi, (x - hi.astype(F32)).astype(BF16)


def _dot3(a_hi, a_lo, b):
    b_hi, b_lo = _split_bf16(b)
    return _dot(a_hi, b_hi) + _dot(a_lo, b_hi) + _dot(a_hi, b_lo)


def _sigmoid(x):
    return 1.0 / (1.0 + jnp.exp(-x))


def _rope_parts(rope_ref):
    return rope_ref[:, 0:LANES], rope_ref[:, LANES:2 * LANES], rope_ref[:, 2 * LANES:3 * LANES]


def _head_norm_rope(x, gain, cos, sin_next, sin_prev):
    lane = lax.broadcasted_iota(jnp.int32, x.shape, 1)
    first = lane < HEAD_DIM
    sq = x * x
    m0 = jnp.sum(jnp.where(first, sq, 0.0), axis=-1, keepdims=True) * (1.0 / HEAD_DIM)
    m1 = jnp.sum(jnp.where(first, 0.0, sq), axis=-1, keepdims=True) * (1.0 / HEAD_DIM)
    y = x * lax.rsqrt(jnp.where(first, m0, m1) + EPS) * gain
    half = ROPE_DIM // 2
    return (y * cos + pltpu.roll(y, LANES - half, 1) * sin_next
            + pltpu.roll(y, half, 1) * sin_prev)


def _pool_mix(h, window_sum, pos, x, pw_ref, pscale_ref, store):
    for g, win in enumerate(POOL_WINDOWS):
        cols = slice(g * POOL_GROUP_DIM, (g + 1) * POOL_GROUP_DIM)
        hg = h[..., cols]
        cnt = jnp.minimum(win, pos + 1).astype(F32)
        pooled = (window_sum(g, cols) / cnt - hg).reshape(-1, POOL_GROUP_DIM)
        hi, lo = _split_bf16(pooled)
        mixed = _dot3(hi, lo, pw_ref[g])
        store(cols, x[..., cols] + mixed.reshape(hg.shape) * pscale_ref[:, cols])


def _mixer0_prompt_kernel(x_ref, xprev_ref, nmix_ref, pw_ref, pscale_ref,
                          x1_ref, pool_ref, hs_ref, ua_ref, ub_ref):
    t = pl.program_id(1)
    tm = x_ref.shape[1]
    x = x_ref[0]
    h = _rms(x, nmix_ref[...])
    hp = _rms(xprev_ref[0], nmix_ref[...])
    top = 2 * CTX_ROWS
    hs_ref[0:CTX_ROWS, :] = jnp.zeros((CTX_ROWS, x.shape[1]), F32)
    hs_ref[CTX_ROWS:top, :] = jnp.where(t > 0, hp, 0.0)
    hs_ref[top:, :] = h
    ua_ref[0:SUBLANES, :] = jnp.zeros((SUBLANES, x.shape[1]), F32)
    ub_ref[0:SUBLANES, :] = jnp.zeros((SUBLANES, x.shape[1]), F32)

    n = tm + top - SUBLANES
    src, level_of = hs_ref, []
    for g, win in enumerate(POOL_WINDOWS):
        dst = ua_ref if g % 2 == 0 else ub_ref
        c0 = g * POOL_GROUP_DIM
        back = win // 2
        dst[SUBLANES:, c0:] = src[SUBLANES:, c0:] + src[SUBLANES - back:SUBLANES - back + n, c0:]
        level_of.append(dst)
        src = dst

    def window_sum(g, cols):
        return level_of[g][top:, cols]

    def store(cols, val):
        x1_ref[0, :, cols] = val

    pos = t * tm + lax.broadcasted_iota(jnp.int32, (tm, 1), 0)
    _pool_mix(h, window_sum, pos, x, pw_ref, pscale_ref, store)

    @pl.when(t == pl.num_programs(1) - 1)
    def _():
        pool_ref[0] = hs_ref[tm + top - CTX_ROWS:tm + top, :]


def _mixer0_sample_kernel(x_ref, ctx_ref, nmix_ref, pw_ref, pscale_ref,
                          x1_ref, pool_ref, hs_ref):
    x = x_ref[...]
    steps = x.shape[1]
    h = _rms(x, nmix_ref[...])
    hs_ref[:, CTX_ROWS - POOL_CTX:CTX_ROWS, :] = ctx_ref[...]
    hs_ref[:, CTX_ROWS:, :] = h
    pool_ref[:, 0:POOL_CTX - steps, :] = ctx_ref[:, steps:, :]
    pool_ref[:, POOL_CTX - steps:, :] = h

    def window_sum(g, cols):
        s = h[..., cols]
        for j in range(1, POOL_WINDOWS[g]):
            s = s + hs_ref[:, CTX_ROWS - j:CTX_ROWS - j + steps, cols]
        return s

    def store(cols, val):
        x1_ref[:, :, cols] = val

    pos = PAST_LEN + lax.broadcasted_iota(jnp.int32, (1, steps, 1), 1)
    _pool_mix(h, window_sum, pos, x, pw_ref, pscale_ref, store)


def _mixer0_prompt(x, nmix, pw, pscale):
    b, seq, d = x.shape
    tm = TOKEN_TILE
    nt = seq // tm
    per = tm // CTX_ROWS
    const2 = lambda bi, ti: (0, 0)
    return pl.pallas_call(
        _mixer0_prompt_kernel,
        grid=(b, nt),
        in_specs=[
            pl.BlockSpec((1, tm, d), lambda bi, ti: (bi, ti, 0)),
            pl.BlockSpec((1, CTX_ROWS, d), lambda bi, ti: (bi, jnp.maximum(ti * per - 1, 0), 0)),
            pl.BlockSpec((1, d), const2),
            pl.BlockSpec(pw.shape, lambda bi, ti: (0, 0, 0)),
            pl.BlockSpec((1, d), const2),
        ],
        out_specs=[
            pl.BlockSpec((1, tm, d), lambda bi, ti: (bi, ti, 0)),
            pl.BlockSpec((1, CTX_ROWS, d), lambda bi, ti: (bi, 0, 0)),
        ],
        out_shape=[
            jax.ShapeDtypeStruct((b, seq, d), F32),
            jax.ShapeDtypeStruct((b, CTX_ROWS, d), F32),
        ],
        scratch_shapes=[pltpu.VMEM((tm + 2 * CTX_ROWS, d), F32)] * 3,
        compiler_params=pltpu.CompilerParams(
            dimension_semantics=("arbitrary", "arbitrary"), vmem_limit_bytes=VMEM_LIMIT),
        name="mixer0_prompt",
    )(x, x, nmix, pw, pscale)


def _mixer0_sample(x, ctx, nmix, pw, pscale):
    b, steps, d = x.shape
    sb = SAMPLE_SEQ_BLOCK
    const2 = lambda i: (0, 0)
    return pl.pallas_call(
        _mixer0_sample_kernel,
        grid=(b // sb,),
        in_specs=[
            pl.BlockSpec((sb, steps, d), lambda i: (i, 0, 0)),
            pl.BlockSpec((sb, POOL_CTX, d), lambda i: (i, 0, 0)),
            pl.BlockSpec((1, d), const2),
            pl.BlockSpec(pw.shape, lambda i: (0, 0, 0)),
            pl.BlockSpec((1, d), const2),
        ],
        out_specs=[
            pl.BlockSpec((sb, steps, d), lambda i: (i, 0, 0)),
            pl.BlockSpec((sb, POOL_CTX, d), lambda i: (i, 0, 0)),
        ],
        out_shape=[
            jax.ShapeDtypeStruct((b, steps, d), F32),
            jax.ShapeDtypeStruct((b, POOL_CTX, d), F32),
        ],
        scratch_shapes=[pltpu.VMEM((sb, CTX_ROWS + steps, d), F32)],
        compiler_params=pltpu.CompilerParams(
            dimension_semantics=("arbitrary",), vmem_limit_bytes=VMEM_LIMIT),
        name="mixer0_sample",
    )(x, ctx, nmix, pw, pscale)


def _route_rows(glog, elog):
    sub8 = lax.broadcasted_iota(jnp.int32, glog.shape, 0).astype(F32)
    is_grp = sub8 < N_GROUPS
    gmax = jnp.max(jnp.where(is_grp, glog, NEG_BIG), axis=0, keepdims=True)
    gsum = jnp.sum(jnp.where(is_grp, jnp.exp(jnp.minimum(glog - gmax, 0.0)), 0.0),
                   axis=0, keepdims=True)
    g_w = 1.0 / gsum
    g_sel = jnp.min(jnp.where(is_grp & (glog == gmax), sub8, float(N_GROUPS)),
                    axis=0, keepdims=True)
    sub16 = lax.broadcasted_iota(jnp.int32, elog.shape, 0).astype(F32)
    lo = E_PER_GROUP * g_sel
    in_grp = (sub16 >= lo) & (sub16 < lo + E_PER_GROUP)
    none = float(N_EXPERTS)
    v1 = jnp.max(jnp.where(in_grp, elog, NEG_BIG), axis=0, keepdims=True)
    i1 = jnp.min(jnp.where(in_grp & (elog == v1), sub16, none), axis=0, keepdims=True)
    rest = in_grp & (sub16 != i1)
    v2 = jnp.max(jnp.where(rest, elog, NEG_BIG), axis=0, keepdims=True)
    i2 = jnp.min(jnp.where(rest & (elog == v2), sub16, none), axis=0, keepdims=True)
    e2 = jnp.exp(v2 - v1)
    den = 1.0 + e2
    return i1, i2, (1.0 / den) * g_w, (e2 / den) * g_w


def _dispatch(x1_ref, nffn_ref, rw_ref, rb_ref,
              xs_s, perm_s, gsort_s, meta_v, meta_sm, sem):
    tm = x1_ref.shape[0]
    sub = MOE_SUBTILE
    hi, lo = _split_bf16(_rms(x1_ref[...], nffn_ref[...]))
    w_hi, w_lo = _split_bf16(rw_ref[...])
    wide = _dot(hi, jnp.concatenate([w_hi, w_lo], axis=1))
    logits = wide[:, :LANES] + wide[:, LANES:] + _dot(lo, w_hi) + rb_ref[...]
    lt = logits.T
    i1, i2, w1, w2 = _route_rows(lt[0:SUBLANES], lt[EXPERT_LANE0:EXPERT_LANE0 + N_EXPERTS])

    sub16 = lax.broadcasted_iota(jnp.int32, (N_EXPERTS, 2 * sub), 0).astype(F32)
    jr = lax.broadcasted_iota(jnp.int32, (2 * sub, 2 * sub), 0)
    jc = lax.broadcasted_iota(jnp.int32, (2 * sub, 2 * sub), 1)
    earlier = jnp.where(jr < jc, 1.0, 0.0).astype(BF16)
    rsub = lax.broadcasted_iota(jnp.int32, (SORT_ROWS, sub), 0).astype(F32)
    esub = lax.broadcasted_iota(jnp.int32, (N_EXPERTS, LANES), 0)
    mlane = lax.broadcasted_iota(jnp.int32, (N_EXPERTS, LANES), 1)
    meta = jnp.zeros((N_EXPERTS, LANES), F32)
    for s in range(tm // sub):
        sl = slice(s * sub, (s + 1) * sub)
        sel = jnp.concatenate([i1[:, sl], i2[:, sl]], axis=1)
        at = jnp.where(sub16 == sel, 1.0, 0.0)
        cnt = jnp.sum(at, axis=1, keepdims=True)
        ca = jnp.floor((cnt + (SEG_ALIGN - 1)) * (1.0 / SEG_ALIGN)) * SEG_ALIGN
        cab = jnp.broadcast_to(ca, (N_EXPERTS, LANES))
        inc = cab
        for k in (1, 2, 4, 8):
            inc = inc + jnp.where(esub >= k, pltpu.roll(inc, k, 0), 0.0)
        seg0 = inc - cab
        rank = _dot(at.astype(BF16), earlier)
        dest = jnp.sum(at * (seg0[:, 0:1] + rank), axis=0, keepdims=True)
        p1 = rsub == dest[:, :sub]
        p2 = rsub == dest[:, sub:]
        perm = jnp.where(p1 | p2, 1.0, 0.0).astype(BF16)
        gsort_s[s] = jnp.sum(jnp.where(p1, w1[:, sl], 0.0) + jnp.where(p2, w2[:, sl], 0.0),
                             axis=1, keepdims=True)
        perm_s[s] = perm
        xs_s[s * SORT_ROWS:(s + 1) * SORT_ROWS, :] = _dot(perm, hi[sl, :]).astype(BF16)
        meta = jnp.where(mlane == 2 * s, seg0, meta)
        meta = jnp.where(mlane == 2 * s + 1, cab * (1.0 / SEG_ALIGN), meta)
    meta_v[...] = meta.astype(jnp.int32)
    cp = pltpu.make_async_copy(meta_v, meta_sm, sem)
    cp.start()
    cp.wait()


def _copy_blocks(src_ref, src0, dst_ref, dst0, nblocks):
    def body(b, carry):
        so = pl.multiple_of(src0 + b * SEG_ALIGN, SEG_ALIGN)
        do = pl.multiple_of(dst0 + b * SEG_ALIGN, SEG_ALIGN)
        dst_ref[pl.ds(do, SEG_ALIGN), :] = src_ref[pl.ds(so, SEG_ALIGN), :]
        return carry
    lax.fori_loop(0, nblocks, body, 0)


def _expert_step(e, nsub, xs_s, stage_s, meta_sm, wg, wu, wd):
    segs = []
    rows = 0
    for s in range(nsub):
        src = meta_sm[e, 2 * s] + s * SORT_ROWS
        nblocks = meta_sm[e, 2 * s + 1]
        _copy_blocks(xs_s, src, stage_s, rows, nblocks)
        segs.append((src, rows, nblocks))
        rows = rows + nblocks * SEG_ALIGN

    def mlp(r0, m):
        x = stage_s[pl.ds(r0, m), :]
        gt = _dot(x, wg[...])
        up = _dot(x, wu[...])
        act = (gt * _sigmoid(gt)) * up
        stage_s[pl.ds(r0, m), :] = _dot(act.astype(BF16), wd[...]).astype(BF16)

    def chunk(c, carry):
        mlp(pl.multiple_of(c * EXPERT_CHUNK, EXPERT_CHUNK), EXPERT_CHUNK)
        return carry

    full = rows // EXPERT_CHUNK
    lax.fori_loop(0, full, chunk, 0)
    tail0 = pl.multiple_of(full * EXPERT_CHUNK, EXPERT_CHUNK)
    tail = (rows - full * EXPERT_CHUNK + EXPERT_STEP - 1) // EXPERT_STEP
    for k in range(1, EXPERT_CHUNK // EXPERT_STEP + 1):
        @pl.when(tail == k)
        def _():
            mlp(tail0, k * EXPERT_STEP)

    for src, dst, nblocks in segs:
        _copy_blocks(stage_s, dst, xs_s, src, nblocks)


def _moe_kernel(with_kv, *refs):
    if with_kv:
        (x1_ref, p_ref, nffn_ref, rw_ref, rb_ref, wg_ref, wu_ref, wd_ref,
         nple_ref, pg_ref, pp_ref, kvn_ref, wkv_ref, kn_ref, rope_ref,
         out_ref, k_ref, v_ref, xs_s, stage_s, perm_s, gsort_s, meta_v, meta_sm, sem) = refs
    else:
        (x1_ref, p_ref, nffn_ref, rw_ref, rb_ref, wg_ref, wu_ref, wd_ref,
         nple_ref, pg_ref, pp_ref,
         out_ref, xs_s, stage_s, perm_s, gsort_s, meta_v, meta_sm, sem) = refs
    i = pl.program_id(0)
    step = pl.program_id(1)
    nsub = x1_ref.shape[0] // MOE_SUBTILE

    @pl.when((i == 0) & (step == 0))
    def _():
        stage_s[...] = jnp.zeros_like(stage_s)

    @pl.when(step == 0)
    def _():
        _dispatch(x1_ref, nffn_ref, rw_ref, rb_ref,
                  xs_s, perm_s, gsort_s, meta_v, meta_sm, sem)

    for j in range(EXPERTS_PER_STEP):
        _expert_step(step * EXPERTS_PER_STEP + j, nsub, xs_s, stage_s, meta_sm,
                     wg_ref.at[j], wu_ref.at[j], wd_ref.at[j])

    @pl.when(step == pl.num_programs(1) - 1)
    def _():
        for s in range(nsub):
            sl = slice(s * MOE_SUBTILE, (s + 1) * MOE_SUBTILE)
            ys = xs_s[s * SORT_ROWS:(s + 1) * SORT_ROWS, :].astype(F32) * gsort_s[s]
            out_ref[sl, :] = x1_ref[sl, :] + _dot_tn(perm_s[s], ys.astype(BF16))
        x2 = out_ref[...]
        hp = _rms(x2, nple_ref[...]).astype(BF16)
        gate = _sigmoid(_dot(hp, pg_ref[...]))
        proj = _dot(p_ref[...].astype(BF16), pp_ref[...])
        x3 = x2 + gate * proj
        out_ref[...] = x3
        if with_kv:
            hk = _rms(x3, kvn_ref[...]).astype(BF16)
            kv = _dot(hk, wkv_ref[...])
            k_ref[...] = _head_norm_rope(kv[:, :LANES], kn_ref[...], *_rope_parts(rope_ref))
            v_ref[...] = kv[:, LANES:]


def _moe_layer(x1, p_all, layer, w, kv=None):
    n, d = x1.shape
    tm = TOKEN_TILE
    nt = n // tm
    nsub = tm // MOE_SUBTILE
    tok = lambda i, e: (i, 0)
    const2 = lambda i, e: (0, 0)
    lay3 = lambda i, e: (layer, 0, 0)
    exp4 = lambda i, e: (layer, e, 0, 0)
    in_specs = [
        pl.BlockSpec((tm, d), tok),
        pl.BlockSpec((None, tm, D_PLE), lambda i, e: (layer, i, 0)),
        pl.BlockSpec((None, 1, d), lay3),
        pl.BlockSpec((None, d, LANES), lay3),
        pl.BlockSpec((None, 1, LANES), lay3),
        pl.BlockSpec((None, EXPERTS_PER_STEP, d, D_EXPERT), exp4),
        pl.BlockSpec((None, EXPERTS_PER_STEP, d, D_EXPERT), exp4),
        pl.BlockSpec((None, EXPERTS_PER_STEP, D_EXPERT, d), exp4),
        pl.BlockSpec((None, 1, d), lay3),
        pl.BlockSpec((None, d, d), lay3),
        pl.BlockSpec((None, D_PLE, d), lay3),
    ]
    args = [x1, p_all, w["norm_ffn"], w["router_w"], w["router_b"],
            w["exp_gate"], w["exp_up"], w["exp_down"], w["norm_ple"], w["ple_gate"], w["ple_proj"]]
    out_specs = [pl.BlockSpec((tm, d), tok)]
    out_shape = [jax.ShapeDtypeStruct((n, d), F32)]
    if kv is not None:
        tab_blocks = kv["rope"].shape[0] // tm
        in_specs += [
            pl.BlockSpec((1, d), const2),
            pl.BlockSpec((d, 2 * LANES), const2),
            pl.BlockSpec((1, LANES), const2),
            pl.BlockSpec((tm, 3 * LANES), lambda i, e: (i % tab_blocks, 0)),
        ]
        args += [kv["kv_norm"], kv["w_kv"], kv["k_norm"], kv["rope"]]
        out_specs += [pl.BlockSpec((tm, LANES), tok), pl.BlockSpec((tm, LANES), tok)]
        out_shape += [jax.ShapeDtypeStruct((n, LANES), F32), jax.ShapeDtypeStruct((n, LANES), F32)]
    return pl.pallas_call(
        functools.partial(_moe_kernel, kv is not None),
        grid=(nt, N_EXPERTS // EXPERTS_PER_STEP),
        in_specs=in_specs,
        out_specs=out_specs,
        out_shape=out_shape,
        scratch_shapes=[
            pltpu.VMEM((nsub * SORT_ROWS, d), BF16),
            pltpu.VMEM((STAGE_ROWS, d), BF16),
            pltpu.VMEM((nsub, SORT_ROWS, MOE_SUBTILE), BF16),
            pltpu.VMEM((nsub, SORT_ROWS, 1), F32),
            pltpu.VMEM((N_EXPERTS, LANES), jnp.int32),
            pltpu.SMEM((N_EXPERTS, LANES), jnp.int32),
            pltpu.SemaphoreType.DMA,
        ],
        compiler_params=pltpu.CompilerParams(
            dimension_semantics=("arbitrary", "arbitrary"), vmem_limit_bytes=VMEM_LIMIT),
        name="moe_kv" if kv is not None else "moe",
    )(*args)


def _project_q(x3, nmix_ref, wq_ref, qn_ref, cos, sin_next, sin_prev, q_s):
    h = _rms(x3, nmix_ref[...]).astype(BF16)
    wide = 2 * LANES
    for blk in range(D_MODEL // wide):
        q = _dot(h, wq_ref[:, blk * wide:(blk + 1) * wide])
        for c in range(wide // LANES):
            cols = slice(blk * wide + c * LANES, blk * wide + (c + 1) * LANES)
            qc = _head_norm_rope(q[:, c * LANES:(c + 1) * LANES], qn_ref[...],
                                 cos, sin_next, sin_prev)
            q_s[:, cols] = (qc * (HEAD_DIM ** -0.5)).astype(q_s.dtype)


def _both_orders(x):
    return x.astype(BF16), pltpu.roll(x, HEAD_DIM, 1).astype(BF16)


def _kv_order(par, g):
    return 0 if par == g else 1


def _softmax_sink(s, mask, sink):
    s = jnp.where(mask, s, NEG_BIG)
    m = jnp.maximum(jnp.max(s, axis=-1, keepdims=True), sink)
    p = jnp.exp(s - m)
    den = jnp.sum(p, axis=-1, keepdims=True) + jnp.exp(sink - m)
    return p.astype(BF16), 1.0 / den


def _attn_prompt_kernel(x3_ref, k_ref, v_ref, kprev_ref, vprev_ref, nmix_ref, wq_ref, qn_ref,
                        rope_ref, sinks_ref, wo_ref,
                        x4_ref, q_s, o_s, kd_s, vd_s):
    t = pl.program_id(1)
    tq = x3_ref.shape[1]
    x3 = x3_ref[0]
    _project_q(x3, nmix_ref, wq_ref, qn_ref, *_rope_parts(rope_ref), q_s)

    for src_prev, src, dst in ((kprev_ref, k_ref, kd_s), (vprev_ref, v_ref, vd_s)):
        dst[0, 0:WINDOW, :], dst[1, 0:WINDOW, :] = _both_orders(src_prev[0])
        dst[0, WINDOW:, :], dst[1, WINDOW:, :] = _both_orders(src[0])

    qi = lax.broadcasted_iota(jnp.int32, (WINDOW, 2 * WINDOW), 0)
    ki = lax.broadcasted_iota(jnp.int32, (WINDOW, 2 * WINDOW), 1)
    band = (ki > qi) & (ki <= qi + WINDOW)
    lane = lax.broadcasted_iota(jnp.int32, (WINDOW, LANES), 1)
    first = lane < HEAD_DIM

    def q_block(j, carry):
        r0 = pl.multiple_of(j * WINDOW, WINDOW)
        mask = band & (ki >= (1 - j) * WINDOW - t * tq)
        for c in range(D_MODEL // LANES):
            g = (2 * c) // Q_PER_KV
            cols = slice(c * LANES, (c + 1) * LANES)
            qc = q_s[pl.ds(r0, WINDOW), cols]
            outs = []
            for par in range(2):
                order = _kv_order(par, g)
                qh = jnp.where(first, qc, 0) if par == 0 else jnp.where(first, 0, qc)
                s = _dot_nt(qh, kd_s[order, pl.ds(r0, 2 * WINDOW), :])
                pr, inv = _softmax_sink(s, mask, sinks_ref[2 * c + par])
                outs.append(_dot(pr, vd_s[order, pl.ds(r0, 2 * WINDOW), :]) * inv)
            o_s[pl.ds(r0, WINDOW), cols] = jnp.where(first, outs[0], outs[1]).astype(BF16)
        return carry

    lax.fori_loop(0, tq // WINDOW, q_block, 0)
    x4_ref[0] = x3 + _dot(o_s[...], wo_ref[...])


def _attn_prompt(x3, k, v, w, rope):
    b, seq, d = x3.shape
    tq = ATTN_TILE
    nt = seq // tq
    per = tq // WINDOW
    tile = lambda bi, ti: (bi, ti, 0)
    prev = lambda bi, ti: (bi, jnp.maximum(ti * per - 1, 0), 0)
    const2 = lambda bi, ti: (0, 0)
    tab = lambda bi, ti: (ti, 0)
    return pl.pallas_call(
        _attn_prompt_kernel,
        grid=(b, nt),
        in_specs=[
            pl.BlockSpec((1, tq, d), tile),
            pl.BlockSpec((1, tq, LANES), tile),
            pl.BlockSpec((1, tq, LANES), tile),
            pl.BlockSpec((1, WINDOW, LANES), prev),
            pl.BlockSpec((1, WINDOW, LANES), prev),
            pl.BlockSpec((1, d), const2),
            pl.BlockSpec((d, d), const2),
            pl.BlockSpec((1, LANES), const2),
            pl.BlockSpec((tq, 3 * LANES), tab),
            pl.BlockSpec(memory_space=pltpu.SMEM),
            pl.BlockSpec((d, d), const2),
        ],
        out_specs=pl.BlockSpec((1, tq, d), tile),
        out_shape=jax.ShapeDtypeStruct((b, seq, d), F32),
        scratch_shapes=[
            pltpu.VMEM((tq, d), BF16),
            pltpu.VMEM((tq, d), BF16),
            pltpu.VMEM((N_KV_HEADS, tq + WINDOW, LANES), BF16),
            pltpu.VMEM((N_KV_HEADS, tq + WINDOW, LANES), BF16),
        ],
        compiler_params=pltpu.CompilerParams(
            dimension_semantics=("arbitrary", "arbitrary"), vmem_limit_bytes=VMEM_LIMIT),
        name="attn_prompt",
    )(x3, k, v, k, v, w["norm_mix1"], w["w_q"], w["q_norm"], rope, w["sinks"], w["w_o"])


def _attn_sample_kernel(x3_ref, knew_ref, vnew_ref, kc_ref, vc_ref, nmix_ref, wq_ref, qn_ref,
                        rope_ref, sinks_ref, wo_ref,
                        x4_ref, q_s, o_s):
    rows = x3_ref.shape[0]
    steps = knew_ref.shape[0] // kc_ref.shape[0]
    x3 = x3_ref[...]
    _project_q(x3, nmix_ref, wq_ref, qn_ref, *_rope_parts(rope_ref), q_s)

    n_chunks = D_MODEL // LANES
    stack = n_chunks * steps
    ri = lax.broadcasted_iota(jnp.int32, (stack, 2 * WINDOW), 0)
    ki = lax.broadcasted_iota(jnp.int32, (stack, 2 * WINDOW), 1)
    tq = ri % steps
    mask = ((ki < WINDOW) & (ki > tq)) | ((ki >= WINDOW) & ((ki - WINDOW) <= tq))
    def par_of(order, c):
        g = (2 * c) // Q_PER_KV
        return g if order == 0 else 1 - g

    chunk_of_row = lax.broadcasted_iota(jnp.int32, (stack, 1), 0) // steps
    sink_cols = []
    for order in range(2):
        col = jnp.zeros((stack, 1), F32)
        for c in range(n_chunks):
            col = jnp.where(chunk_of_row == c, sinks_ref[2 * c + par_of(order, c)], col)
        sink_cols.append(col)
    lane = lax.broadcasted_iota(jnp.int32, (steps, LANES), 1)
    first = lane < HEAD_DIM
    pad = jnp.zeros((WINDOW - steps, LANES), F32)

    def one_seq(b):
        r0 = pl.multiple_of(b * steps, steps)
        keys = _both_orders(jnp.concatenate([kc_ref[b], knew_ref[pl.ds(r0, steps), :], pad], axis=0))
        vals = _both_orders(jnp.concatenate([vc_ref[b], vnew_ref[pl.ds(r0, steps), :], pad], axis=0))
        qcs = [q_s[pl.ds(r0, steps), c * LANES:(c + 1) * LANES] for c in range(n_chunks)]
        outs = []
        for order in range(2):
            slabs = [jnp.where(first, qc, 0.0) if par_of(order, c) == 0 else jnp.where(first, 0.0, qc)
                     for c, qc in enumerate(qcs)]
            s = _dot_nt(jnp.concatenate(slabs, axis=0).astype(BF16), keys[order])
            pr, inv = _softmax_sink(s, mask, sink_cols[order])
            outs.append(_dot(pr, vals[order]) * inv)
        for c in range(n_chunks):
            lo_half = outs[0] if par_of(0, c) == 0 else outs[1]
            hi_half = outs[1] if par_of(0, c) == 0 else outs[0]
            o_s[pl.ds(r0, steps), c * LANES:(c + 1) * LANES] = jnp.where(
                first, lo_half[c * steps:(c + 1) * steps], hi_half[c * steps:(c + 1) * steps])

    def seq_pair(i, carry):
        for u in range(SAMPLE_SEQ_UNROLL):
            one_seq(i * SAMPLE_SEQ_UNROLL + u)
        return carry

    lax.fori_loop(0, rows // steps // SAMPLE_SEQ_UNROLL, seq_pair, 0)
    x4_ref[...] = x3 + _dot(o_s[...].astype(BF16), wo_ref[...])


def _attn_sample(x3, knew, vnew, kcache, vcache, w, rope, steps):
    n, d = x3.shape
    sb = SAMPLE_SEQ_BLOCK
    rows = sb * steps
    tok = lambda i: (i, 0)
    const2 = lambda i: (0, 0)
    return pl.pallas_call(
        _attn_sample_kernel,
        grid=(n // rows,),
        in_specs=[
            pl.BlockSpec((rows, d), tok),
            pl.BlockSpec((rows, LANES), tok),
            pl.BlockSpec((rows, LANES), tok),
            pl.BlockSpec((sb, WINDOW, LANES), lambda i: (i, 0, 0)),
            pl.BlockSpec((sb, WINDOW, LANES), lambda i: (i, 0, 0)),
            pl.BlockSpec((1, d), const2),
            pl.BlockSpec((d, d), const2),
            pl.BlockSpec((1, LANES), const2),
            pl.BlockSpec((rows, 3 * LANES), tok),
            pl.BlockSpec(memory_space=pltpu.SMEM),
            pl.BlockSpec((d, d), const2),
        ],
        out_specs=pl.BlockSpec((rows, d), tok),
        out_shape=jax.ShapeDtypeStruct((n, d), F32),
        scratch_shapes=[
            pltpu.VMEM((rows, d), F32),
            pltpu.VMEM((rows, d), F32),
        ],
        compiler_params=pltpu.CompilerParams(
            dimension_semantics=("arbitrary",), vmem_limit_bytes=VMEM_LIMIT),
        name="attn_sample",
    )(x3, knew, vnew, kcache, vcache, w["norm_mix1"], w["w_q"], w["q_norm"], rope,
      w["sinks"], w["w_o"])


def _rope_table(pos):
    half = ROPE_DIM // 2
    dim = jnp.arange(LANES, dtype=jnp.int32) % HEAD_DIM
    inv = jnp.float32(ROPE_THETA) ** (-((dim % half).astype(jnp.float32) * 2.0 / ROPE_DIM))
    inv = jnp.where(dim < ROPE_DIM, inv, 0.0)
    ang = pos.astype(jnp.float32)[:, None] * inv[None, :]
    cos, sin = jnp.cos(ang), jnp.sin(ang)
    s_next = jnp.where(dim < half, -sin, 0.0)
    s_prev = jnp.where((dim >= half) & (dim < ROPE_DIM), sin, 0.0)
    return jnp.concatenate([cos, s_next, s_prev], axis=1)


def kernel(x_prompt, x_sample, p_prompt, p_sample, state_pool, cache_k_win, cache_v_win, norm_mix, norm_ffn, norm_ple, pool_w, pool_scale, kv_norm, w_kv, k_norm, w_q, q_norm, sinks, w_o, router_g_w, router_g_b, router_e_w, router_e_b, exp_gate, exp_up, exp_down, ple_gate, ple_proj):
    b, seq, d = x_prompt.shape
    sb, steps, _ = x_sample.shape
    depth = norm_mix.shape[0]
    row = lambda v: v.reshape(1, -1)

    gap_w = jnp.zeros((depth, d, EXPERT_LANE0 - N_GROUPS), F32)
    tail_w = jnp.zeros((depth, d, LANES - EXPERT_LANE0 - N_EXPERTS), F32)
    router_w = jnp.concatenate(
        [router_g_w, gap_w, router_e_w.reshape(depth, d, N_EXPERTS), tail_w], axis=2)
    router_b = jnp.concatenate(
        [router_g_b, gap_w[:, 0], router_e_b.reshape(depth, N_EXPERTS), tail_w[:, 0]], axis=1)
    moe_w = {
        "norm_ffn": norm_ffn.reshape(depth, 1, d), "router_w": router_w,
        "router_b": router_b.reshape(depth, 1, LANES),
        "exp_gate": exp_gate.astype(BF16), "exp_up": exp_up.astype(BF16),
        "exp_down": exp_down.astype(BF16), "norm_ple": norm_ple.reshape(depth, 1, d),
        "ple_gate": ple_gate.astype(BF16), "ple_proj": ple_proj.astype(BF16),
    }
    two = lambda a: jnp.concatenate([a, a])
    attn_w = {
        "norm_mix1": row(norm_mix[1]), "w_q": w_q[0].astype(BF16), "q_norm": row(two(q_norm[0])),
        "sinks": sinks[0], "w_o": w_o[0].astype(BF16),
    }
    pw = pool_w[0]
    nmix0 = row(norm_mix[0])
    pscale = row(pool_scale[0])

    rope_p = _rope_table(jnp.arange(seq, dtype=jnp.int32))
    rope_s = jnp.tile(_rope_table(PAST_LEN + jnp.arange(steps, dtype=jnp.int32)), (sb, 1))
    kv_w = {"kv_norm": row(kv_norm), "w_kv": w_kv.astype(BF16), "k_norm": row(two(k_norm))}

    pp_all = p_prompt.reshape(depth, b * seq, D_PLE)
    x1p, pool16 = _mixer0_prompt(x_prompt, nmix0, pw, pscale)
    x3p, kp, vp = _moe_layer(x1p.reshape(b * seq, d), pp_all, 0, moe_w, dict(kv_w, rope=rope_p))
    kp = kp.reshape(b, seq, LANES)
    vp = vp.reshape(b, seq, LANES)
    x4p = _attn_prompt(x3p.reshape(b, seq, d), kp, vp, attn_w, rope_p)
    y_prompt = _moe_layer(x4p.reshape(b * seq, d), pp_all, 1, moe_w)[0].reshape(b, seq, d)
    pool_prompt = pool16[None, :, CTX_ROWS - POOL_CTX:, :]
    k_win_prompt = kp[:, seq - WINDOW:].reshape(b, WINDOW, N_KV_HEADS, HEAD_DIM)
    v_win_prompt = vp[:, seq - WINDOW:].reshape(b, WINDOW, N_KV_HEADS, HEAD_DIM)

    ps_all = p_sample.reshape(depth, sb * steps, D_PLE)
    x1s, pool_s = _mixer0_sample(x_sample, state_pool[0], nmix0, pw, pscale)
    x3s, ks, vs = _moe_layer(x1s.reshape(sb * steps, d), ps_all, 0, moe_w, dict(kv_w, rope=rope_s))
    kc = cache_k_win.reshape(sb, WINDOW, LANES)
    vc = cache_v_win.reshape(sb, WINDOW, LANES)
    x4s = _attn_sample(x3s, ks, vs, kc, vc, attn_w, rope_s, steps)
    y_sample = _moe_layer(x4s, ps_all, 1, moe_w)[0].reshape(sb, steps, d)
    pool_sample = pool_s[None]
    heads = (sb, steps, N_KV_HEADS, HEAD_DIM)
    k_win_sample = jnp.concatenate([cache_k_win[:, steps:], ks.reshape(heads)], axis=1)
    v_win_sample = jnp.concatenate([cache_v_win[:, steps:], vs.reshape(heads)], axis=1)

    return (y_prompt, y_sample, pool_prompt, pool_sample,
            k_win_prompt, v_win_prompt, k_win_sample, v_win_sample)
```

```python
import functools

import jax
import jax.numpy as jnp
from jax import lax
from jax.experimental import pallas as pl
from jax.experimental.pallas import tpu as pltpu

D_MODEL = 1024
PAST_LEN = 8192
POOL_WINDOWS = (2, 4, 8, 16)
POOL_GROUP_DIM = D_MODEL // len(POOL_WINDOWS)
POOL_CTX = max(POOL_WINDOWS) - 1
assert all(win == 2 ** (g + 1) for g, win in enumerate(POOL_WINDOWS))
HEAD_DIM = 64
N_HEADS = D_MODEL // HEAD_DIM
N_KV_HEADS = 2
Q_PER_KV = N_HEADS // N_KV_HEADS
WINDOW = 128
ROPE_DIM = HEAD_DIM // 4
ROPE_THETA = 500000.0
N_GROUPS = 4
E_PER_GROUP = 4
N_EXPERTS = N_GROUPS * E_PER_GROUP
D_EXPERT = 256
D_PLE = 256
EPS = 1e-6

LANES = 128
SUBLANES = 8
CTX_ROWS = 16
TOKEN_TILE = 1024
ATTN_TILE = 512
SAMPLE_SEQ_BLOCK = 16
SAMPLE_SEQ_UNROLL = 2
VMEM_LIMIT = 56 * 1024 * 1024
NEG_BIG = -1e30
LOG2_E = 1.4426950408889634
F32 = jnp.float32
BF16 = jnp.bfloat16

EXPERT_LANE0 = SUBLANES
MOE_SUBTILE = 256
SEG_ALIGN = 2 * SUBLANES
SORT_ROWS = 768
EXPERT_CHUNK = 256
EXPERT_STEP = 32
EXPERTS_PER_STEP = 4
SORT_BLOCKS = SORT_ROWS // SEG_ALIGN
MOE_SUBTILES = TOKEN_TILE // MOE_SUBTILE
XE_DUMP_ROW = (2 * TOKEN_TILE + MOE_SUBTILES * N_EXPERTS * (SEG_ALIGN - 1)
               + N_EXPERTS * (EXPERT_STEP - SEG_ALIGN))
META_BASE, META_ROWS = 64, 65
assert SORT_ROWS >= 2 * MOE_SUBTILE + N_EXPERTS * (SEG_ALIGN - 1) and SORT_ROWS % LANES == 0
assert TOKEN_TILE % MOE_SUBTILE == 0 and MOE_SUBTILES <= N_EXPERTS and SORT_BLOCKS <= META_BASE
assert EXPERT_CHUNK % EXPERT_STEP == 0 and EXPERT_STEP % SEG_ALIGN == 0
assert N_EXPERTS % EXPERTS_PER_STEP == 0 and XE_DUMP_ROW % SEG_ALIGN == 0


def _rms(x, g):
    return x * lax.rsqrt(jnp.mean(x * x, axis=-1, keepdims=True) + EPS) * g


def _dot(a, b):
    return jnp.dot(a, b, preferred_element_type=F32)


def _dot_nt(a, b):
    return lax.dot_general(a, b, (((1,), (1,)), ((), ())), preferred_element_type=F32)


def _dot_tn(a, b):
    return lax.dot_general(a, b, (((0,), (0,)), ((), ())), preferred_element_type=F32)


def _split_bf16(x):
    hi = x.astype(BF16)
    return hi, (x - hi.astype(F32)).astype(BF16)


def _dot3(a_hi, a_lo, b):
    b_hi, b_lo = _split_bf16(b)
    return _dot(a_hi, b_hi) + _dot(a_lo, b_hi) + _dot(a_hi, b_lo)


def _sigmoid(x):
    return 1.0 / (1.0 + jnp.exp(-x))


def _rope_parts(rope_ref):
    return rope_ref[:, 0:LANES], rope_ref[:, LANES:2 * LANES], rope_ref[:, 2 * LANES:3 * LANES]


def _head_norm_rope(x, gain, cos, sin_next, sin_prev):
    lane = lax.broadcasted_iota(jnp.int32, x.shape, 1)
    first = lane < HEAD_DIM
    sq = x * x
    m0 = jnp.sum(jnp.where(first, sq, 0.0), axis=-1, keepdims=True) * (1.0 / HEAD_DIM)
    m1 = jnp.sum(jnp.where(first, 0.0, sq), axis=-1, keepdims=True) * (1.0 / HEAD_DIM)
    y = x * lax.rsqrt(jnp.where(first, m0, m1) + EPS) * gain
    half = ROPE_DIM // 2
    return (y * cos + pltpu.roll(y, LANES - half, 1) * sin_next
            + pltpu.roll(y, half, 1) * sin_prev)


def _pool_mix(h, window_sum, pos, x, pw_ref, pscale_ref, store):
    for g, win in enumerate(POOL_WINDOWS):
        cols = slice(g * POOL_GROUP_DIM, (g + 1) * POOL_GROUP_DIM)
        hg = h[..., cols]
        cnt = jnp.minimum(win, pos + 1).astype(F32)
        pooled = (window_sum(g, cols) / cnt - hg).reshape(-1, POOL_GROUP_DIM)
        hi, lo = _split_bf16(pooled)
        mixed = _dot3(hi, lo, pw_ref[g])
        store(cols, x[..., cols] + mixed.reshape(hg.shape) * pscale_ref[:, cols])


def _mixer0_prompt_kernel(x_ref, xprev_ref, nmix_ref, pw_ref, pscale_ref,
                          x1_ref, pool_ref, hs_ref, ua_ref, ub_ref):
    t = pl.program_id(1)
    tm = x_ref.shape[1]
    x = x_ref[0]
    h = _rms(x, nmix_ref[...])
    hp = _rms(xprev_ref[0], nmix_ref[...])
    top = 2 * CTX_ROWS
    hs_ref[0:CTX_ROWS, :] = jnp.zeros((CTX_ROWS, x.shape[1]), F32)
    hs_ref[CTX_ROWS:top, :] = jnp.where(t > 0, hp, 0.0)
    hs_ref[top:, :] = h
    ua_ref[0:SUBLANES, :] = jnp.zeros((SUBLANES, x.shape[1]), F32)
    ub_ref[0:SUBLANES, :] = jnp.zeros((SUBLANES, x.shape[1]), F32)

    n = tm + top - SUBLANES
    src, level_of = hs_ref, []
    for g, win in enumerate(POOL_WINDOWS):
        dst = ua_ref if g % 2 == 0 else ub_ref
        c0 = g * POOL_GROUP_DIM
        back = win // 2
        dst[SUBLANES:, c0:] = src[SUBLANES:, c0:] + src[SUBLANES - back:SUBLANES - back + n, c0:]
        level_of.append(dst)
        src = dst

    def window_sum(g, cols):
        return level_of[g][top:, cols]

    def store(cols, val):
        x1_ref[0, :, cols] = val

    pos = t * tm + lax.broadcasted_iota(jnp.int32, (tm, 1), 0)
    _pool_mix(h, window_sum, pos, x, pw_ref, pscale_ref, store)

    @pl.when(t == pl.num_programs(1) - 1)
    def _():
        pool_ref[0] = hs_ref[tm + top - CTX_ROWS:tm + top, :]


def _mixer0_sample_kernel(x_ref, ctx_ref, nmix_ref, pw_ref, pscale_ref,
                          x1_ref, pool_ref, hs_ref):
    x = x_ref[...]
    steps = x.shape[1]
    h = _rms(x, nmix_ref[...])
    hs_ref[:, CTX_ROWS - POOL_CTX:CTX_ROWS, :] = ctx_ref[...]
    hs_ref[:, CTX_ROWS:, :] = h
    pool_ref[:, 0:POOL_CTX - steps, :] = ctx_ref[:, steps:, :]
    pool_ref[:, POOL_CTX - steps:, :] = h

    def window_sum(g, cols):
        s = h[..., cols]
        for j in range(1, POOL_WINDOWS[g]):
            s = s + hs_ref[:, CTX_ROWS - j:CTX_ROWS - j + steps, cols]
        return s

    def store(cols, val):
        x1_ref[:, :, cols] = val

    pos = PAST_LEN + lax.broadcasted_iota(jnp.int32, (1, steps, 1), 1)
    _pool_mix(h, window_sum, pos, x, pw_ref, pscale_ref, store)


def _mixer0_prompt(x, nmix, pw, pscale):
    b, seq, d = x.shape
    tm = TOKEN_TILE
    nt = seq // tm
    per = tm // CTX_ROWS
    const2 = lambda bi, ti: (0, 0)
    return pl.pallas_call(
        _mixer0_prompt_kernel,
        grid=(b, nt),
        in_specs=[
            pl.BlockSpec((1, tm, d), lambda bi, ti: (bi, ti, 0)),
            pl.BlockSpec((1, CTX_ROWS, d), lambda bi, ti: (bi, jnp.maximum(ti * per - 1, 0), 0)),
            pl.BlockSpec((1, d), const2),
            pl.BlockSpec(pw.shape, lambda bi, ti: (0, 0, 0)),
            pl.BlockSpec((1, d), const2),
        ],
        out_specs=[
            pl.BlockSpec((1, tm, d), lambda bi, ti: (bi, ti, 0)),
            pl.BlockSpec((1, CTX_ROWS, d), lambda bi, ti: (bi, 0, 0)),
        ],
        out_shape=[
            jax.ShapeDtypeStruct((b, seq, d), F32),
            jax.ShapeDtypeStruct((b, CTX_ROWS, d), F32),
        ],
        scratch_shapes=[pltpu.VMEM((tm + 2 * CTX_ROWS, d), F32)] * 3,
        compiler_params=pltpu.CompilerParams(
            dimension_semantics=("arbitrary", "arbitrary"), vmem_limit_bytes=VMEM_LIMIT),
        name="mixer0_prompt",
    )(x, x, nmix, pw, pscale)


def _mixer0_sample(x, ctx, nmix, pw, pscale):
    b, steps, d = x.shape
    sb = SAMPLE_SEQ_BLOCK
    const2 = lambda i: (0, 0)
    return pl.pallas_call(
        _mixer0_sample_kernel,
        grid=(b // sb,),
        in_specs=[
            pl.BlockSpec((sb, steps, d), lambda i: (i, 0, 0)),
            pl.BlockSpec((sb, POOL_CTX, d), lambda i: (i, 0, 0)),
            pl.BlockSpec((1, d), const2),
            pl.BlockSpec(pw.shape, lambda i: (0, 0, 0)),
            pl.BlockSpec((1, d), const2),
        ],
        out_specs=[
            pl.BlockSpec((sb, steps, d), lambda i: (i, 0, 0)),
            pl.BlockSpec((sb, POOL_CTX, d), lambda i: (i, 0, 0)),
        ],
        out_shape=[
            jax.ShapeDtypeStruct((b, steps, d), F32),
            jax.ShapeDtypeStruct((b, POOL_CTX, d), F32),
        ],
        scratch_shapes=[pltpu.VMEM((sb, CTX_ROWS + steps, d), F32)],
        compiler_params=pltpu.CompilerParams(
            dimension_semantics=("arbitrary",), vmem_limit_bytes=VMEM_LIMIT),
        name="mixer0_sample",
    )(x, ctx, nmix, pw, pscale)


def _route_rows(glog, elog):
    sub8 = lax.broadcasted_iota(jnp.int32, glog.shape, 0).astype(F32)
    is_grp = sub8 < N_GROUPS
    gmax = jnp.max(jnp.where(is_grp, glog, NEG_BIG), axis=0, keepdims=True)
    gsum = jnp.sum(jnp.where(is_grp, jnp.exp(jnp.minimum(glog - gmax, 0.0)), 0.0),
                   axis=0, keepdims=True)
    g_w = 1.0 / gsum
    g_sel = jnp.min(jnp.where(is_grp & (glog == gmax), sub8, float(N_GROUPS)),
                    axis=0, keepdims=True)
    sub16 = lax.broadcasted_iota(jnp.int32, elog.shape, 0).astype(F32)
    lo = E_PER_GROUP * g_sel
    in_grp = (sub16 >= lo) & (sub16 < lo + E_PER_GROUP)
    none = float(N_EXPERTS)
    v1 = jnp.max(jnp.where(in_grp, elog, NEG_BIG), axis=0, keepdims=True)
    i1 = jnp.min(jnp.where(in_grp & (elog == v1), sub16, none), axis=0, keepdims=True)
    rest = in_grp & (sub16 != i1)
    v2 = jnp.max(jnp.where(rest, elog, NEG_BIG), axis=0, keepdims=True)
    i2 = jnp.min(jnp.where(rest & (elog == v2), sub16, none), axis=0, keepdims=True)
    e2 = jnp.exp(v2 - v1)
    den = 1.0 + e2
    return i1, i2, (1.0 / den) * g_w, (e2 / den) * g_w


def _dispatch(x1_ref, nffn_ref, rw_ref, rb_ref,
              xe_s, perm_s, gsort_s, meta_v, meta_sm, sem):
    tm = x1_ref.shape[0]
    sub = MOE_SUBTILE
    nsub = tm // sub
    hi, lo = _split_bf16(_rms(x1_ref[...], nffn_ref[...]))
    w_hi, w_lo = _split_bf16(rw_ref[...])
    wide = _dot(hi, jnp.concatenate([w_hi, w_lo], axis=1))
    logits = wide[:, :LANES] + wide[:, LANES:] + _dot(lo, w_hi) + rb_ref[...]
    lt = logits.T
    i1, i2, w1, w2 = _route_rows(lt[0:SUBLANES], lt[EXPERT_LANE0:EXPERT_LANE0 + N_EXPERTS])

    sub16 = lax.broadcasted_iota(jnp.int32, (N_EXPERTS, 2 * sub), 0).astype(F32)
    esub = lax.broadcasted_iota(jnp.int32, (N_EXPERTS, LANES), 0)
    mlane = lax.broadcasted_iota(jnp.int32, (N_EXPERTS, LANES), 1)

    def round_up(x, step):
        return jnp.floor((x + (step - 1)) * (1.0 / step)) * step

    def starts(sizes):
        inc = sizes
        for k in (1, 2, 4, 8):
            inc = inc + jnp.where(esub >= k, pltpu.roll(inc, k, 0), 0.0)
        return inc - sizes

    onehots, sizes = [], []
    for s in range(nsub):
        sl = slice(s * sub, (s + 1) * sub)
        sel = jnp.concatenate([i1[:, sl], i2[:, sl]], axis=1)
        at = jnp.where(sub16 == sel, 1.0, 0.0)
        cnt = jnp.sum(at, axis=1, keepdims=True)
        onehots.append(at)
        sizes.append(jnp.broadcast_to(round_up(cnt, SEG_ALIGN), (N_EXPERTS, LANES)))

    region = round_up(sum(sizes), EXPERT_STEP)
    base = starts(region)
    meta = jnp.where(mlane == META_BASE, base, jnp.where(mlane == META_ROWS, region, 0.0))
    blk0 = mlane.astype(F32) * SEG_ALIGN
    seg_starts, at_row = [], base
    for s in range(nsub):
        seg0 = starts(sizes[s])
        seg_starts.append(seg0)
        inside = (blk0 >= seg0) & (blk0 < seg0 + sizes[s])
        dst = jnp.sum(jnp.where(inside, at_row - seg0 + blk0, 0.0), axis=0, keepdims=True)
        used = jnp.sum(jnp.where(inside, 1.0, 0.0), axis=0, keepdims=True)
        dst = jnp.where(used > 0.0, dst, float(XE_DUMP_ROW))
        meta = jnp.where((esub == s) & (mlane < SORT_BLOCKS), dst, meta)
        at_row = at_row + sizes[s]
    meta_v[...] = meta.astype(jnp.int32)
    cp = pltpu.make_async_copy(meta_v, meta_sm, sem)
    cp.start()

    jr = lax.broadcasted_iota(jnp.int32, (2 * sub, 2 * sub), 0)
    jc = lax.broadcasted_iota(jnp.int32, (2 * sub, 2 * sub), 1)
    earlier = jnp.where(jr < jc, 1.0, 0.0).astype(BF16)
    rsub = lax.broadcasted_iota(jnp.int32, (SORT_ROWS, sub), 0).astype(F32)
    for s in range(nsub):
        sl = slice(s * sub, (s + 1) * sub)
        at = onehots[s]
        rank = _dot(at.astype(BF16), earlier)
        dest = jnp.sum(at * (seg_starts[s][:, 0:1] + rank), axis=0, keepdims=True)
        p1 = rsub == dest[:, :sub]
        p2 = rsub == dest[:, sub:]
        perm = jnp.where(p1 | p2, 1.0, 0.0).astype(BF16)
        gsort_s[s] = jnp.sum(jnp.where(p1, w1[:, sl], 0.0) + jnp.where(p2, w2[:, sl], 0.0),
                             axis=1, keepdims=True)
        perm_s[s] = perm
        xs = _dot(perm, hi[sl, :]).astype(BF16)
        if s == 0:
            cp.wait()
        for b in range(SORT_BLOCKS):
            row = pl.multiple_of(meta_sm[s, b], SEG_ALIGN)
            xe_s[pl.ds(row, SEG_ALIGN), :] = xs[b * SEG_ALIGN:(b + 1) * SEG_ALIGN, :]


def _expert_step(e, xe_s, meta_sm, wg, wu, wd):
    base = meta_sm[e, META_BASE]
    rows = meta_sm[e, META_ROWS]

    def mlp(r0, m):
        x = xe_s[pl.ds(r0, m), :]
        gt = _dot(x, wg[...])
        up = _dot(x, wu[...])
        act = (gt * _sigmoid(gt)) * up
        xe_s[pl.ds(r0, m), :] = _dot(act.astype(BF16), wd[...]).astype(BF16)

    def chunk(c, carry):
        mlp(pl.multiple_of(base + c * EXPERT_CHUNK, EXPERT_STEP), EXPERT_CHUNK)
        return carry

    full = rows // EXPERT_CHUNK
    lax.fori_loop(0, full, chunk, 0)
    tail0 = pl.multiple_of(base + full * EXPERT_CHUNK, EXPERT_STEP)
    tail = (rows - full * EXPERT_CHUNK) // EXPERT_STEP
    for k in range(1, EXPERT_CHUNK // EXPERT_STEP):
        @pl.when(tail == k)
        def _():
            mlp(tail0, k * EXPERT_STEP)


def _moe_kernel(with_kv, *refs):
    if with_kv:
        (x1_ref, p_ref, nffn_ref, rw_ref, rb_ref, wg_ref, wu_ref, wd_ref,
         nple_ref, pg_ref, pp_ref, kvn_ref, wkv_ref, kn_ref, rope_ref,
         out_ref, k_ref, v_ref, xe_s, ys_s, perm_s, gsort_s, meta_v, meta_sm, sem) = refs
    else:
        (x1_ref, p_ref, nffn_ref, rw_ref, rb_ref, wg_ref, wu_ref, wd_ref,
         nple_ref, pg_ref, pp_ref,
         out_ref, xe_s, ys_s, perm_s, gsort_s, meta_v, meta_sm, sem) = refs
    i = pl.program_id(0)
    step = pl.program_id(1)
    nsub = x1_ref.shape[0] // MOE_SUBTILE

    @pl.when((i == 0) & (step == 0))
    def _():
        xe_s[...] = jnp.zeros_like(xe_s)

    @pl.when(step == 0)
    def _():
        _dispatch(x1_ref, nffn_ref, rw_ref, rb_ref,
                  xe_s, perm_s, gsort_s, meta_v, meta_sm, sem)

    for j in range(EXPERTS_PER_STEP):
        _expert_step(step * EXPERTS_PER_STEP + j, xe_s, meta_sm,
                     wg_ref.at[j], wu_ref.at[j], wd_ref.at[j])

    @pl.when(step == pl.num_programs(1) - 1)
    def _():
        for s in range(nsub):
            sl = slice(s * MOE_SUBTILE, (s + 1) * MOE_SUBTILE)
            for b in range(SORT_BLOCKS):
                rows = slice(b * SEG_ALIGN, (b + 1) * SEG_ALIGN)
                src = pl.multiple_of(meta_sm[s, b], SEG_ALIGN)
                blk = xe_s[pl.ds(src, SEG_ALIGN), :].astype(F32) * gsort_s[s, rows, :]
                ys_s[rows, :] = blk.astype(BF16)
            out_ref[sl, :] = x1_ref[sl, :] + _dot_tn(perm_s[s], ys_s[...])
        x2 = out_ref[...]
        hp = _rms(x2, nple_ref[...]).astype(BF16)
        gate = _sigmoid(_dot(hp, pg_ref[...]))
        proj = _dot(p_ref[...].astype(BF16), pp_ref[...])
        x3 = x2 + gate * proj
        out_ref[...] = x3
        if with_kv:
            hk = _rms(x3, kvn_ref[...]).astype(BF16)
            kv = _dot(hk, wkv_ref[...])
            k_ref[...] = _head_norm_rope(kv[:, :LANES], kn_ref[...], *_rope_parts(rope_ref))
            v_ref[...] = kv[:, LANES:]


def _moe_layer(x1, p_all, layer, w, kv=None):
    n, d = x1.shape
    tm = TOKEN_TILE
    nt = n // tm
    nsub = tm // MOE_SUBTILE
    tok = lambda i, e: (i, 0)
    const2 = lambda i, e: (0, 0)
    lay3 = lambda i, e: (layer, 0, 0)
    exp4 = lambda i, e: (layer, e, 0, 0)
    in_specs = [
        pl.BlockSpec((tm, d), tok),
        pl.BlockSpec((None, tm, D_PLE), lambda i, e: (layer, i, 0)),
        pl.BlockSpec((None, 1, d), lay3),
        pl.BlockSpec((None, d, LANES), lay3),
        pl.BlockSpec((None, 1, LANES), lay3),
        pl.BlockSpec((None, EXPERTS_PER_STEP, d, D_EXPERT), exp4),
        pl.BlockSpec((None, EXPERTS_PER_STEP, d, D_EXPERT), exp4),
        pl.BlockSpec((None, EXPERTS_PER_STEP, D_EXPERT, d), exp4),
        pl.BlockSpec((None, 1, d), lay3),
        pl.BlockSpec((None, d, d), lay3),
        pl.BlockSpec((None, D_PLE, d), lay3),
    ]
    args = [x1, p_all, w["norm_ffn"], w["router_w"], w["router_b"],
            w["exp_gate"], w["exp_up"], w["exp_down"], w["norm_ple"], w["ple_gate"], w["ple_proj"]]
    out_specs = [pl.BlockSpec((tm, d), tok)]
    out_shape = [jax.ShapeDtypeStruct((n, d), F32)]
    if kv is not None:
        tab_blocks = kv["rope"].shape[0] // tm
        in_specs += [
            pl.BlockSpec((1, d), const2),
            pl.BlockSpec((d, 2 * LANES), const2),
            pl.BlockSpec((1, LANES), const2),
            pl.BlockSpec((tm, 3 * LANES), lambda i, e: (i % tab_blocks, 0)),
        ]
        args += [kv["kv_norm"], kv["w_kv"], kv["k_norm"], kv["rope"]]
        out_specs += [pl.BlockSpec((tm, LANES), tok), pl.BlockSpec((tm, LANES), tok)]
        out_shape += [jax.ShapeDtypeStruct((n, LANES), F32), jax.ShapeDtypeStruct((n, LANES), F32)]
    return pl.pallas_call(
        functools.partial(_moe_kernel, kv is not None),
        grid=(nt, N_EXPERTS // EXPERTS_PER_STEP),
        in_specs=in_specs,
        out_specs=out_specs,
        out_shape=out_shape,
        scratch_shapes=[
            pltpu.VMEM((XE_DUMP_ROW + SEG_ALIGN, d), BF16),
            pltpu.VMEM((SORT_ROWS, d), BF16),
            pltpu.VMEM((nsub, SORT_ROWS, MOE_SUBTILE), BF16),
            pltpu.VMEM((nsub, SORT_ROWS, 1), F32),
            pltpu.VMEM((N_EXPERTS, LANES), jnp.int32),
            pltpu.SMEM((N_EXPERTS, LANES), jnp.int32),
            pltpu.SemaphoreType.DMA,
        ],
        compiler_params=pltpu.CompilerParams(
            dimension_semantics=("arbitrary", "arbitrary"), vmem_limit_bytes=VMEM_LIMIT),
        name="moe_kv" if kv is not None else "moe",
    )(*args)


def _project_q(x3, nmix_ref, wq_ref, qn_ref, cos, sin_next, sin_prev, q_s):
    h = _rms(x3, nmix_ref[...]).astype(BF16)
    wide = 2 * LANES
    for blk in range(D_MODEL // wide):
        q = _dot(h, wq_ref[:, blk * wide:(blk + 1) * wide])
        for c in range(wide // LANES):
            cols = slice(blk * wide + c * LANES, blk * wide + (c + 1) * LANES)
            qc = _head_norm_rope(q[:, c * LANES:(c + 1) * LANES], qn_ref[...],
                                 cos, sin_next, sin_prev)
            q_s[:, cols] = (qc * (HEAD_DIM ** -0.5 * LOG2_E)).astype(q_s.dtype)


def _both_orders(x):
    return x.astype(BF16), pltpu.roll(x, HEAD_DIM, 1).astype(BF16)


def _kv_order(par, g):
    return 0 if par == g else 1


def _softmax_sink(s, mask, sink):
    sink = sink * LOG2_E
    s = jnp.where(mask, s, NEG_BIG)
    m = jnp.maximum(jnp.max(s, axis=-1, keepdims=True), sink)
    p = jnp.exp2(s - m)
    den = jnp.sum(p, axis=-1, keepdims=True) + jnp.exp2(sink - m)
    return p.astype(BF16), 1.0 / den


def _attn_prompt_kernel(x3_ref, k_ref, v_ref, kprev_ref, vprev_ref, nmix_ref, wq_ref, qn_ref,
                        rope_ref, sinks_ref, wo_ref,
                        x4_ref, q_s, o_s, kd_s, vd_s):
    t = pl.program_id(1)
    tq = x3_ref.shape[1]
    x3 = x3_ref[0]
    _project_q(x3, nmix_ref, wq_ref, qn_ref, *_rope_parts(rope_ref), q_s)

    for src_prev, src, dst in ((kprev_ref, k_ref, kd_s), (vprev_ref, v_ref, vd_s)):
        dst[0, 0:WINDOW, :], dst[1, 0:WINDOW, :] = _both_orders(src_prev[0])
        dst[0, WINDOW:, :], dst[1, WINDOW:, :] = _both_orders(src[0])

    qi = lax.broadcasted_iota(jnp.int32, (WINDOW, 2 * WINDOW), 0)
    ki = lax.broadcasted_iota(jnp.int32, (WINDOW, 2 * WINDOW), 1)
    band = (ki > qi) & (ki <= qi + WINDOW)
    lane = lax.broadcasted_iota(jnp.int32, (WINDOW, LANES), 1)
    first = lane < HEAD_DIM

    def q_block(j, carry):
        r0 = pl.multiple_of(j * WINDOW, WINDOW)
        mask = band & (ki >= (1 - j) * WINDOW - t * tq)
        for c in range(D_MODEL // LANES):
            g = (2 * c) // Q_PER_KV
            cols = slice(c * LANES, (c + 1) * LANES)
            qc = q_s[pl.ds(r0, WINDOW), cols]
            outs = []
            for par in range(2):
                order = _kv_order(par, g)
                qh = jnp.where(first, qc, 0) if par == 0 else jnp.where(first, 0, qc)
                s = _dot_nt(qh, kd_s[order, pl.ds(r0, 2 * WINDOW), :])
                pr, inv = _softmax_sink(s, mask, sinks_ref[2 * c + par])
                outs.append(_dot(pr, vd_s[order, pl.ds(r0, 2 * WINDOW), :]) * inv)
            o_s[pl.ds(r0, WINDOW), cols] = jnp.where(first, outs[0], outs[1]).astype(BF16)
        return carry

    lax.fori_loop(0, tq // WINDOW, q_block, 0)
    x4_ref[0] = x3 + _dot(o_s[...], wo_ref[...])


def _attn_prompt(x3, k, v, w, rope):
    b, seq, d = x3.shape
    tq = ATTN_TILE
    nt = seq // tq
    per = tq // WINDOW
    tile = lambda bi, ti: (bi, ti, 0)
    prev = lambda bi, ti: (bi, jnp.maximum(ti * per - 1, 0), 0)
    const2 = lambda bi, ti: (0, 0)
    tab = lambda bi, ti: (ti, 0)
    return pl.pallas_call(
        _attn_prompt_kernel,
        grid=(b, nt),
        in_specs=[
            pl.BlockSpec((1, tq, d), tile),
            pl.BlockSpec((1, tq, LANES), tile),
            pl.BlockSpec((1, tq, LANES), tile),
            pl.BlockSpec((1, WINDOW, LANES), prev),
            pl.BlockSpec((1, WINDOW, LANES), prev),
            pl.BlockSpec((1, d), const2),
            pl.BlockSpec((d, d), const2),
            pl.BlockSpec((1, LANES), const2),
            pl.BlockSpec((tq, 3 * LANES), tab),
            pl.BlockSpec(memory_space=pltpu.SMEM),
            pl.BlockSpec((d, d), const2),
        ],
        out_specs=pl.BlockSpec((1, tq, d), tile),
        out_shape=jax.ShapeDtypeStruct((b, seq, d), F32),
        scratch_shapes=[
            pltpu.VMEM((tq, d), BF16),
            pltpu.VMEM((tq, d), BF16),
            pltpu.VMEM((N_KV_HEADS, tq + WINDOW, LANES), BF16),
            pltpu.VMEM((N_KV_HEADS, tq + WINDOW, LANES), BF16),
        ],
        compiler_params=pltpu.CompilerParams(
            dimension_semantics=("arbitrary", "arbitrary"), vmem_limit_bytes=VMEM_LIMIT),
        name="attn_prompt",
    )(x3, k, v, k, v, w["norm_mix1"], w["w_q"], w["q_norm"], rope, w["sinks"], w["w_o"])


def _attn_sample_kernel(x3_ref, knew_ref, vnew_ref, kc_ref, vc_ref, nmix_ref, wq_ref, qn_ref,
                        rope_ref, sinks_ref, wo_ref,
                        x4_ref, q_s, o_s):
    rows = x3_ref.shape[0]
    steps = knew_ref.shape[0] // kc_ref.shape[0]
    x3 = x3_ref[...]
    _project_q(x3, nmix_ref, wq_ref, qn_ref, *_rope_parts(rope_ref), q_s)

    n_chunks = D_MODEL // LANES
    stack = n_chunks * steps
    ri = lax.broadcasted_iota(jnp.int32, (stack, 2 * WINDOW), 0)
    ki = lax.broadcasted_iota(jnp.int32, (stack, 2 * WINDOW), 1)
    tq = ri % steps
    mask = ((ki < WINDOW) & (ki > tq)) | ((ki >= WINDOW) & ((ki - WINDOW) <= tq))
    def par_of(order, c):
        g = (2 * c) // Q_PER_KV
        return g if order == 0 else 1 - g

    chunk_of_row = lax.broadcasted_iota(jnp.int32, (stack, 1), 0) // steps
    sink_cols = []
    for order in range(2):
        col = jnp.zeros((stack, 1), F32)
        for c in range(n_chunks):
            col = jnp.where(chunk_of_row == c, sinks_ref[2 * c + par_of(order, c)], col)
        sink_cols.append(col)
    lane = lax.broadcasted_iota(jnp.int32, (steps, LANES), 1)
    first = lane < HEAD_DIM
    pad = jnp.zeros((WINDOW - steps, LANES), F32)

    def one_seq(b):
        r0 = pl.multiple_of(b * steps, steps)
        keys = _both_orders(jnp.concatenate([kc_ref[b], knew_ref[pl.ds(r0, steps), :], pad], axis=0))
        vals = _both_orders(jnp.concatenate([vc_ref[b], vnew_ref[pl.ds(r0, steps), :], pad], axis=0))
        qcs = [q_s[pl.ds(r0, steps), c * LANES:(c + 1) * LANES] for c in range(n_chunks)]
        outs = []
        for order in range(2):
            slabs = [jnp.where(first, qc, 0.0) if par_of(order, c) == 0 else jnp.where(first, 0.0, qc)
                     for c, qc in enumerate(qcs)]
            s = _dot_nt(jnp.concatenate(slabs, axis=0).astype(BF16), keys[order])
            pr, inv = _softmax_sink(s, mask, sink_cols[order])
            outs.append(_dot(pr, vals[order]) * inv)
        for c in range(n_chunks):
            lo_half = outs[0] if par_of(0, c) == 0 else outs[1]
            hi_half = outs[1] if par_of(0, c) == 0 else outs[0]
            o_s[pl.ds(r0, steps), c * LANES:(c + 1) * LANES] = jnp.where(
                first, lo_half[c * steps:(c + 1) * steps], hi_half[c * steps:(c + 1) * steps])

    def seq_pair(i, carry):
        for u in range(SAMPLE_SEQ_UNROLL):
            one_seq(i * SAMPLE_SEQ_UNROLL + u)
        return carry

    lax.fori_loop(0, rows // steps // SAMPLE_SEQ_UNROLL, seq_pair, 0)
    x4_ref[...] = x3 + _dot(o_s[...].astype(BF16), wo_ref[...])


def _attn_sample(x3, knew, vnew, kcache, vcache, w, rope, steps):
    n, d = x3.shape
    sb = SAMPLE_SEQ_BLOCK
    rows = sb * steps
    tok = lambda i: (i, 0)
    const2 = lambda i: (0, 0)
    return pl.pallas_call(
        _attn_sample_kernel,
        grid=(n // rows,),
        in_specs=[
            pl.BlockSpec((rows, d), tok),
            pl.BlockSpec((rows, LANES), tok),
            pl.BlockSpec((rows, LANES), tok),
            pl.BlockSpec((sb, WINDOW, LANES), lambda i: (i, 0, 0)),
            pl.BlockSpec((sb, WINDOW, LANES), lambda i: (i, 0, 0)),
            pl.BlockSpec((1, d), const2),
            pl.BlockSpec((d, d), const2),
            pl.BlockSpec((1, LANES), const2),
            pl.BlockSpec((rows, 3 * LANES), tok),
            pl.BlockSpec(memory_space=pltpu.SMEM),
            pl.BlockSpec((d, d), const2),
        ],
        out_specs=pl.BlockSpec((rows, d), tok),
        out_shape=jax.ShapeDtypeStruct((n, d), F32),
        scratch_shapes=[
            pltpu.VMEM((rows, d), F32),
            pltpu.VMEM((rows, d), F32),
        ],
        compiler_params=pltpu.CompilerParams(
            dimension_semantics=("arbitrary",), vmem_limit_bytes=VMEM_LIMIT),
        name="attn_sample",
    )(x3, knew, vnew, kcache, vcache, w["norm_mix1"], w["w_q"], w["q_norm"], rope,
      w["sinks"], w["w_o"])


def _rope_table(pos):
    half = ROPE_DIM // 2
    dim = jnp.arange(LANES, dtype=jnp.int32) % HEAD_DIM
    inv = jnp.float32(ROPE_THETA) ** (-((dim % half).astype(jnp.float32) * 2.0 / ROPE_DIM))
    inv = jnp.where(dim < ROPE_DIM, inv, 0.0)
    ang = pos.astype(jnp.float32)[:, None] * inv[None, :]
    cos, sin = jnp.cos(ang), jnp.sin(ang)
    s_next = jnp.where(dim < half, -sin, 0.0)
    s_prev = jnp.where((dim >= half) & (dim < ROPE_DIM), sin, 0.0)
    return jnp.concatenate([cos, s_next, s_prev], axis=1)


def kernel(x_prompt, x_sample, p_prompt, p_sample, state_pool, cache_k_win, cache_v_win, norm_mix, norm_ffn, norm_ple, pool_w, pool_scale, kv_norm, w_kv, k_norm, w_q, q_norm, sinks, w_o, router_g_w, router_g_b, router_e_w, router_e_b, exp_gate, exp_up, exp_down, ple_gate, ple_proj):
    b, seq, d = x_prompt.shape
    sb, steps, _ = x_sample.shape
    depth = norm_mix.shape[0]
    row = lambda v: v.reshape(1, -1)

    gap_w = jnp.zeros((depth, d, EXPERT_LANE0 - N_GROUPS), F32)
    tail_w = jnp.zeros((depth, d, LANES - EXPERT_LANE0 - N_EXPERTS), F32)
    router_w = jnp.concatenate(
        [router_g_w, gap_w, router_e_w.reshape(depth, d, N_EXPERTS), tail_w], axis=2)
    router_b = jnp.concatenate(
        [router_g_b, gap_w[:, 0], router_e_b.reshape(depth, N_EXPERTS), tail_w[:, 0]], axis=1)
    moe_w = {
        "norm_ffn": norm_ffn.reshape(depth, 1, d), "router_w": router_w,
        "router_b": router_b.reshape(depth, 1, LANES),
        "exp_gate": exp_gate.astype(BF16), "exp_up": exp_up.astype(BF16),
        "exp_down": exp_down.astype(BF16), "norm_ple": norm_ple.reshape(depth, 1, d),
        "ple_gate": ple_gate.astype(BF16), "ple_proj": ple_proj.astype(BF16),
    }
    two = lambda a: jnp.concatenate([a, a])
    attn_w = {
        "norm_mix1": row(norm_mix[1]), "w_q": w_q[0].astype(BF16), "q_norm": row(two(q_norm[0])),
        "sinks": sinks[0], "w_o": w_o[0].astype(BF16),
    }
    pw = pool_w[0]
    nmix0 = row(norm_mix[0])
    pscale = row(pool_scale[0])

    rope_p = _rope_table(jnp.arange(seq, dtype=jnp.int32))
    rope_s = jnp.tile(_rope_table(PAST_LEN + jnp.arange(steps, dtype=jnp.int32)), (sb, 1))
    kv_w = {"kv_norm": row(kv_norm), "w_kv": w_kv.astype(BF16), "k_norm": row(two(k_norm))}

    pp_all = p_prompt.reshape(depth, b * seq, D_PLE)
    x1p, pool16 = _mixer0_prompt(x_prompt, nmix0, pw, pscale)
    x3p, kp, vp = _moe_layer(x1p.reshape(b * seq, d), pp_all, 0, moe_w, dict(kv_w, rope=rope_p))
    kp = kp.reshape(b, seq, LANES)
    vp = vp.reshape(b, seq, LANES)
    x4p = _attn_prompt(x3p.reshape(b, seq, d), kp, vp, attn_w, rope_p)
    y_prompt = _moe_layer(x4p.reshape(b * seq, d), pp_all, 1, moe_w)[0].reshape(b, seq, d)
    pool_prompt = pool16[None, :, CTX_ROWS - POOL_CTX:, :]
    k_win_prompt = kp[:, seq - WINDOW:].reshape(b, WINDOW, N_KV_HEADS, HEAD_DIM)
    v_win_prompt = vp[:, seq - WINDOW:].reshape(b, WINDOW, N_KV_HEADS, HEAD_DIM)

    ps_all = p_sample.reshape(depth, sb * steps, D_PLE)
    x1s, pool_s = _mixer0_sample(x_sample, state_pool[0], nmix0, pw, pscale)
    x3s, ks, vs = _moe_layer(x1s.reshape(sb * steps, d), ps_all, 0, moe_w, dict(kv_w, rope=rope_s))
    kc = cache_k_win.reshape(sb, WINDOW, LANES)
    vc = cache_v_win.reshape(sb, WINDOW, LANES)
    x4s = _attn_sample(x3s, ks, vs, kc, vc, attn_w, rope_s, steps)
    y_sample = _moe_layer(x4s, ps_all, 1, moe_w)[0].reshape(sb, steps, d)
    pool_sample = pool_s[None]
    heads = (sb, steps, N_KV_HEADS, HEAD_DIM)
    k_win_sample = jnp.concatenate([cache_k_win[:, steps:], ks.reshape(heads)], axis=1)
    v_win_sample = jnp.concatenate([cache_v_win[:, steps:], vs.reshape(heads)], axis=1)

    return (y_prompt, y_sample, pool_prompt, pool_sample,
            k_win_prompt, v_win_prompt, k_win_sample, v_win_sample)
```

```python
import functools

import jax
import jax.numpy as jnp
from jax import lax
from jax.experimental import pallas as pl
from jax.experimental.pallas import tpu as pltpu

D_MODEL = 1024
PAST_LEN = 8192
POOL_WINDOWS = (2, 4, 8, 16)
POOL_GROUP_DIM = D_MODEL // len(POOL_WINDOWS)
POOL_CTX = max(POOL_WINDOWS) - 1
assert all(win == 2 ** (g + 1) for g, win in enumerate(POOL_WINDOWS))
HEAD_DIM = 64
N_HEADS = D_MODEL // HEAD_DIM
N_KV_HEADS = 2
Q_PER_KV = N_HEADS // N_KV_HEADS
WINDOW = 128
ROPE_DIM = HEAD_DIM // 4
ROPE_THETA = 500000.0
N_GROUPS = 4
E_PER_GROUP = 4
N_EXPERTS = N_GROUPS * E_PER_GROUP
D_EXPERT = 256
D_PLE = 256
EPS = 1e-6

LANES = 128
SUBLANES = 8
CTX_ROWS = 16
TOKEN_TILE = 1024
ATTN_TILE = 512
SAMPLE_SEQ_BLOCK = 16
SAMPLE_SEQ_UNROLL = 2
VMEM_LIMIT = 60 * 1024 * 1024
NEG_BIG = -1e30
LOG2_E = 1.4426950408889634
F32 = jnp.float32
BF16 = jnp.bfloat16

EXPERT_LANE0 = SUBLANES
MOE_SUBTILE = 256
SEG_ALIGN = 2 * SUBLANES
SORT_ROWS = 768
EXPERT_FIXED = 192
EXPERT_STEP = 32
EXPERTS_PER_STEP = 4
SORT_BLOCKS = SORT_ROWS // SEG_ALIGN
MOE_SUBTILES = TOKEN_TILE // MOE_SUBTILE
XE_DUMP_ROW = (2 * TOKEN_TILE + MOE_SUBTILES * N_EXPERTS * (SEG_ALIGN - 1)
               + N_EXPERTS * (EXPERT_STEP - SEG_ALIGN + EXPERT_FIXED))
META_BASE, META_ROWS = 64, 65
assert SORT_ROWS >= 2 * MOE_SUBTILE + N_EXPERTS * (SEG_ALIGN - 1) and SORT_ROWS % LANES == 0
assert TOKEN_TILE % MOE_SUBTILE == 0 and MOE_SUBTILES <= N_EXPERTS and SORT_BLOCKS <= META_BASE
assert EXPERT_FIXED % EXPERT_STEP == 0 and EXPERT_STEP % SEG_ALIGN == 0
assert N_EXPERTS % EXPERTS_PER_STEP == 0 and XE_DUMP_ROW % SEG_ALIGN == 0


def _rms(x, g):
    return x * lax.rsqrt(jnp.mean(x * x, axis=-1, keepdims=True) + EPS) * g


def _dot(a, b):
    return jnp.dot(a, b, preferred_element_type=F32)


def _dot_nt(a, b):
    return lax.dot_general(a, b, (((1,), (1,)), ((), ())), preferred_element_type=F32)


def _dot_tn(a, b):
    return lax.dot_general(a, b, (((0,), (0,)), ((), ())), preferred_element_type=F32)


def _split_bf16(x):
    hi = x.astype(BF16)
    return hi, (x - hi.astype(F32)).astype(BF16)


def _dot3(a_hi, a_lo, b):
    b_hi, b_lo = _split_bf16(b)
    return _dot(a_hi, b_hi) + _dot(a_lo, b_hi) + _dot(a_hi, b_lo)


def _sigmoid(x):
    return 1.0 / (1.0 + jnp.exp(-x))


def _rope_parts(rope_ref):
    return rope_ref[:, 0:LANES], rope_ref[:, LANES:2 * LANES], rope_ref[:, 2 * LANES:3 * LANES]


def _head_norm_rope(x, gain, cos, sin_next, sin_prev):
    lane = lax.broadcasted_iota(jnp.int32, x.shape, 1)
    first = lane < HEAD_DIM
    sq = x * x
    m0 = jnp.sum(jnp.where(first, sq, 0.0), axis=-1, keepdims=True) * (1.0 / HEAD_DIM)
    m1 = jnp.sum(jnp.where(first, 0.0, sq), axis=-1, keepdims=True) * (1.0 / HEAD_DIM)
    y = x * lax.rsqrt(jnp.where(first, m0, m1) + EPS) * gain
    half = ROPE_DIM // 2
    return (y * cos + pltpu.roll(y, LANES - half, 1) * sin_next
            + pltpu.roll(y, half, 1) * sin_prev)


def _pool_mix(h, window_sum, pos, x, pw_ref, pscale_ref, store):
    for g, win in enumerate(POOL_WINDOWS):
        cols = slice(g * POOL_GROUP_DIM, (g + 1) * POOL_GROUP_DIM)
        hg = h[..., cols]
        cnt = jnp.minimum(win, pos + 1).astype(F32)
        pooled = (window_sum(g, cols) / cnt - hg).reshape(-1, POOL_GROUP_DIM)
        hi, lo = _split_bf16(pooled)
        mixed = _dot3(hi, lo, pw_ref[g])
        store(cols, x[..., cols] + mixed.reshape(hg.shape) * pscale_ref[:, cols])


def _mixer0_prompt_kernel(x_ref, xprev_ref, nmix_ref, pw_ref, pscale_ref,
                          x1_ref, pool_ref, hs_ref, ua_ref, ub_ref):
    t = pl.program_id(1)
    tm = x_ref.shape[1]
    x = x_ref[0]
    h = _rms(x, nmix_ref[...])
    hp = _rms(xprev_ref[0], nmix_ref[...])
    top = 2 * CTX_ROWS
    hs_ref[0:CTX_ROWS, :] = jnp.zeros((CTX_ROWS, x.shape[1]), F32)
    hs_ref[CTX_ROWS:top, :] = jnp.where(t > 0, hp, 0.0)
    hs_ref[top:, :] = h
    ua_ref[0:SUBLANES, :] = jnp.zeros((SUBLANES, x.shape[1]), F32)
    ub_ref[0:SUBLANES, :] = jnp.zeros((SUBLANES, x.shape[1]), F32)

    n = tm + top - SUBLANES
    src, level_of = hs_ref, []
    for g, win in enumerate(POOL_WINDOWS):
        dst = ua_ref if g % 2 == 0 else ub_ref
        c0 = g * POOL_GROUP_DIM
        back = win // 2
        dst[SUBLANES:, c0:] = src[SUBLANES:, c0:] + src[SUBLANES - back:SUBLANES - back + n, c0:]
        level_of.append(dst)
        src = dst

    def window_sum(g, cols):
        return level_of[g][top:, cols]

    def store(cols, val):
        x1_ref[0, :, cols] = val

    pos = t * tm + lax.broadcasted_iota(jnp.int32, (tm, 1), 0)
    _pool_mix(h, window_sum, pos, x, pw_ref, pscale_ref, store)

    @pl.when(t == pl.num_programs(1) - 1)
    def _():
        pool_ref[0] = hs_ref[tm + top - CTX_ROWS:tm + top, :]


def _mixer0_sample_kernel(x_ref, ctx_ref, nmix_ref, pw_ref, pscale_ref,
                          x1_ref, pool_ref, hs_ref):
    x = x_ref[...]
    steps = x.shape[1]
    h = _rms(x, nmix_ref[...])
    hs_ref[:, CTX_ROWS - POOL_CTX:CTX_ROWS, :] = ctx_ref[...]
    hs_ref[:, CTX_ROWS:, :] = h
    pool_ref[:, 0:POOL_CTX - steps, :] = ctx_ref[:, steps:, :]
    pool_ref[:, POOL_CTX - steps:, :] = h

    def window_sum(g, cols):
        s = h[..., cols]
        for j in range(1, POOL_WINDOWS[g]):
            s = s + hs_ref[:, CTX_ROWS - j:CTX_ROWS - j + steps, cols]
        return s

    def store(cols, val):
        x1_ref[:, :, cols] = val

    pos = PAST_LEN + lax.broadcasted_iota(jnp.int32, (1, steps, 1), 1)
    _pool_mix(h, window_sum, pos, x, pw_ref, pscale_ref, store)


def _mixer0_prompt(x, nmix, pw, pscale):
    b, seq, d = x.shape
    tm = TOKEN_TILE
    nt = seq // tm
    per = tm // CTX_ROWS
    const2 = lambda bi, ti: (0, 0)
    return pl.pallas_call(
        _mixer0_prompt_kernel,
        grid=(b, nt),
        in_specs=[
            pl.BlockSpec((1, tm, d), lambda bi, ti: (bi, ti, 0)),
            pl.BlockSpec((1, CTX_ROWS, d), lambda bi, ti: (bi, jnp.maximum(ti * per - 1, 0), 0)),
            pl.BlockSpec((1, d), const2),
            pl.BlockSpec(pw.shape, lambda bi, ti: (0, 0, 0)),
            pl.BlockSpec((1, d), const2),
        ],
        out_specs=[
            pl.BlockSpec((1, tm, d), lambda bi, ti: (bi, ti, 0)),
            pl.BlockSpec((1, CTX_ROWS, d), lambda bi, ti: (bi, 0, 0)),
        ],
        out_shape=[
            jax.ShapeDtypeStruct((b, seq, d), F32),
            jax.ShapeDtypeStruct((b, CTX_ROWS, d), F32),
        ],
        scratch_shapes=[pltpu.VMEM((tm + 2 * CTX_ROWS, d), F32)] * 3,
        compiler_params=pltpu.CompilerParams(
            dimension_semantics=("arbitrary", "arbitrary"), vmem_limit_bytes=VMEM_LIMIT),
        name="mixer0_prompt",
    )(x, x, nmix, pw, pscale)


def _mixer0_sample(x, ctx, nmix, pw, pscale):
    b, steps, d = x.shape
    sb = SAMPLE_SEQ_BLOCK
    const2 = lambda i: (0, 0)
    return pl.pallas_call(
        _mixer0_sample_kernel,
        grid=(b // sb,),
        in_specs=[
            pl.BlockSpec((sb, steps, d), lambda i: (i, 0, 0)),
            pl.BlockSpec((sb, POOL_CTX, d), lambda i: (i, 0, 0)),
            pl.BlockSpec((1, d), const2),
            pl.BlockSpec(pw.shape, lambda i: (0, 0, 0)),
            pl.BlockSpec((1, d), const2),
        ],
        out_specs=[
            pl.BlockSpec((sb, steps, d), lambda i: (i, 0, 0)),
            pl.BlockSpec((sb, POOL_CTX, d), lambda i: (i, 0, 0)),
        ],
        out_shape=[
            jax.ShapeDtypeStruct((b, steps, d), F32),
            jax.ShapeDtypeStruct((b, POOL_CTX, d), F32),
        ],
        scratch_shapes=[pltpu.VMEM((sb, CTX_ROWS + steps, d), F32)],
        compiler_params=pltpu.CompilerParams(
            dimension_semantics=("arbitrary",), vmem_limit_bytes=VMEM_LIMIT),
        name="mixer0_sample",
    )(x, ctx, nmix, pw, pscale)


def _route_rows(glog, elog):
    sub8 = lax.broadcasted_iota(jnp.int32, glog.shape, 0).astype(F32)
    is_grp = sub8 < N_GROUPS
    gmax = jnp.max(jnp.where(is_grp, glog, NEG_BIG), axis=0, keepdims=True)
    gsum = jnp.sum(jnp.where(is_grp, jnp.exp(jnp.minimum(glog - gmax, 0.0)), 0.0),
                   axis=0, keepdims=True)
    g_w = 1.0 / gsum
    g_sel = jnp.min(jnp.where(is_grp & (glog == gmax), sub8, float(N_GROUPS)),
                    axis=0, keepdims=True)
    sub16 = lax.broadcasted_iota(jnp.int32, elog.shape, 0).astype(F32)
    lo = E_PER_GROUP * g_sel
    in_grp = (sub16 >= lo) & (sub16 < lo + E_PER_GROUP)
    none = float(N_EXPERTS)
    v1 = jnp.max(jnp.where(in_grp, elog, NEG_BIG), axis=0, keepdims=True)
    i1 = jnp.min(jnp.where(in_grp & (elog == v1), sub16, none), axis=0, keepdims=True)
    rest = in_grp & (sub16 != i1)
    v2 = jnp.max(jnp.where(rest, elog, NEG_BIG), axis=0, keepdims=True)
    i2 = jnp.min(jnp.where(rest & (elog == v2), sub16, none), axis=0, keepdims=True)
    e2 = jnp.exp(v2 - v1)
    den = 1.0 + e2
    return i1, i2, (1.0 / den) * g_w, (e2 / den) * g_w


def _dispatch(x1_ref, nffn_ref, rw_ref, rb_ref,
              xe_s, perm_s, gsort_s, meta_v, meta_sm, sem):
    tm = x1_ref.shape[0]
    sub = MOE_SUBTILE
    nsub = tm // sub
    hi, lo = _split_bf16(_rms(x1_ref[...], nffn_ref[...]))
    w_hi, w_lo = _split_bf16(rw_ref[...])
    wide = _dot(hi, jnp.concatenate([w_hi, w_lo], axis=1))
    logits = wide[:, :LANES] + wide[:, LANES:] + _dot(lo, w_hi) + rb_ref[...]
    lt = logits.T
    i1, i2, w1, w2 = _route_rows(lt[0:SUBLANES], lt[EXPERT_LANE0:EXPERT_LANE0 + N_EXPERTS])

    sub16 = lax.broadcasted_iota(jnp.int32, (N_EXPERTS, 2 * sub), 0).astype(F32)
    esub = lax.broadcasted_iota(jnp.int32, (N_EXPERTS, LANES), 0)
    mlane = lax.broadcasted_iota(jnp.int32, (N_EXPERTS, LANES), 1)

    def round_up(x, step):
        return jnp.floor((x + (step - 1)) * (1.0 / step)) * step

    def starts(sizes):
        inc = sizes
        for k in (1, 2, 4, 8):
            inc = inc + jnp.where(esub >= k, pltpu.roll(inc, k, 0), 0.0)
        return inc - sizes

    onehots, sizes = [], []
    for s in range(nsub):
        sl = slice(s * sub, (s + 1) * sub)
        sel = jnp.concatenate([i1[:, sl], i2[:, sl]], axis=1)
        at = jnp.where(sub16 == sel, 1.0, 0.0)
        cnt = jnp.sum(at, axis=1, keepdims=True)
        onehots.append(at)
        sizes.append(jnp.broadcast_to(round_up(cnt, SEG_ALIGN), (N_EXPERTS, LANES)))

    region = jnp.maximum(round_up(sum(sizes), EXPERT_STEP), float(EXPERT_FIXED))
    base = starts(region)
    meta = jnp.where(mlane == META_BASE, base, jnp.where(mlane == META_ROWS, region, 0.0))
    blk0 = mlane.astype(F32) * SEG_ALIGN
    seg_starts, at_row = [], base
    for s in range(nsub):
        seg0 = starts(sizes[s])
        seg_starts.append(seg0)
        inside = (blk0 >= seg0) & (blk0 < seg0 + sizes[s])
        dst = jnp.sum(jnp.where(inside, at_row - seg0 + blk0, 0.0), axis=0, keepdims=True)
        used = jnp.sum(jnp.where(inside, 1.0, 0.0), axis=0, keepdims=True)
        dst = jnp.where(used > 0.0, dst, float(XE_DUMP_ROW))
        meta = jnp.where((esub == s) & (mlane < SORT_BLOCKS), dst, meta)
        at_row = at_row + sizes[s]
    meta_v[...] = meta.astype(jnp.int32)
    cp = pltpu.make_async_copy(meta_v, meta_sm, sem)
    cp.start()

    jr = lax.broadcasted_iota(jnp.int32, (2 * sub, 2 * sub), 0)
    jc = lax.broadcasted_iota(jnp.int32, (2 * sub, 2 * sub), 1)
    earlier = jnp.where(jr < jc, 1.0, 0.0).astype(BF16)
    rsub = lax.broadcasted_iota(jnp.int32, (SORT_ROWS, sub), 0).astype(F32)
    for s in range(nsub):
        sl = slice(s * sub, (s + 1) * sub)
        at = onehots[s]
        rank = _dot(at.astype(BF16), earlier)
        dest = jnp.sum(at * (seg_starts[s][:, 0:1] + rank), axis=0, keepdims=True)
        p1 = rsub == dest[:, :sub]
        p2 = rsub == dest[:, sub:]
        perm = jnp.where(p1 | p2, 1.0, 0.0).astype(BF16)
        gsort_s[s] = jnp.sum(jnp.where(p1, w1[:, sl], 0.0) + jnp.where(p2, w2[:, sl], 0.0),
                             axis=1, keepdims=True)
        perm_s[s] = perm
        xs = _dot(perm, hi[sl, :]).astype(BF16)
        if s == 0:
            cp.wait()
        for b in range(SORT_BLOCKS):
            row = pl.multiple_of(meta_sm[s, b], SEG_ALIGN)
            xe_s[pl.ds(row, SEG_ALIGN), :] = xs[b * SEG_ALIGN:(b + 1) * SEG_ALIGN, :]


def _expert_group(first, xe_s, act_s, meta_sm, wg_ref, wu_ref, wd_ref):
    def hidden(j, r0, m):
        x = xe_s[pl.ds(r0, m), :]
        gt = _dot(x, wg_ref[j])
        up = _dot(x, wu_ref[j])
        return ((gt * _sigmoid(gt)) * up).astype(BF16)

    bases = [pl.multiple_of(meta_sm[first + j, META_BASE], EXPERT_STEP)
             for j in range(EXPERTS_PER_STEP)]
    for j in range(EXPERTS_PER_STEP):
        act_s[j] = hidden(j, bases[j], EXPERT_FIXED)
    for j in range(EXPERTS_PER_STEP):
        xe_s[pl.ds(bases[j], EXPERT_FIXED), :] = _dot(act_s[j], wd_ref[j]).astype(BF16)
    for j in range(EXPERTS_PER_STEP):
        def piece(c, carry, j=j):
            r0 = pl.multiple_of(bases[j] + EXPERT_FIXED + c * EXPERT_STEP, EXPERT_STEP)
            xe_s[pl.ds(r0, EXPERT_STEP), :] = _dot(hidden(j, r0, EXPERT_STEP), wd_ref[j]).astype(BF16)
            return carry
        lax.fori_loop(0, (meta_sm[first + j, META_ROWS] - EXPERT_FIXED) // EXPERT_STEP, piece, 0)


def _moe_kernel(with_kv, *refs):
    if with_kv:
        (x1_ref, p_ref, nffn_ref, rw_ref, rb_ref, wg_ref, wu_ref, wd_ref,
         nple_ref, pg_ref, pp_ref, kvn_ref, wkv_ref, kn_ref, rope_ref,
         out_ref, k_ref, v_ref, xe_s, ys_s, act_s, perm_s, gsort_s, meta_v, meta_sm, sem) = refs
    else:
        (x1_ref, p_ref, nffn_ref, rw_ref, rb_ref, wg_ref, wu_ref, wd_ref,
         nple_ref, pg_ref, pp_ref,
         out_ref, xe_s, ys_s, act_s, perm_s, gsort_s, meta_v, meta_sm, sem) = refs
    i = pl.program_id(0)
    step = pl.program_id(1)
    nsub = x1_ref.shape[0] // MOE_SUBTILE

    @pl.when((i == 0) & (step == 0))
    def _():
        xe_s[...] = jnp.zeros_like(xe_s)

    @pl.when(step == 0)
    def _():
        _dispatch(x1_ref, nffn_ref, rw_ref, rb_ref,
                  xe_s, perm_s, gsort_s, meta_v, meta_sm, sem)

    _expert_group(step * EXPERTS_PER_STEP, xe_s, act_s, meta_sm, wg_ref, wu_ref, wd_ref)

    @pl.when(step == pl.num_programs(1) - 1)
    def _():
        for s in range(nsub):
            sl = slice(s * MOE_SUBTILE, (s + 1) * MOE_SUBTILE)
            for b in range(SORT_BLOCKS):
                rows = slice(b * SEG_ALIGN, (b + 1) * SEG_ALIGN)
                src = pl.multiple_of(meta_sm[s, b], SEG_ALIGN)
                blk = xe_s[pl.ds(src, SEG_ALIGN), :].astype(F32) * gsort_s[s, rows, :]
                ys_s[rows, :] = blk.astype(BF16)
            out_ref[sl, :] = x1_ref[sl, :] + _dot_tn(perm_s[s], ys_s[...])
        x2 = out_ref[...]
        hp = _rms(x2, nple_ref[...]).astype(BF16)
        gate = _sigmoid(_dot(hp, pg_ref[...]))
        proj = _dot(p_ref[...].astype(BF16), pp_ref[...])
        x3 = x2 + gate * proj
        out_ref[...] = x3
        if with_kv:
            hk = _rms(x3, kvn_ref[...]).astype(BF16)
            kv = _dot(hk, wkv_ref[...])
            k_ref[...] = _head_norm_rope(kv[:, :LANES], kn_ref[...], *_rope_parts(rope_ref))
            v_ref[...] = kv[:, LANES:]


def _moe_layer(x1, p_all, layer, w, kv=None):
    n, d = x1.shape
    tm = TOKEN_TILE
    nt = n // tm
    nsub = tm // MOE_SUBTILE
    tok = lambda i, e: (i, 0)
    const2 = lambda i, e: (0, 0)
    lay3 = lambda i, e: (layer, 0, 0)
    exp4 = lambda i, e: (layer, e, 0, 0)
    in_specs = [
        pl.BlockSpec((tm, d), tok),
        pl.BlockSpec((None, tm, D_PLE), lambda i, e: (layer, i, 0)),
        pl.BlockSpec((None, 1, d), lay3),
        pl.BlockSpec((None, d, LANES), lay3),
        pl.BlockSpec((None, 1, LANES), lay3),
        pl.BlockSpec((None, EXPERTS_PER_STEP, d, D_EXPERT), exp4),
        pl.BlockSpec((None, EXPERTS_PER_STEP, d, D_EXPERT), exp4),
        pl.BlockSpec((None, EXPERTS_PER_STEP, D_EXPERT, d), exp4),
        pl.BlockSpec((None, 1, d), lay3),
        pl.BlockSpec((None, d, d), lay3),
        pl.BlockSpec((None, D_PLE, d), lay3),
    ]
    args = [x1, p_all, w["norm_ffn"], w["router_w"], w["router_b"],
            w["exp_gate"], w["exp_up"], w["exp_down"], w["norm_ple"], w["ple_gate"], w["ple_proj"]]
    out_specs = [pl.BlockSpec((tm, d), tok)]
    out_shape = [jax.ShapeDtypeStruct((n, d), F32)]
    if kv is not None:
        tab_blocks = kv["rope"].shape[0] // tm
        in_specs += [
            pl.BlockSpec((1, d), const2),
            pl.BlockSpec((d, 2 * LANES), const2),
            pl.BlockSpec((1, LANES), const2),
            pl.BlockSpec((tm, 3 * LANES), lambda i, e: (i % tab_blocks, 0)),
        ]
        args += [kv["kv_norm"], kv["w_kv"], kv["k_norm"], kv["rope"]]
        out_specs += [pl.BlockSpec((tm, LANES), tok), pl.BlockSpec((tm, LANES), tok)]
        out_shape += [jax.ShapeDtypeStruct((n, LANES), F32), jax.ShapeDtypeStruct((n, LANES), F32)]
    return pl.pallas_call(
        functools.partial(_moe_kernel, kv is not None),
        grid=(nt, N_EXPERTS // EXPERTS_PER_STEP),
        in_specs=in_specs,
        out_specs=out_specs,
        out_shape=out_shape,
        scratch_shapes=[
            pltpu.VMEM((XE_DUMP_ROW + SEG_ALIGN, d), BF16),
            pltpu.VMEM((SORT_ROWS, d), BF16),
            pltpu.VMEM((EXPERTS_PER_STEP, EXPERT_FIXED, D_EXPERT), BF16),
            pltpu.VMEM((nsub, SORT_ROWS, MOE_SUBTILE), BF16),
            pltpu.VMEM((nsub, SORT_ROWS, 1), F32),
            pltpu.VMEM((N_EXPERTS, LANES), jnp.int32),
            pltpu.SMEM((N_EXPERTS, LANES), jnp.int32),
            pltpu.SemaphoreType.DMA,
        ],
        compiler_params=pltpu.CompilerParams(
            dimension_semantics=("arbitrary", "arbitrary"), vmem_limit_bytes=VMEM_LIMIT),
        name="moe_kv" if kv is not None else "moe",
    )(*args)


def _project_q(x3, nmix_ref, wq_ref, qn_ref, cos, sin_next, sin_prev, q_s):
    h = _rms(x3, nmix_ref[...]).astype(BF16)
    wide = 2 * LANES
    for blk in range(D_MODEL // wide):
        q = _dot(h, wq_ref[:, blk * wide:(blk + 1) * wide])
        for c in range(wide // LANES):
            cols = slice(blk * wide + c * LANES, blk * wide + (c + 1) * LANES)
            qc = _head_norm_rope(q[:, c * LANES:(c + 1) * LANES], qn_ref[...],
                                 cos, sin_next, sin_prev)
            q_s[:, cols] = (qc * (HEAD_DIM ** -0.5 * LOG2_E)).astype(q_s.dtype)


def _both_orders(x):
    return x.astype(BF16), pltpu.roll(x, HEAD_DIM, 1).astype(BF16)


def _kv_order(par, g):
    return 0 if par == g else 1


def _softmax_sink(s, mask, sink):
    sink = sink * LOG2_E
    s = jnp.where(mask, s, NEG_BIG)
    m = jnp.maximum(jnp.max(s, axis=-1, keepdims=True), sink)
    p = jnp.exp2(s - m)
    den = jnp.sum(p, axis=-1, keepdims=True) + jnp.exp2(sink - m)
    return p.astype(BF16), 1.0 / den


def _attn_prompt_kernel(x3_ref, k_ref, v_ref, kprev_ref, vprev_ref, nmix_ref, wq_ref, qn_ref,
                        rope_ref, sinks_ref, wo_ref,
                        x4_ref, q_s, o_s, kd_s, vd_s):
    t = pl.program_id(1)
    tq = x3_ref.shape[1]
    x3 = x3_ref[0]
    _project_q(x3, nmix_ref, wq_ref, qn_ref, *_rope_parts(rope_ref), q_s)

    for src_prev, src, dst in ((kprev_ref, k_ref, kd_s), (vprev_ref, v_ref, vd_s)):
        dst[0, 0:WINDOW, :], dst[1, 0:WINDOW, :] = _both_orders(src_prev[0])
        dst[0, WINDOW:, :], dst[1, WINDOW:, :] = _both_orders(src[0])

    qi = lax.broadcasted_iota(jnp.int32, (WINDOW, 2 * WINDOW), 0)
    ki = lax.broadcasted_iota(jnp.int32, (WINDOW, 2 * WINDOW), 1)
    band = (ki > qi) & (ki <= qi + WINDOW)
    lane = lax.broadcasted_iota(jnp.int32, (WINDOW, LANES), 1)
    first = lane < HEAD_DIM

    def q_block(j, carry):
        r0 = pl.multiple_of(j * WINDOW, WINDOW)
        mask = band & (ki >= (1 - j) * WINDOW - t * tq)
        for c in range(D_MODEL // LANES):
            g = (2 * c) // Q_PER_KV
            cols = slice(c * LANES, (c + 1) * LANES)
            qc = q_s[pl.ds(r0, WINDOW), cols]
            outs = []
            for par in range(2):
                order = _kv_order(par, g)
                qh = jnp.where(first, qc, 0) if par == 0 else jnp.where(first, 0, qc)
                s = _dot_nt(qh, kd_s[order, pl.ds(r0, 2 * WINDOW), :])
                pr, inv = _softmax_sink(s, mask, sinks_ref[2 * c + par])
                outs.append(_dot(pr, vd_s[order, pl.ds(r0, 2 * WINDOW), :]) * inv)
            o_s[pl.ds(r0, WINDOW), cols] = jnp.where(first, outs[0], outs[1]).astype(BF16)
        return carry

    lax.fori_loop(0, tq // WINDOW, q_block, 0)
    x4_ref[0] = x3 + _dot(o_s[...], wo_ref[...])


def _attn_prompt(x3, k, v, w, rope):
    b, seq, d = x3.shape
    tq = ATTN_TILE
    nt = seq // tq
    per = tq // WINDOW
    tile = lambda bi, ti: (bi, ti, 0)
    prev = lambda bi, ti: (bi, jnp.maximum(ti * per - 1, 0), 0)
    const2 = lambda bi, ti: (0, 0)
    tab = lambda bi, ti: (ti, 0)
    return pl.pallas_call(
        _attn_prompt_kernel,
        grid=(b, nt),
        in_specs=[
            pl.BlockSpec((1, tq, d), tile),
            pl.BlockSpec((1, tq, LANES), tile),
            pl.BlockSpec((1, tq, LANES), tile),
            pl.BlockSpec((1, WINDOW, LANES), prev),
            pl.BlockSpec((1, WINDOW, LANES), prev),
            pl.BlockSpec((1, d), const2),
            pl.BlockSpec((d, d), const2),
            pl.BlockSpec((1, LANES), const2),
            pl.BlockSpec((tq, 3 * LANES), tab),
            pl.BlockSpec(memory_space=pltpu.SMEM),
            pl.BlockSpec((d, d), const2),
        ],
        out_specs=pl.BlockSpec((1, tq, d), tile),
        out_shape=jax.ShapeDtypeStruct((b, seq, d), F32),
        scratch_shapes=[
            pltpu.VMEM((tq, d), BF16),
            pltpu.VMEM((tq, d), BF16),
            pltpu.VMEM((N_KV_HEADS, tq + WINDOW, LANES), BF16),
            pltpu.VMEM((N_KV_HEADS, tq + WINDOW, LANES), BF16),
        ],
        compiler_params=pltpu.CompilerParams(
            dimension_semantics=("arbitrary", "arbitrary"), vmem_limit_bytes=VMEM_LIMIT),
        name="attn_prompt",
    )(x3, k, v, k, v, w["norm_mix1"], w["w_q"], w["q_norm"], rope, w["sinks"], w["w_o"])


def _attn_sample_kernel(x3_ref, knew_ref, vnew_ref, kc_ref, vc_ref, nmix_ref, wq_ref, qn_ref,
                        rope_ref, sinks_ref, wo_ref,
                        x4_ref, q_s, o_s):
    rows = x3_ref.shape[0]
    steps = knew_ref.shape[0] // kc_ref.shape[0]
    x3 = x3_ref[...]
    _project_q(x3, nmix_ref, wq_ref, qn_ref, *_rope_parts(rope_ref), q_s)

    n_chunks = D_MODEL // LANES
    stack = n_chunks * steps
    ri = lax.broadcasted_iota(jnp.int32, (stack, 2 * WINDOW), 0)
    ki = lax.broadcasted_iota(jnp.int32, (stack, 2 * WINDOW), 1)
    tq = ri % steps
    mask = ((ki < WINDOW) & (ki > tq)) | ((ki >= WINDOW) & ((ki - WINDOW) <= tq))
    def par_of(order, c):
        g = (2 * c) // Q_PER_KV
        return g if order == 0 else 1 - g

    chunk_of_row = lax.broadcasted_iota(jnp.int32, (stack, 1), 0) // steps
    sink_cols = []
    for order in range(2):
        col = jnp.zeros((stack, 1), F32)
        for c in range(n_chunks):
            col = jnp.where(chunk_of_row == c, sinks_ref[2 * c + par_of(order, c)], col)
        sink_cols.append(col)
    lane = lax.broadcasted_iota(jnp.int32, (steps, LANES), 1)
    first = lane < HEAD_DIM
    pad = jnp.zeros((WINDOW - steps, LANES), F32)

    def one_seq(b):
        r0 = pl.multiple_of(b * steps, steps)
        keys = _both_orders(jnp.concatenate([kc_ref[b], knew_ref[pl.ds(r0, steps), :], pad], axis=0))
        vals = _both_orders(jnp.concatenate([vc_ref[b], vnew_ref[pl.ds(r0, steps), :], pad], axis=0))
        qcs = [q_s[pl.ds(r0, steps), c * LANES:(c + 1) * LANES] for c in range(n_chunks)]
        outs = []
        for order in range(2):
            slabs = [jnp.where(first, qc, 0.0) if par_of(order, c) == 0 else jnp.where(first, 0.0, qc)
                     for c, qc in enumerate(qcs)]
            s = _dot_nt(jnp.concatenate(slabs, axis=0).astype(BF16), keys[order])
            pr, inv = _softmax_sink(s, mask, sink_cols[order])
            outs.append(_dot(pr, vals[order]) * inv)
        for c in range(n_chunks):
            lo_half = outs[0] if par_of(0, c) == 0 else outs[1]
            hi_half = outs[1] if par_of(0, c) == 0 else outs[0]
            o_s[pl.ds(r0, steps), c * LANES:(c + 1) * LANES] = jnp.where(
                first, lo_half[c * steps:(c + 1) * steps], hi_half[c * steps:(c + 1) * steps])

    def seq_pair(i, carry):
        for u in range(SAMPLE_SEQ_UNROLL):
            one_seq(i * SAMPLE_SEQ_UNROLL + u)
        return carry

    lax.fori_loop(0, rows // steps // SAMPLE_SEQ_UNROLL, seq_pair, 0)
    x4_ref[...] = x3 + _dot(o_s[...].astype(BF16), wo_ref[...])


def _attn_sample(x3, knew, vnew, kcache, vcache, w, rope, steps):
    n, d = x3.shape
    sb = SAMPLE_SEQ_BLOCK
    rows = sb * steps
    tok = lambda i: (i, 0)
    const2 = lambda i: (0, 0)
    return pl.pallas_call(
        _attn_sample_kernel,
        grid=(n // rows,),
        in_specs=[
            pl.BlockSpec((rows, d), tok),
            pl.BlockSpec((rows, LANES), tok),
            pl.BlockSpec((rows, LANES), tok),
            pl.BlockSpec((sb, WINDOW, LANES), lambda i: (i, 0, 0)),
            pl.BlockSpec((sb, WINDOW, LANES), lambda i: (i, 0, 0)),
            pl.BlockSpec((1, d), const2),
            pl.BlockSpec((d, d), const2),
            pl.BlockSpec((1, LANES), const2),
            pl.BlockSpec((rows, 3 * LANES), tok),
            pl.BlockSpec(memory_space=pltpu.SMEM),
            pl.BlockSpec((d, d), const2),
        ],
        out_specs=pl.BlockSpec((rows, d), tok),
        out_shape=jax.ShapeDtypeStruct((n, d), F32),
        scratch_shapes=[
            pltpu.VMEM((rows, d), F32),
            pltpu.VMEM((rows, d), F32),
        ],
        compiler_params=pltpu.CompilerParams(
            dimension_semantics=("arbitrary",), vmem_limit_bytes=VMEM_LIMIT),
        name="attn_sample",
    )(x3, knew, vnew, kcache, vcache, w["norm_mix1"], w["w_q"], w["q_norm"], rope,
      w["sinks"], w["w_o"])


def _rope_table(pos):
    half = ROPE_DIM // 2
    dim = jnp.arange(LANES, dtype=jnp.int32) % HEAD_DIM
    inv = jnp.float32(ROPE_THETA) ** (-((dim % half).astype(jnp.float32) * 2.0 / ROPE_DIM))
    inv = jnp.where(dim < ROPE_DIM, inv, 0.0)
    ang = pos.astype(jnp.float32)[:, None] * inv[None, :]
    cos, sin = jnp.cos(ang), jnp.sin(ang)
    s_next = jnp.where(dim < half, -sin, 0.0)
    s_prev = jnp.where((dim >= half) & (dim < ROPE_DIM), sin, 0.0)
    return jnp.concatenate([cos, s_next, s_prev], axis=1)


def kernel(x_prompt, x_sample, p_prompt, p_sample, state_pool, cache_k_win, cache_v_win, norm_mix, norm_ffn, norm_ple, pool_w, pool_scale, kv_norm, w_kv, k_norm, w_q, q_norm, sinks, w_o, router_g_w, router_g_b, router_e_w, router_e_b, exp_gate, exp_up, exp_down, ple_gate, ple_proj):
    b, seq, d = x_prompt.shape
    sb, steps, _ = x_sample.shape
    depth = norm_mix.shape[0]
    row = lambda v: v.reshape(1, -1)

    gap_w = jnp.zeros((depth, d, EXPERT_LANE0 - N_GROUPS), F32)
    tail_w = jnp.zeros((depth, d, LANES - EXPERT_LANE0 - N_EXPERTS), F32)
    router_w = jnp.concatenate(
        [router_g_w, gap_w, router_e_w.reshape(depth, d, N_EXPERTS), tail_w], axis=2)
    router_b = jnp.concatenate(
        [router_g_b, gap_w[:, 0], router_e_b.reshape(depth, N_EXPERTS), tail_w[:, 0]], axis=1)
    moe_w = {
        "norm_ffn": norm_ffn.reshape(depth, 1, d), "router_w": router_w,
        "router_b": router_b.reshape(depth, 1, LANES),
        "exp_gate": exp_gate.astype(BF16), "exp_up": exp_up.astype(BF16),
        "exp_down": exp_down.astype(BF16), "norm_ple": norm_ple.reshape(depth, 1, d),
        "ple_gate": ple_gate.astype(BF16), "ple_proj": ple_proj.astype(BF16),
    }
    two = lambda a: jnp.concatenate([a, a])
    attn_w = {
        "norm_mix1": row(norm_mix[1]), "w_q": w_q[0].astype(BF16), "q_norm": row(two(q_norm[0])),
        "sinks": sinks[0], "w_o": w_o[0].astype(BF16),
    }
    pw = pool_w[0]
    nmix0 = row(norm_mix[0])
    pscale = row(pool_scale[0])

    rope_p = _rope_table(jnp.arange(seq, dtype=jnp.int32))
    rope_s = jnp.tile(_rope_table(PAST_LEN + jnp.arange(steps, dtype=jnp.int32)), (sb, 1))
    kv_w = {"kv_norm": row(kv_norm), "w_kv": w_kv.astype(BF16), "k_norm": row(two(k_norm))}

    pp_all = p_prompt.reshape(depth, b * seq, D_PLE)
    x1p, pool16 = _mixer0_prompt(x_prompt, nmix0, pw, pscale)
    x3p, kp, vp = _moe_layer(x1p.reshape(b * seq, d), pp_all, 0, moe_w, dict(kv_w, rope=rope_p))
    kp = kp.reshape(b, seq, LANES)
    vp = vp.reshape(b, seq, LANES)
    x4p = _attn_prompt(x3p.reshape(b, seq, d), kp, vp, attn_w, rope_p)
    y_prompt = _moe_layer(x4p.reshape(b * seq, d), pp_all, 1, moe_w)[0].reshape(b, seq, d)
    pool_prompt = pool16[None, :, CTX_ROWS - POOL_CTX:, :]
    k_win_prompt = kp[:, seq - WINDOW:].reshape(b, WINDOW, N_KV_HEADS, HEAD_DIM)
    v_win_prompt = vp[:, seq - WINDOW:].reshape(b, WINDOW, N_KV_HEADS, HEAD_DIM)

    ps_all = p_sample.reshape(depth, sb * steps, D_PLE)
    x1s, pool_s = _mixer0_sample(x_sample, state_pool[0], nmix0, pw, pscale)
    x3s, ks, vs = _moe_layer(x1s.reshape(sb * steps, d), ps_all, 0, moe_w, dict(kv_w, rope=rope_s))
    kc = cache_k_win.reshape(sb, WINDOW, LANES)
    vc = cache_v_win.reshape(sb, WINDOW, LANES)
    x4s = _attn_sample(x3s, ks, vs, kc, vc, attn_w, rope_s, steps)
    y_sample = _moe_layer(x4s, ps_all, 1, moe_w)[0].reshape(sb, steps, d)
    pool_sample = pool_s[None]
    heads = (sb, steps, N_KV_HEADS, HEAD_DIM)
    k_win_sample = jnp.concatenate([cache_k_win[:, steps:], ks.reshape(heads)], axis=1)
    v_win_sample = jnp.concatenate([cache_v_win[:, steps:], vs.reshape(heads)], axis=1)

    return (y_prompt, y_sample, pool_prompt, pool_sample,
            k_win_prompt, v_win_prompt, k_win_sample, v_win_sample)
```

```python
import functools

import jax
import jax.numpy as jnp
from jax import lax
from jax.experimental import pallas as pl
from jax.experimental.pallas import tpu as pltpu

D_MODEL = 1024
PAST_LEN = 8192
POOL_WINDOWS = (2, 4, 8, 16)
POOL_GROUP_DIM = D_MODEL // len(POOL_WINDOWS)
POOL_CTX = max(POOL_WINDOWS) - 1
assert all(win == 2 ** (g + 1) for g, win in enumerate(POOL_WINDOWS))
HEAD_DIM = 64
N_HEADS = D_MODEL // HEAD_DIM
N_KV_HEADS = 2
Q_PER_KV = N_HEADS // N_KV_HEADS
WINDOW = 128
ROPE_DIM = HEAD_DIM // 4
ROPE_THETA = 500000.0
N_GROUPS = 4
E_PER_GROUP = 4
N_EXPERTS = N_GROUPS * E_PER_GROUP
D_EXPERT = 256
D_PLE = 256
EPS = 1e-6

LANES = 128
SUBLANES = 8
CTX_ROWS = 16
TOKEN_TILE = 1024
ATTN_TILE = 512
SAMPLE_SEQ_BLOCK = 16
SAMPLE_SEQ_UNROLL = 2
VMEM_LIMIT = 60 * 1024 * 1024
NEG_BIG = -1e30
LOG2_E = 1.4426950408889634
F32 = jnp.float32
BF16 = jnp.bfloat16

EXPERT_LANE0 = SUBLANES
MOE_SUBTILE = 256
SEG_ALIGN = 2 * SUBLANES
SORT_ROWS = 768
EXPERT_FIXED = 192
EXPERT_STEP = 32
EXPERTS_PER_STEP = 4
SORT_BLOCKS = SORT_ROWS // SEG_ALIGN
MOE_SUBTILES = TOKEN_TILE // MOE_SUBTILE
XE_DUMP_ROW = (2 * TOKEN_TILE + MOE_SUBTILES * N_EXPERTS * (SEG_ALIGN - 1)
               + N_EXPERTS * (EXPERT_STEP - SEG_ALIGN + EXPERT_FIXED))
META_BASE, META_ROWS = 64, 65
assert SORT_ROWS >= 2 * MOE_SUBTILE + N_EXPERTS * (SEG_ALIGN - 1) and SORT_ROWS % LANES == 0
assert TOKEN_TILE % MOE_SUBTILE == 0 and MOE_SUBTILES <= N_EXPERTS and SORT_BLOCKS <= META_BASE
assert EXPERT_FIXED % EXPERT_STEP == 0 and EXPERT_STEP % SEG_ALIGN == 0
assert N_EXPERTS % EXPERTS_PER_STEP == 0 and XE_DUMP_ROW % SEG_ALIGN == 0


def _rms(x, g):
    return x * lax.rsqrt(jnp.mean(x * x, axis=-1, keepdims=True) + EPS) * g


def _dot(a, b):
    return jnp.dot(a, b, preferred_element_type=F32)


def _dot_nt(a, b):
    return lax.dot_general(a, b, (((1,), (1,)), ((), ())), preferred_element_type=F32)


def _dot_tn(a, b):
    return lax.dot_general(a, b, (((0,), (0,)), ((), ())), preferred_element_type=F32)


def _split_bf16(x):
    hi = x.astype(BF16)
    return hi, (x - hi.astype(F32)).astype(BF16)


def _dot3(a_hi, a_lo, b):
    b_hi, b_lo = _split_bf16(b)
    return _dot(a_hi, b_hi) + _dot(a_lo, b_hi) + _dot(a_hi, b_lo)


def _sigmoid(x):
    return 1.0 / (1.0 + jnp.exp(-x))


def _rope_parts(rope_ref):
    return rope_ref[:, 0:LANES], rope_ref[:, LANES:2 * LANES], rope_ref[:, 2 * LANES:3 * LANES]


def _same_head():
    hi = lax.broadcasted_iota(jnp.int32, (LANES, LANES), 0) // HEAD_DIM
    hj = lax.broadcasted_iota(jnp.int32, (LANES, LANES), 1) // HEAD_DIM
    return jnp.where(hi == hj, 1.0, 0.0).astype(BF16)


def _head_norm_rope(x, gain, same_head, cos, sin_next, sin_prev, two_piece):
    sq_hi, sq_lo = _split_bf16(x * x)
    ssq = _dot(sq_hi, same_head)
    if two_piece:
        ssq = ssq + _dot(sq_lo, same_head)
    y = x * lax.rsqrt(ssq * (1.0 / HEAD_DIM) + EPS) * gain
    half = ROPE_DIM // 2
    return (y * cos + pltpu.roll(y, LANES - half, 1) * sin_next
            + pltpu.roll(y, half, 1) * sin_prev)


def _pool_mix(h, window_sum, pos, x, pw_ref, pscale_ref, store):
    for g, win in enumerate(POOL_WINDOWS):
        cols = slice(g * POOL_GROUP_DIM, (g + 1) * POOL_GROUP_DIM)
        hg = h[..., cols]
        cnt = jnp.minimum(win, pos + 1).astype(F32)
        pooled = (window_sum(g, cols) / cnt - hg).reshape(-1, POOL_GROUP_DIM)
        hi, lo = _split_bf16(pooled)
        mixed = _dot3(hi, lo, pw_ref[g])
        store(cols, x[..., cols] + mixed.reshape(hg.shape) * pscale_ref[:, cols])


def _mixer0_prompt_kernel(x_ref, xprev_ref, nmix_ref, pw_ref, pscale_ref,
                          x1_ref, pool_ref, hs_ref, ua_ref, ub_ref):
    t = pl.program_id(1)
    tm = x_ref.shape[1]
    x = x_ref[0]
    h = _rms(x, nmix_ref[...])
    hp = _rms(xprev_ref[0], nmix_ref[...])
    top = 2 * CTX_ROWS
    hs_ref[0:CTX_ROWS, :] = jnp.zeros((CTX_ROWS, x.shape[1]), F32)
    hs_ref[CTX_ROWS:top, :] = jnp.where(t > 0, hp, 0.0)
    hs_ref[top:, :] = h
    ua_ref[0:SUBLANES, :] = jnp.zeros((SUBLANES, x.shape[1]), F32)
    ub_ref[0:SUBLANES, :] = jnp.zeros((SUBLANES, x.shape[1]), F32)

    n = tm + top - SUBLANES
    src, level_of = hs_ref, []
    for g, win in enumerate(POOL_WINDOWS):
        dst = ua_ref if g % 2 == 0 else ub_ref
        c0 = g * POOL_GROUP_DIM
        back = win // 2
        dst[SUBLANES:, c0:] = src[SUBLANES:, c0:] + src[SUBLANES - back:SUBLANES - back + n, c0:]
        level_of.append(dst)
        src = dst

    def window_sum(g, cols):
        return level_of[g][top:, cols]

    def store(cols, val):
        x1_ref[0, :, cols] = val

    pos = t * tm + lax.broadcasted_iota(jnp.int32, (tm, 1), 0)
    _pool_mix(h, window_sum, pos, x, pw_ref, pscale_ref, store)

    @pl.when(t == pl.num_programs(1) - 1)
    def _():
        pool_ref[0] = hs_ref[tm + top - CTX_ROWS:tm + top, :]


def _mixer0_sample_kernel(x_ref, ctx_ref, nmix_ref, pw_ref, pscale_ref,
                          x1_ref, pool_ref, hs_ref):
    x = x_ref[...]
    steps = x.shape[1]
    h = _rms(x, nmix_ref[...])
    hs_ref[:, CTX_ROWS - POOL_CTX:CTX_ROWS, :] = ctx_ref[...]
    hs_ref[:, CTX_ROWS:, :] = h
    pool_ref[:, 0:POOL_CTX - steps, :] = ctx_ref[:, steps:, :]
    pool_ref[:, POOL_CTX - steps:, :] = h

    def window_sum(g, cols):
        s = h[..., cols]
        for j in range(1, POOL_WINDOWS[g]):
            s = s + hs_ref[:, CTX_ROWS - j:CTX_ROWS - j + steps, cols]
        return s

    def store(cols, val):
        x1_ref[:, :, cols] = val

    pos = PAST_LEN + lax.broadcasted_iota(jnp.int32, (1, steps, 1), 1)
    _pool_mix(h, window_sum, pos, x, pw_ref, pscale_ref, store)


def _mixer0_prompt(x, nmix, pw, pscale):
    b, seq, d = x.shape
    tm = TOKEN_TILE
    nt = seq // tm
    per = tm // CTX_ROWS
    const2 = lambda bi, ti: (0, 0)
    return pl.pallas_call(
        _mixer0_prompt_kernel,
        grid=(b, nt),
        in_specs=[
            pl.BlockSpec((1, tm, d), lambda bi, ti: (bi, ti, 0)),
            pl.BlockSpec((1, CTX_ROWS, d), lambda bi, ti: (bi, jnp.maximum(ti * per - 1, 0), 0)),
            pl.BlockSpec((1, d), const2),
            pl.BlockSpec(pw.shape, lambda bi, ti: (0, 0, 0)),
            pl.BlockSpec((1, d), const2),
        ],
        out_specs=[
            pl.BlockSpec((1, tm, d), lambda bi, ti: (bi, ti, 0)),
            pl.BlockSpec((1, CTX_ROWS, d), lambda bi, ti: (bi, 0, 0)),
        ],
        out_shape=[
            jax.ShapeDtypeStruct((b, seq, d), F32),
            jax.ShapeDtypeStruct((b, CTX_ROWS, d), F32),
        ],
        scratch_shapes=[pltpu.VMEM((tm + 2 * CTX_ROWS, d), F32)] * 3,
        compiler_params=pltpu.CompilerParams(
            dimension_semantics=("arbitrary", "arbitrary"), vmem_limit_bytes=VMEM_LIMIT),
        name="mixer0_prompt",
    )(x, x, nmix, pw, pscale)


def _mixer0_sample(x, ctx, nmix, pw, pscale):
    b, steps, d = x.shape
    sb = SAMPLE_SEQ_BLOCK
    const2 = lambda i: (0, 0)
    return pl.pallas_call(
        _mixer0_sample_kernel,
        grid=(b // sb,),
        in_specs=[
            pl.BlockSpec((sb, steps, d), lambda i: (i, 0, 0)),
            pl.BlockSpec((sb, POOL_CTX, d), lambda i: (i, 0, 0)),
            pl.BlockSpec((1, d), const2),
            pl.BlockSpec(pw.shape, lambda i: (0, 0, 0)),
            pl.BlockSpec((1, d), const2),
        ],
        out_specs=[
            pl.BlockSpec((sb, steps, d), lambda i: (i, 0, 0)),
            pl.BlockSpec((sb, POOL_CTX, d), lambda i: (i, 0, 0)),
        ],
        out_shape=[
            jax.ShapeDtypeStruct((b, steps, d), F32),
            jax.ShapeDtypeStruct((b, POOL_CTX, d), F32),
        ],
        scratch_shapes=[pltpu.VMEM((sb, CTX_ROWS + steps, d), F32)],
        compiler_params=pltpu.CompilerParams(
            dimension_semantics=("arbitrary",), vmem_limit_bytes=VMEM_LIMIT),
        name="mixer0_sample",
    )(x, ctx, nmix, pw, pscale)


def _route_rows(glog, elog):
    sub8 = lax.broadcasted_iota(jnp.int32, glog.shape, 0).astype(F32)
    is_grp = sub8 < N_GROUPS
    gmax = jnp.max(jnp.where(is_grp, glog, NEG_BIG), axis=0, keepdims=True)
    gsum = jnp.sum(jnp.where(is_grp, jnp.exp(jnp.minimum(glog - gmax, 0.0)), 0.0),
                   axis=0, keepdims=True)
    g_w = 1.0 / gsum
    g_sel = jnp.min(jnp.where(is_grp & (glog == gmax), sub8, float(N_GROUPS)),
                    axis=0, keepdims=True)
    sub16 = lax.broadcasted_iota(jnp.int32, elog.shape, 0).astype(F32)
    lo = E_PER_GROUP * g_sel
    in_grp = (sub16 >= lo) & (sub16 < lo + E_PER_GROUP)
    none = float(N_EXPERTS)
    v1 = jnp.max(jnp.where(in_grp, elog, NEG_BIG), axis=0, keepdims=True)
    i1 = jnp.min(jnp.where(in_grp & (elog == v1), sub16, none), axis=0, keepdims=True)
    rest = in_grp & (sub16 != i1)
    v2 = jnp.max(jnp.where(rest, elog, NEG_BIG), axis=0, keepdims=True)
    i2 = jnp.min(jnp.where(rest & (elog == v2), sub16, none), axis=0, keepdims=True)
    e2 = jnp.exp(v2 - v1)
    den = 1.0 + e2
    return i1, i2, (1.0 / den) * g_w, (e2 / den) * g_w


def _dispatch(x1_ref, nffn_ref, rw_ref, rb_ref,
              xe_s, perm_s, gsort_s, meta_v, meta_sm, sem):
    tm = x1_ref.shape[0]
    sub = MOE_SUBTILE
    nsub = tm // sub
    hi, lo = _split_bf16(_rms(x1_ref[...], nffn_ref[...]))
    w_hi, w_lo = _split_bf16(rw_ref[...])
    wide = _dot(hi, jnp.concatenate([w_hi, w_lo], axis=1))
    logits = wide[:, :LANES] + wide[:, LANES:] + _dot(lo, w_hi) + rb_ref[...]
    lt = logits.T
    i1, i2, w1, w2 = _route_rows(lt[0:SUBLANES], lt[EXPERT_LANE0:EXPERT_LANE0 + N_EXPERTS])

    sub16 = lax.broadcasted_iota(jnp.int32, (N_EXPERTS, 2 * sub), 0).astype(F32)
    esub = lax.broadcasted_iota(jnp.int32, (N_EXPERTS, LANES), 0)
    mlane = lax.broadcasted_iota(jnp.int32, (N_EXPERTS, LANES), 1)

    def round_up(x, step):
        return jnp.floor((x + (step - 1)) * (1.0 / step)) * step

    def starts(sizes):
        inc = sizes
        for k in (1, 2, 4, 8):
            inc = inc + jnp.where(esub >= k, pltpu.roll(inc, k, 0), 0.0)
        return inc - sizes

    onehots, sizes = [], []
    for s in range(nsub):
        sl = slice(s * sub, (s + 1) * sub)
        sel = jnp.concatenate([i1[:, sl], i2[:, sl]], axis=1)
        at = jnp.where(sub16 == sel, 1.0, 0.0)
        cnt = jnp.sum(at, axis=1, keepdims=True)
        onehots.append(at)
        sizes.append(jnp.broadcast_to(round_up(cnt, SEG_ALIGN), (N_EXPERTS, LANES)))

    region = jnp.maximum(round_up(sum(sizes), EXPERT_STEP), float(EXPERT_FIXED))
    base = starts(region)
    meta = jnp.where(mlane == META_BASE, base, jnp.where(mlane == META_ROWS, region, 0.0))
    blk0 = mlane.astype(F32) * SEG_ALIGN
    seg_starts, at_row = [], base
    for s in range(nsub):
        seg0 = starts(sizes[s])
        seg_starts.append(seg0)
        inside = (blk0 >= seg0) & (blk0 < seg0 + sizes[s])
        dst = jnp.sum(jnp.where(inside, at_row - seg0 + blk0, 0.0), axis=0, keepdims=True)
        used = jnp.sum(jnp.where(inside, 1.0, 0.0), axis=0, keepdims=True)
        dst = jnp.where(used > 0.0, dst, float(XE_DUMP_ROW))
        meta = jnp.where((esub == s) & (mlane < SORT_BLOCKS), dst, meta)
        at_row = at_row + sizes[s]
    meta_v[...] = meta.astype(jnp.int32)
    cp = pltpu.make_async_copy(meta_v, meta_sm, sem)
    cp.start()

    jr = lax.broadcasted_iota(jnp.int32, (2 * sub, 2 * sub), 0)
    jc = lax.broadcasted_iota(jnp.int32, (2 * sub, 2 * sub), 1)
    earlier = jnp.where(jr < jc, 1.0, 0.0).astype(BF16)
    rsub = lax.broadcasted_iota(jnp.int32, (SORT_ROWS, sub), 0).astype(F32)
    for s in range(nsub):
        sl = slice(s * sub, (s + 1) * sub)
        at = onehots[s]
        rank = _dot(at.astype(BF16), earlier)
        dest = jnp.sum(at * (seg_starts[s][:, 0:1] + rank), axis=0, keepdims=True)
        p1 = rsub == dest[:, :sub]
        p2 = rsub == dest[:, sub:]
        perm = jnp.where(p1 | p2, 1.0, 0.0).astype(BF16)
        gsort_s[s] = jnp.sum(jnp.where(p1, w1[:, sl], 0.0) + jnp.where(p2, w2[:, sl], 0.0),
                             axis=1, keepdims=True)
        perm_s[s] = perm
        xs = _dot(perm, hi[sl, :]).astype(BF16)
        if s == 0:
            cp.wait()
        for b in range(SORT_BLOCKS):
            row = pl.multiple_of(meta_sm[s, b], SEG_ALIGN)
            xe_s[pl.ds(row, SEG_ALIGN), :] = xs[b * SEG_ALIGN:(b + 1) * SEG_ALIGN, :]


def _expert_group(first, xe_s, act_s, meta_sm, wg_ref, wu_ref, wd_ref):
    def hidden(j, r0, m):
        x = xe_s[pl.ds(r0, m), :]
        gt = _dot(x, wg_ref[j])
        up = _dot(x, wu_ref[j])
        return ((gt * _sigmoid(gt)) * up).astype(BF16)

    bases = [pl.multiple_of(meta_sm[first + j, META_BASE], EXPERT_STEP)
             for j in range(EXPERTS_PER_STEP)]
    for j in range(EXPERTS_PER_STEP):
        act_s[j] = hidden(j, bases[j], EXPERT_FIXED)
    for j in range(EXPERTS_PER_STEP):
        xe_s[pl.ds(bases[j], EXPERT_FIXED), :] = _dot(act_s[j], wd_ref[j]).astype(BF16)
    for j in range(EXPERTS_PER_STEP):
        def piece(c, carry, j=j):
            r0 = pl.multiple_of(bases[j] + EXPERT_FIXED + c * EXPERT_STEP, EXPERT_STEP)
            xe_s[pl.ds(r0, EXPERT_STEP), :] = _dot(hidden(j, r0, EXPERT_STEP), wd_ref[j]).astype(BF16)
            return carry
        lax.fori_loop(0, (meta_sm[first + j, META_ROWS] - EXPERT_FIXED) // EXPERT_STEP, piece, 0)


def _moe_kernel(with_kv, *refs):
    if with_kv:
        (x1_ref, p_ref, nffn_ref, rw_ref, rb_ref, wg_ref, wu_ref, wd_ref,
         nple_ref, pg_ref, pp_ref, kvn_ref, wkv_ref, kn_ref, rope_ref,
         out_ref, k_ref, v_ref, xe_s, ys_s, act_s, perm_s, gsort_s, meta_v, meta_sm, sem) = refs
    else:
        (x1_ref, p_ref, nffn_ref, rw_ref, rb_ref, wg_ref, wu_ref, wd_ref,
         nple_ref, pg_ref, pp_ref,
         out_ref, xe_s, ys_s, act_s, perm_s, gsort_s, meta_v, meta_sm, sem) = refs
    i = pl.program_id(0)
    step = pl.program_id(1)
    nsub = x1_ref.shape[0] // MOE_SUBTILE

    @pl.when((i == 0) & (step == 0))
    def _():
        xe_s[...] = jnp.zeros_like(xe_s)

    @pl.when(step == 0)
    def _():
        _dispatch(x1_ref, nffn_ref, rw_ref, rb_ref,
                  xe_s, perm_s, gsort_s, meta_v, meta_sm, sem)

    _expert_group(step * EXPERTS_PER_STEP, xe_s, act_s, meta_sm, wg_ref, wu_ref, wd_ref)

    @pl.when(step == pl.num_programs(1) - 1)
    def _():
        for s in range(nsub):
            sl = slice(s * MOE_SUBTILE, (s + 1) * MOE_SUBTILE)
            for b in range(SORT_BLOCKS):
                rows = slice(b * SEG_ALIGN, (b + 1) * SEG_ALIGN)
                src = pl.multiple_of(meta_sm[s, b], SEG_ALIGN)
                blk = xe_s[pl.ds(src, SEG_ALIGN), :].astype(F32) * gsort_s[s, rows, :]
                ys_s[rows, :] = blk.astype(BF16)
            out_ref[sl, :] = x1_ref[sl, :] + _dot_tn(perm_s[s], ys_s[...])
        x2 = out_ref[...]
        hp = _rms(x2, nple_ref[...]).astype(BF16)
        gate = _sigmoid(_dot(hp, pg_ref[...]))
        proj = _dot(p_ref[...].astype(BF16), pp_ref[...])
        x3 = x2 + gate * proj
        out_ref[...] = x3
        if with_kv:
            hk = _rms(x3, kvn_ref[...]).astype(BF16)
            kv = _dot(hk, wkv_ref[...])
            k_ref[...] = _head_norm_rope(kv[:, :LANES], kn_ref[...], _same_head(),
                                         *_rope_parts(rope_ref), two_piece=True)
            v_ref[...] = kv[:, LANES:]


def _moe_layer(x1, p_all, layer, w, kv=None):
    n, d = x1.shape
    tm = TOKEN_TILE
    nt = n // tm
    nsub = tm // MOE_SUBTILE
    tok = lambda i, e: (i, 0)
    const2 = lambda i, e: (0, 0)
    lay3 = lambda i, e: (layer, 0, 0)
    exp4 = lambda i, e: (layer, e, 0, 0)
    in_specs = [
        pl.BlockSpec((tm, d), tok),
        pl.BlockSpec((None, tm, D_PLE), lambda i, e: (layer, i, 0)),
        pl.BlockSpec((None, 1, d), lay3),
        pl.BlockSpec((None, d, LANES), lay3),
        pl.BlockSpec((None, 1, LANES), lay3),
        pl.BlockSpec((None, EXPERTS_PER_STEP, d, D_EXPERT), exp4),
        pl.BlockSpec((None, EXPERTS_PER_STEP, d, D_EXPERT), exp4),
        pl.BlockSpec((None, EXPERTS_PER_STEP, D_EXPERT, d), exp4),
        pl.BlockSpec((None, 1, d), lay3),
        pl.BlockSpec((None, d, d), lay3),
        pl.BlockSpec((None, D_PLE, d), lay3),
    ]
    args = [x1, p_all, w["norm_ffn"], w["router_w"], w["router_b"],
            w["exp_gate"], w["exp_up"], w["exp_down"], w["norm_ple"], w["ple_gate"], w["ple_proj"]]
    out_specs = [pl.BlockSpec((tm, d), tok)]
    out_shape = [jax.ShapeDtypeStruct((n, d), F32)]
    if kv is not None:
        tab_blocks = kv["rope"].shape[0] // tm
        in_specs += [
            pl.BlockSpec((1, d), const2),
            pl.BlockSpec((d, 2 * LANES), const2),
            pl.BlockSpec((1, LANES), const2),
            pl.BlockSpec((tm, 3 * LANES), lambda i, e: (i % tab_blocks, 0)),
        ]
        args += [kv["kv_norm"], kv["w_kv"], kv["k_norm"], kv["rope"]]
        out_specs += [pl.BlockSpec((tm, LANES), tok), pl.BlockSpec((tm, LANES), tok)]
        out_shape += [jax.ShapeDtypeStruct((n, LANES), F32), jax.ShapeDtypeStruct((n, LANES), F32)]
    return pl.pallas_call(
        functools.partial(_moe_kernel, kv is not None),
        grid=(nt, N_EXPERTS // EXPERTS_PER_STEP),
        in_specs=in_specs,
        out_specs=out_specs,
        out_shape=out_shape,
        scratch_shapes=[
            pltpu.VMEM((XE_DUMP_ROW + SEG_ALIGN, d), BF16),
            pltpu.VMEM((SORT_ROWS, d), BF16),
            pltpu.VMEM((EXPERTS_PER_STEP, EXPERT_FIXED, D_EXPERT), BF16),
            pltpu.VMEM((nsub, SORT_ROWS, MOE_SUBTILE), BF16),
            pltpu.VMEM((nsub, SORT_ROWS, 1), F32),
            pltpu.VMEM((N_EXPERTS, LANES), jnp.int32),
            pltpu.SMEM((N_EXPERTS, LANES), jnp.int32),
            pltpu.SemaphoreType.DMA,
        ],
        compiler_params=pltpu.CompilerParams(
            dimension_semantics=("arbitrary", "arbitrary"), vmem_limit_bytes=VMEM_LIMIT),
        name="moe_kv" if kv is not None else "moe",
    )(*args)


def _project_q(x3, nmix_ref, wq_ref, qn_ref, cos, sin_next, sin_prev, q_s):
    h = _rms(x3, nmix_ref[...]).astype(BF16)
    same_head = _same_head()
    wide = 4 * LANES
    for blk in range(D_MODEL // wide):
        q = _dot(h, wq_ref[:, blk * wide:(blk + 1) * wide])
        for c in range(wide // LANES):
            cols = slice(blk * wide + c * LANES, blk * wide + (c + 1) * LANES)
            qc = _head_norm_rope(q[:, c * LANES:(c + 1) * LANES], qn_ref[...], same_head,
                                 cos, sin_next, sin_prev, two_piece=False)
            q_s[:, cols] = (qc * (HEAD_DIM ** -0.5 * LOG2_E)).astype(q_s.dtype)


def _both_orders(x):
    return x.astype(BF16), pltpu.roll(x, HEAD_DIM, 1).astype(BF16)


def _kv_order(par, g):
    return 0 if par == g else 1


def _softmax_sink(s, mask, sink):
    sink = sink * LOG2_E
    s = jnp.where(mask, s, NEG_BIG)
    m = jnp.maximum(jnp.max(s, axis=-1, keepdims=True), sink)
    p = jnp.exp2(s - m)
    den = jnp.sum(p, axis=-1, keepdims=True) + jnp.exp2(sink - m)
    return p.astype(BF16), 1.0 / den


def _attn_prompt_kernel(x3_ref, k_ref, v_ref, kprev_ref, vprev_ref, nmix_ref, wq_ref, qn_ref,
                        rope_ref, sinks_ref, wo_ref,
                        x4_ref, q_s, o_s, kd_s, vd_s):
    t = pl.program_id(1)
    tq = x3_ref.shape[1]
    x3 = x3_ref[0]
    _project_q(x3, nmix_ref, wq_ref, qn_ref, *_rope_parts(rope_ref), q_s)

    for src_prev, src, dst in ((kprev_ref, k_ref, kd_s), (vprev_ref, v_ref, vd_s)):
        dst[0, 0:WINDOW, :], dst[1, 0:WINDOW, :] = _both_orders(src_prev[0])
        dst[0, WINDOW:, :], dst[1, WINDOW:, :] = _both_orders(src[0])

    qi = lax.broadcasted_iota(jnp.int32, (WINDOW, 2 * WINDOW), 0)
    ki = lax.broadcasted_iota(jnp.int32, (WINDOW, 2 * WINDOW), 1)
    band = (ki > qi) & (ki <= qi + WINDOW)
    lane = lax.broadcasted_iota(jnp.int32, (WINDOW, LANES), 1)
    first = lane < HEAD_DIM

    def q_block(j, carry):
        r0 = pl.multiple_of(j * WINDOW, WINDOW)
        mask = band & (ki >= (1 - j) * WINDOW - t * tq)
        for c in range(D_MODEL // LANES):
            g = (2 * c) // Q_PER_KV
            cols = slice(c * LANES, (c + 1) * LANES)
            qc = q_s[pl.ds(r0, WINDOW), cols]
            outs = []
            for par in range(2):
                order = _kv_order(par, g)
                qh = jnp.where(first, qc, 0) if par == 0 else jnp.where(first, 0, qc)
                s = _dot_nt(qh, kd_s[order, pl.ds(r0, 2 * WINDOW), :])
                pr, inv = _softmax_sink(s, mask, sinks_ref[2 * c + par])
                outs.append(_dot(pr, vd_s[order, pl.ds(r0, 2 * WINDOW), :]) * inv)
            o_s[pl.ds(r0, WINDOW), cols] = jnp.where(first, outs[0], outs[1]).astype(BF16)
        return carry

    lax.fori_loop(0, tq // WINDOW, q_block, 0)
    x4_ref[0] = x3 + _dot(o_s[...], wo_ref[...])


def _attn_prompt(x3, k, v, w, rope):
    b, seq, d = x3.shape
    tq = ATTN_TILE
    nt = seq // tq
    per = tq // WINDOW
    tile = lambda bi, ti: (bi, ti, 0)
    prev = lambda bi, ti: (bi, jnp.maximum(ti * per - 1, 0), 0)
    const2 = lambda bi, ti: (0, 0)
    tab = lambda bi, ti: (ti, 0)
    return pl.pallas_call(
        _attn_prompt_kernel,
        grid=(b, nt),
        in_specs=[
            pl.BlockSpec((1, tq, d), tile),
            pl.BlockSpec((1, tq, LANES), tile),
            pl.BlockSpec((1, tq, LANES), tile),
            pl.BlockSpec((1, WINDOW, LANES), prev),
            pl.BlockSpec((1, WINDOW, LANES), prev),
            pl.BlockSpec((1, d), const2),
            pl.BlockSpec((d, d), const2),
            pl.BlockSpec((1, LANES), const2),
            pl.BlockSpec((tq, 3 * LANES), tab),
            pl.BlockSpec(memory_space=pltpu.SMEM),
            pl.BlockSpec((d, d), const2),
        ],
        out_specs=pl.BlockSpec((1, tq, d), tile),
        out_shape=jax.ShapeDtypeStruct((b, seq, d), F32),
        scratch_shapes=[
            pltpu.VMEM((tq, d), BF16),
            pltpu.VMEM((tq, d), BF16),
            pltpu.VMEM((N_KV_HEADS, tq + WINDOW, LANES), BF16),
            pltpu.VMEM((N_KV_HEADS, tq + WINDOW, LANES), BF16),
        ],
        compiler_params=pltpu.CompilerParams(
            dimension_semantics=("arbitrary", "arbitrary"), vmem_limit_bytes=VMEM_LIMIT),
        name="attn_prompt",
    )(x3, k, v, k, v, w["norm_mix1"], w["w_q"], w["q_norm"], rope, w["sinks"], w["w_o"])


def _attn_sample_kernel(x3_ref, knew_ref, vnew_ref, kc_ref, vc_ref, nmix_ref, wq_ref, qn_ref,
                        rope_ref, sinks_ref, wo_ref,
                        x4_ref, q_s, o_s):
    rows = x3_ref.shape[0]
    steps = knew_ref.shape[0] // kc_ref.shape[0]
    x3 = x3_ref[...]
    _project_q(x3, nmix_ref, wq_ref, qn_ref, *_rope_parts(rope_ref), q_s)

    n_chunks = D_MODEL // LANES
    stack = n_chunks * steps
    ri = lax.broadcasted_iota(jnp.int32, (stack, 2 * WINDOW), 0)
    ki = lax.broadcasted_iota(jnp.int32, (stack, 2 * WINDOW), 1)
    tq = ri % steps
    mask = ((ki < WINDOW) & (ki > tq)) | ((ki >= WINDOW) & ((ki - WINDOW) <= tq))
    def par_of(order, c):
        g = (2 * c) // Q_PER_KV
        return g if order == 0 else 1 - g

    chunk_of_row = lax.broadcasted_iota(jnp.int32, (stack, 1), 0) // steps
    sink_cols = []
    for order in range(2):
        col = jnp.zeros((stack, 1), F32)
        for c in range(n_chunks):
            col = jnp.where(chunk_of_row == c, sinks_ref[2 * c + par_of(order, c)], col)
        sink_cols.append(col)
    lane = lax.broadcasted_iota(jnp.int32, (steps, LANES), 1)
    first = lane < HEAD_DIM
    pad = jnp.zeros((WINDOW - steps, LANES), F32)

    def one_seq(b):
        r0 = pl.multiple_of(b * steps, steps)
        keys = _both_orders(jnp.concatenate([kc_ref[b], knew_ref[pl.ds(r0, steps), :], pad], axis=0))
        vals = _both_orders(jnp.concatenate([vc_ref[b], vnew_ref[pl.ds(r0, steps), :], pad], axis=0))
        qcs = [q_s[pl.ds(r0, steps), c * LANES:(c + 1) * LANES] for c in range(n_chunks)]
        outs = []
        for order in range(2):
            slabs = [jnp.where(first, qc, 0.0) if par_of(order, c) == 0 else jnp.where(first, 0.0, qc)
                     for c, qc in enumerate(qcs)]
            s = _dot_nt(jnp.concatenate(slabs, axis=0).astype(BF16), keys[order])
            pr, inv = _softmax_sink(s, mask, sink_cols[order])
            outs.append(_dot(pr, vals[order]) * inv)
        for c in range(n_chunks):
            lo_half = outs[0] if par_of(0, c) == 0 else outs[1]
            hi_half = outs[1] if par_of(0, c) == 0 else outs[0]
            o_s[pl.ds(r0, steps), c * LANES:(c + 1) * LANES] = jnp.where(
                first, lo_half[c * steps:(c + 1) * steps], hi_half[c * steps:(c + 1) * steps])

    def seq_pair(i, carry):
        for u in range(SAMPLE_SEQ_UNROLL):
            one_seq(i * SAMPLE_SEQ_UNROLL + u)
        return carry

    lax.fori_loop(0, rows // steps // SAMPLE_SEQ_UNROLL, seq_pair, 0)
    x4_ref[...] = x3 + _dot(o_s[...].astype(BF16), wo_ref[...])


def _attn_sample(x3, knew, vnew, kcache, vcache, w, rope, steps):
    n, d = x3.shape
    sb = SAMPLE_SEQ_BLOCK
    rows = sb * steps
    tok = lambda i: (i, 0)
    const2 = lambda i: (0, 0)
    return pl.pallas_call(
        _attn_sample_kernel,
        grid=(n // rows,),
        in_specs=[
            pl.BlockSpec((rows, d), tok),
            pl.BlockSpec((rows, LANES), tok),
            pl.BlockSpec((rows, LANES), tok),
            pl.BlockSpec((sb, WINDOW, LANES), lambda i: (i, 0, 0)),
            pl.BlockSpec((sb, WINDOW, LANES), lambda i: (i, 0, 0)),
            pl.BlockSpec((1, d), const2),
            pl.BlockSpec((d, d), const2),
            pl.BlockSpec((1, LANES), const2),
            pl.BlockSpec((rows, 3 * LANES), tok),
            pl.BlockSpec(memory_space=pltpu.SMEM),
            pl.BlockSpec((d, d), const2),
        ],
        out_specs=pl.BlockSpec((rows, d), tok),
        out_shape=jax.ShapeDtypeStruct((n, d), F32),
        scratch_shapes=[
            pltpu.VMEM((rows, d), F32),
            pltpu.VMEM((rows, d), F32),
        ],
        compiler_params=pltpu.CompilerParams(
            dimension_semantics=("arbitrary",), vmem_limit_bytes=VMEM_LIMIT),
        name="attn_sample",
    )(x3, knew, vnew, kcache, vcache, w["norm_mix1"], w["w_q"], w["q_norm"], rope,
      w["sinks"], w["w_o"])


def _rope_table(pos):
    half = ROPE_DIM // 2
    dim = jnp.arange(LANES, dtype=jnp.int32) % HEAD_DIM
    inv = jnp.float32(ROPE_THETA) ** (-((dim % half).astype(jnp.float32) * 2.0 / ROPE_DIM))
    inv = jnp.where(dim < ROPE_DIM, inv, 0.0)
    ang = pos.astype(jnp.float32)[:, None] * inv[None, :]
    cos, sin = jnp.cos(ang), jnp.sin(ang)
    s_next = jnp.where(dim < half, -sin, 0.0)
    s_prev = jnp.where((dim >= half) & (dim < ROPE_DIM), sin, 0.0)
    return jnp.concatenate([cos, s_next, s_prev], axis=1)


def kernel(x_prompt, x_sample, p_prompt, p_sample, state_pool, cache_k_win, cache_v_win, norm_mix, norm_ffn, norm_ple, pool_w, pool_scale, kv_norm, w_kv, k_norm, w_q, q_norm, sinks, w_o, router_g_w, router_g_b, router_e_w, router_e_b, exp_gate, exp_up, exp_down, ple_gate, ple_proj):
    b, seq, d = x_prompt.shape
    sb, steps, _ = x_sample.shape
    depth = norm_mix.shape[0]
    row = lambda v: v.reshape(1, -1)

    gap_w = jnp.zeros((depth, d, EXPERT_LANE0 - N_GROUPS), F32)
    tail_w = jnp.zeros((depth, d, LANES - EXPERT_LANE0 - N_EXPERTS), F32)
    router_w = jnp.concatenate(
        [router_g_w, gap_w, router_e_w.reshape(depth, d, N_EXPERTS), tail_w], axis=2)
    router_b = jnp.concatenate(
        [router_g_b, gap_w[:, 0], router_e_b.reshape(depth, N_EXPERTS), tail_w[:, 0]], axis=1)
    moe_w = {
        "norm_ffn": norm_ffn.reshape(depth, 1, d), "router_w": router_w,
        "router_b": router_b.reshape(depth, 1, LANES),
        "exp_gate": exp_gate.astype(BF16), "exp_up": exp_up.astype(BF16),
        "exp_down": exp_down.astype(BF16), "norm_ple": norm_ple.reshape(depth, 1, d),
        "ple_gate": ple_gate.astype(BF16), "ple_proj": ple_proj.astype(BF16),
    }
    two = lambda a: jnp.concatenate([a, a])
    attn_w = {
        "norm_mix1": row(norm_mix[1]), "w_q": w_q[0].astype(BF16), "q_norm": row(two(q_norm[0])),
        "sinks": sinks[0], "w_o": w_o[0].astype(BF16),
    }
    pw = pool_w[0]
    nmix0 = row(norm_mix[0])
    pscale = row(pool_scale[0])

    rope_p = _rope_table(jnp.arange(seq, dtype=jnp.int32))
    rope_s = jnp.tile(_rope_table(PAST_LEN + jnp.arange(steps, dtype=jnp.int32)), (sb, 1))
    kv_w = {"kv_norm": row(kv_norm), "w_kv": w_kv.astype(BF16), "k_norm": row(two(k_norm))}

    pp_all = p_prompt.reshape(depth, b * seq, D_PLE)
    x1p, pool16 = _mixer0_prompt(x_prompt, nmix0, pw, pscale)
    x3p, kp, vp = _moe_layer(x1p.reshape(b * seq, d), pp_all, 0, moe_w, dict(kv_w, rope=rope_p))
    kp = kp.reshape(b, seq, LANES)
    vp = vp.reshape(b, seq, LANES)
    x4p = _attn_prompt(x3p.reshape(b, seq, d), kp, vp, attn_w, rope_p)
    y_prompt = _moe_layer(x4p.reshape(b * seq, d), pp_all, 1, moe_w)[0].reshape(b, seq, d)
    pool_prompt = pool16[None, :, CTX_ROWS - POOL_CTX:, :]
    k_win_prompt = kp[:, seq - WINDOW:].reshape(b, WINDOW, N_KV_HEADS, HEAD_DIM)
    v_win_prompt = vp[:, seq - WINDOW:].reshape(b, WINDOW, N_KV_HEADS, HEAD_DIM)

    ps_all = p_sample.reshape(depth, sb * steps, D_PLE)
    x1s, pool_s = _mixer0_sample(x_sample, state_pool[0], nmix0, pw, pscale)
    x3s, ks, vs = _moe_layer(x1s.reshape(sb * steps, d), ps_all, 0, moe_w, dict(kv_w, rope=rope_s))
    kc = cache_k_win.reshape(sb, WINDOW, LANES)
    vc = cache_v_win.reshape(sb, WINDOW, LANES)
    x4s = _attn_sample(x3s, ks, vs, kc, vc, attn_w, rope_s, steps)
    y_sample = _moe_layer(x4s, ps_all, 1, moe_w)[0].reshape(sb, steps, d)
    pool_sample = pool_s[None]
    heads = (sb, steps, N_KV_HEADS, HEAD_DIM)
    k_win_sample = jnp.concatenate([cache_k_win[:, steps:], ks.reshape(heads)], axis=1)
    v_win_sample = jnp.concatenate([cache_v_win[:, steps:], vs.reshape(heads)], axis=1)

    return (y_prompt, y_sample, pool_prompt, pool_sample,
            k_win_prompt, v_win_prompt, k_win_sample, v_win_sample)
```

```python
import functools

import jax
import jax.numpy as jnp
from jax import lax
from jax.experimental import pallas as pl
from jax.experimental.pallas import tpu as pltpu

D_MODEL = 1024
PAST_LEN = 8192
POOL_WINDOWS = (2, 4, 8, 16)
POOL_GROUP_DIM = D_MODEL // len(POOL_WINDOWS)
POOL_CTX = max(POOL_WINDOWS) - 1
assert all(win == 2 ** (g + 1) for g, win in enumerate(POOL_WINDOWS))
HEAD_DIM = 64
N_HEADS = D_MODEL // HEAD_DIM
N_KV_HEADS = 2
Q_PER_KV = N_HEADS // N_KV_HEADS
WINDOW = 128
ROPE_DIM = HEAD_DIM // 4
ROPE_THETA = 500000.0
N_GROUPS = 4
E_PER_GROUP = 4
N_EXPERTS = N_GROUPS * E_PER_GROUP
D_EXPERT = 256
D_PLE = 256
EPS = 1e-6

LANES = 128
SUBLANES = 8
CTX_ROWS = 16
TOKEN_TILE = 1024
ATTN_TILE = 512
SAMPLE_SEQ_BLOCK = 16
SAMPLE_SEQ_UNROLL = 8
VMEM_LIMIT = 60 * 1024 * 1024
NEG_BIG = -1e30
LOG2_E = 1.4426950408889634
F32 = jnp.float32
BF16 = jnp.bfloat16

EXPERT_LANE0 = SUBLANES
MOE_SUBTILE = 256
SEG_ALIGN = 2 * SUBLANES
SORT_ROWS = 768
EXPERT_FIXED = 192
EXPERT_STEP = 32
EXPERTS_PER_STEP = 4
SORT_BLOCKS = SORT_ROWS // SEG_ALIGN
MOE_SUBTILES = TOKEN_TILE // MOE_SUBTILE
XE_DUMP_ROW = (2 * TOKEN_TILE + MOE_SUBTILES * N_EXPERTS * (SEG_ALIGN - 1)
               + N_EXPERTS * (EXPERT_STEP - SEG_ALIGN + EXPERT_FIXED))
META_BASE, META_ROWS = 64, 65
assert SORT_ROWS >= 2 * MOE_SUBTILE + N_EXPERTS * (SEG_ALIGN - 1) and SORT_ROWS % LANES == 0
assert TOKEN_TILE % MOE_SUBTILE == 0 and MOE_SUBTILES <= N_EXPERTS and SORT_BLOCKS <= META_BASE
assert EXPERT_FIXED % EXPERT_STEP == 0 and EXPERT_STEP % SEG_ALIGN == 0
assert N_EXPERTS % EXPERTS_PER_STEP == 0 and XE_DUMP_ROW % SEG_ALIGN == 0


def _rms(x, g):
    return x * lax.rsqrt(jnp.mean(x * x, axis=-1, keepdims=True) + EPS) * g


def _dot(a, b):
    return jnp.dot(a, b, preferred_element_type=F32)


def _dot_nt(a, b):
    return lax.dot_general(a, b, (((1,), (1,)), ((), ())), preferred_element_type=F32)


def _dot_tn(a, b):
    return lax.dot_general(a, b, (((0,), (0,)), ((), ())), preferred_element_type=F32)


def _split_bf16(x):
    hi = x.astype(BF16)
    return hi, (x - hi.astype(F32)).astype(BF16)


def _dot3(a_hi, a_lo, b):
    b_hi, b_lo = _split_bf16(b)
    return _dot(a_hi, b_hi) + _dot(a_lo, b_hi) + _dot(a_hi, b_lo)


def _sigmoid(x):
    return 1.0 / (1.0 + jnp.exp(-x))


def _rope_parts(rope_ref):
    return rope_ref[:, 0:LANES], rope_ref[:, LANES:2 * LANES], rope_ref[:, 2 * LANES:3 * LANES]


def _same_head():
    hi = lax.broadcasted_iota(jnp.int32, (LANES, LANES), 0) // HEAD_DIM
    hj = lax.broadcasted_iota(jnp.int32, (LANES, LANES), 1) // HEAD_DIM
    return jnp.where(hi == hj, 1.0, 0.0).astype(BF16)


def _head_norm_rope(x, gain, same_head, cos, sin_next, sin_prev, two_piece):
    sq_hi, sq_lo = _split_bf16(x * x)
    ssq = _dot(sq_hi, same_head)
    if two_piece:
        ssq = ssq + _dot(sq_lo, same_head)
    y = x * lax.rsqrt(ssq * (1.0 / HEAD_DIM) + EPS) * gain
    half = ROPE_DIM // 2
    return (y * cos + pltpu.roll(y, LANES - half, 1) * sin_next
            + pltpu.roll(y, half, 1) * sin_prev)


def _pool_mix(h, window_sum, pos, x, pw_ref, pscale_ref, store):
    for g, win in enumerate(POOL_WINDOWS):
        cols = slice(g * POOL_GROUP_DIM, (g + 1) * POOL_GROUP_DIM)
        hg = h[..., cols]
        inv_cnt = 1.0 / jnp.minimum(win, pos + 1).astype(F32)
        pooled = (window_sum(g, cols) * inv_cnt - hg).reshape(-1, POOL_GROUP_DIM)
        hi, lo = _split_bf16(pooled)
        mixed = _dot3(hi, lo, pw_ref[g])
        store(cols, x[..., cols] + mixed.reshape(hg.shape) * pscale_ref[:, cols])


def _mixer0_prompt_kernel(x_ref, xprev_ref, nmix_ref, pw_ref, pscale_ref,
                          x1_ref, pool_ref, hs_ref, ua_ref, ub_ref):
    t = pl.program_id(1)
    tm = x_ref.shape[1]
    x = x_ref[0]
    h = _rms(x, nmix_ref[...])
    hp = _rms(xprev_ref[0], nmix_ref[...])
    top = 2 * CTX_ROWS
    hs_ref[0:CTX_ROWS, :] = jnp.zeros((CTX_ROWS, x.shape[1]), F32)
    hs_ref[CTX_ROWS:top, :] = jnp.where(t > 0, hp, 0.0)
    hs_ref[top:, :] = h
    ua_ref[0:SUBLANES, :] = jnp.zeros((SUBLANES, x.shape[1]), F32)
    ub_ref[0:SUBLANES, :] = jnp.zeros((SUBLANES, x.shape[1]), F32)

    n = tm + top - SUBLANES
    src, level_of = hs_ref, []
    for g, win in enumerate(POOL_WINDOWS):
        dst = ua_ref if g % 2 == 0 else ub_ref
        c0 = g * POOL_GROUP_DIM
        back = win // 2
        dst[SUBLANES:, c0:] = src[SUBLANES:, c0:] + src[SUBLANES - back:SUBLANES - back + n, c0:]
        level_of.append(dst)
        src = dst

    def window_sum(g, cols):
        return level_of[g][top:, cols]

    def store(cols, val):
        x1_ref[0, :, cols] = val

    pos = t * tm + lax.broadcasted_iota(jnp.int32, (tm, 1), 0)
    _pool_mix(h, window_sum, pos, x, pw_ref, pscale_ref, store)

    @pl.when(t == pl.num_programs(1) - 1)
    def _():
        pool_ref[0] = hs_ref[tm + top - CTX_ROWS:tm + top, :]


def _mixer0_sample_kernel(x_ref, ctx_ref, nmix_ref, pw_ref, pscale_ref,
                          x1_ref, pool_ref, hs_ref):
    x = x_ref[...]
    steps = x.shape[1]
    h = _rms(x, nmix_ref[...])
    hs_ref[:, CTX_ROWS - POOL_CTX:CTX_ROWS, :] = ctx_ref[...]
    hs_ref[:, CTX_ROWS:, :] = h
    pool_ref[:, 0:POOL_CTX - steps, :] = ctx_ref[:, steps:, :]
    pool_ref[:, POOL_CTX - steps:, :] = h

    def window_sum(g, cols):
        s = h[..., cols]
        for j in range(1, POOL_WINDOWS[g]):
            s = s + hs_ref[:, CTX_ROWS - j:CTX_ROWS - j + steps, cols]
        return s

    def store(cols, val):
        x1_ref[:, :, cols] = val

    pos = PAST_LEN + lax.broadcasted_iota(jnp.int32, (1, steps, 1), 1)
    _pool_mix(h, window_sum, pos, x, pw_ref, pscale_ref, store)


def _mixer0_prompt(x, nmix, pw, pscale):
    b, seq, d = x.shape
    tm = TOKEN_TILE
    nt = seq // tm
    per = tm // CTX_ROWS
    const2 = lambda bi, ti: (0, 0)
    return pl.pallas_call(
        _mixer0_prompt_kernel,
        grid=(b, nt),
        in_specs=[
            pl.BlockSpec((1, tm, d), lambda bi, ti: (bi, ti, 0)),
            pl.BlockSpec((1, CTX_ROWS, d), lambda bi, ti: (bi, jnp.maximum(ti * per - 1, 0), 0)),
            pl.BlockSpec((1, d), const2),
            pl.BlockSpec(pw.shape, lambda bi, ti: (0, 0, 0)),
            pl.BlockSpec((1, d), const2),
        ],
        out_specs=[
            pl.BlockSpec((1, tm, d), lambda bi, ti: (bi, ti, 0)),
            pl.BlockSpec((1, CTX_ROWS, d), lambda bi, ti: (bi, 0, 0)),
        ],
        out_shape=[
            jax.ShapeDtypeStruct((b, seq, d), F32),
            jax.ShapeDtypeStruct((b, CTX_ROWS, d), F32),
        ],
        scratch_shapes=[pltpu.VMEM((tm + 2 * CTX_ROWS, d), F32)] * 3,
        compiler_params=pltpu.CompilerParams(
            dimension_semantics=("arbitrary", "arbitrary"), vmem_limit_bytes=VMEM_LIMIT),
        name="mixer0_prompt",
    )(x, x, nmix, pw, pscale)


def _mixer0_sample(x, ctx, nmix, pw, pscale):
    b, steps, d = x.shape
    sb = SAMPLE_SEQ_BLOCK
    const2 = lambda i: (0, 0)
    return pl.pallas_call(
        _mixer0_sample_kernel,
        grid=(b // sb,),
        in_specs=[
            pl.BlockSpec((sb, steps, d), lambda i: (i, 0, 0)),
            pl.BlockSpec((sb, POOL_CTX, d), lambda i: (i, 0, 0)),
            pl.BlockSpec((1, d), const2),
            pl.BlockSpec(pw.shape, lambda i: (0, 0, 0)),
            pl.BlockSpec((1, d), const2),
        ],
        out_specs=[
            pl.BlockSpec((sb, steps, d), lambda i: (i, 0, 0)),
            pl.BlockSpec((sb, POOL_CTX, d), lambda i: (i, 0, 0)),
        ],
        out_shape=[
            jax.ShapeDtypeStruct((b, steps, d), F32),
            jax.ShapeDtypeStruct((b, POOL_CTX, d), F32),
        ],
        scratch_shapes=[pltpu.VMEM((sb, CTX_ROWS + steps, d), F32)],
        compiler_params=pltpu.CompilerParams(
            dimension_semantics=("arbitrary",), vmem_limit_bytes=VMEM_LIMIT),
        name="mixer0_sample",
    )(x, ctx, nmix, pw, pscale)


def _route_rows(glog, elog):
    sub8 = lax.broadcasted_iota(jnp.int32, glog.shape, 0).astype(F32)
    is_grp = sub8 < N_GROUPS
    gmax = jnp.max(jnp.where(is_grp, glog, NEG_BIG), axis=0, keepdims=True)
    gsum = jnp.sum(jnp.where(is_grp, jnp.exp(jnp.minimum(glog - gmax, 0.0)), 0.0),
                   axis=0, keepdims=True)
    g_w = 1.0 / gsum
    g_sel = jnp.min(jnp.where(is_grp & (glog == gmax), sub8, float(N_GROUPS)),
                    axis=0, keepdims=True)
    sub16 = lax.broadcasted_iota(jnp.int32, elog.shape, 0).astype(F32)
    lo = E_PER_GROUP * g_sel
    in_grp = (sub16 >= lo) & (sub16 < lo + E_PER_GROUP)
    none = float(N_EXPERTS)
    v1 = jnp.max(jnp.where(in_grp, elog, NEG_BIG), axis=0, keepdims=True)
    i1 = jnp.min(jnp.where(in_grp & (elog == v1), sub16, none), axis=0, keepdims=True)
    rest = in_grp & (sub16 != i1)
    v2 = jnp.max(jnp.where(rest, elog, NEG_BIG), axis=0, keepdims=True)
    i2 = jnp.min(jnp.where(rest & (elog == v2), sub16, none), axis=0, keepdims=True)
    e2 = jnp.exp(v2 - v1)
    den = 1.0 + e2
    return i1, i2, (1.0 / den) * g_w, (e2 / den) * g_w


def _dispatch(x1_ref, nffn_ref, rw_ref, rb_ref,
              xe_s, perm_s, gsort_s, meta_v, meta_sm, sem):
    tm = x1_ref.shape[0]
    sub = MOE_SUBTILE
    nsub = tm // sub
    hi, lo = _split_bf16(_rms(x1_ref[...], nffn_ref[...]))
    w_hi, w_lo = _split_bf16(rw_ref[...])
    wide = _dot(hi, jnp.concatenate([w_hi, w_lo], axis=1))
    logits = wide[:, :LANES] + wide[:, LANES:] + _dot(lo, w_hi) + rb_ref[...]
    lt = logits.T
    i1, i2, w1, w2 = _route_rows(lt[0:SUBLANES], lt[EXPERT_LANE0:EXPERT_LANE0 + N_EXPERTS])

    sub16 = lax.broadcasted_iota(jnp.int32, (N_EXPERTS, 2 * sub), 0).astype(F32)
    esub = lax.broadcasted_iota(jnp.int32, (N_EXPERTS, LANES), 0)
    mlane = lax.broadcasted_iota(jnp.int32, (N_EXPERTS, LANES), 1)

    def round_up(x, step):
        return jnp.floor((x + (step - 1)) * (1.0 / step)) * step

    def starts(sizes):
        inc = sizes
        for k in (1, 2, 4, 8):
            inc = inc + jnp.where(esub >= k, pltpu.roll(inc, k, 0), 0.0)
        return inc - sizes

    onehots, sizes = [], []
    for s in range(nsub):
        sl = slice(s * sub, (s + 1) * sub)
        sel = jnp.concatenate([i1[:, sl], i2[:, sl]], axis=1)
        at = jnp.where(sub16 == sel, 1.0, 0.0)
        cnt = jnp.sum(at, axis=1, keepdims=True)
        onehots.append(at)
        sizes.append(jnp.broadcast_to(round_up(cnt, SEG_ALIGN), (N_EXPERTS, LANES)))

    region = jnp.maximum(round_up(sum(sizes), EXPERT_STEP), float(EXPERT_FIXED))
    base = starts(region)
    meta = jnp.where(mlane == META_BASE, base, jnp.where(mlane == META_ROWS, region, 0.0))
    blk0 = mlane.astype(F32) * SEG_ALIGN
    seg_starts, at_row = [], base
    for s in range(nsub):
        seg0 = starts(sizes[s])
        seg_starts.append(seg0)
        inside = (blk0 >= seg0) & (blk0 < seg0 + sizes[s])
        dst = jnp.sum(jnp.where(inside, at_row - seg0 + blk0, 0.0), axis=0, keepdims=True)
        used = jnp.sum(jnp.where(inside, 1.0, 0.0), axis=0, keepdims=True)
        dst = jnp.where(used > 0.0, dst, float(XE_DUMP_ROW))
        meta = jnp.where((esub == s) & (mlane < SORT_BLOCKS), dst, meta)
        at_row = at_row + sizes[s]
    meta_v[...] = meta.astype(jnp.int32)
    cp = pltpu.make_async_copy(meta_v, meta_sm, sem)
    cp.start()

    jr = lax.broadcasted_iota(jnp.int32, (2 * sub, 2 * sub), 0)
    jc = lax.broadcasted_iota(jnp.int32, (2 * sub, 2 * sub), 1)
    earlier = jnp.where(jr < jc, 1.0, 0.0).astype(BF16)
    rsub = lax.broadcasted_iota(jnp.int32, (SORT_ROWS, sub), 0).astype(F32)
    for s in range(nsub):
        sl = slice(s * sub, (s + 1) * sub)
        at = onehots[s]
        rank = _dot(at.astype(BF16), earlier)
        dest = jnp.sum(at * (seg_starts[s][:, 0:1] + rank), axis=0, keepdims=True)
        p1 = rsub == dest[:, :sub]
        p2 = rsub == dest[:, sub:]
        perm = jnp.where(p1 | p2, 1.0, 0.0).astype(BF16)
        gsort_s[s] = jnp.sum(jnp.where(p1, w1[:, sl], 0.0) + jnp.where(p2, w2[:, sl], 0.0),
                             axis=1, keepdims=True)
        perm_s[s] = perm
        xs = _dot(perm, hi[sl, :]).astype(BF16)
        if s == 0:
            cp.wait()
        for b in range(SORT_BLOCKS):
            row = pl.multiple_of(meta_sm[s, b], SEG_ALIGN)
            xe_s[pl.ds(row, SEG_ALIGN), :] = xs[b * SEG_ALIGN:(b + 1) * SEG_ALIGN, :]


def _expert_group(first, xe_s, act_s, meta_sm, wg_ref, wu_ref, wd_ref):
    def hidden(j, r0, m):
        x = xe_s[pl.ds(r0, m), :]
        gt = _dot(x, wg_ref[j])
        up = _dot(x, wu_ref[j])
        return ((gt * _sigmoid(gt)) * up).astype(BF16)

    bases = [pl.multiple_of(meta_sm[first + j, META_BASE], EXPERT_STEP)
             for j in range(EXPERTS_PER_STEP)]
    for j in range(EXPERTS_PER_STEP):
        act_s[j] = hidden(j, bases[j], EXPERT_FIXED)
    for j in range(EXPERTS_PER_STEP):
        xe_s[pl.ds(bases[j], EXPERT_FIXED), :] = _dot(act_s[j], wd_ref[j]).astype(BF16)
    for j in range(EXPERTS_PER_STEP):
        def piece(c, carry, j=j):
            r0 = pl.multiple_of(bases[j] + EXPERT_FIXED + c * EXPERT_STEP, EXPERT_STEP)
            xe_s[pl.ds(r0, EXPERT_STEP), :] = _dot(hidden(j, r0, EXPERT_STEP), wd_ref[j]).astype(BF16)
            return carry
        lax.fori_loop(0, (meta_sm[first + j, META_ROWS] - EXPERT_FIXED) // EXPERT_STEP, piece, 0)


def _moe_kernel(with_kv, *refs):
    if with_kv:
        (x1_ref, p_ref, nffn_ref, rw_ref, rb_ref, wg_ref, wu_ref, wd_ref,
         nple_ref, pg_ref, pp_ref, kvn_ref, wkv_ref, kn_ref, rope_ref,
         out_ref, k_ref, v_ref, xe_s, ys_s, act_s, perm_s, gsort_s, meta_v, meta_sm, sem) = refs
    else:
        (x1_ref, p_ref, nffn_ref, rw_ref, rb_ref, wg_ref, wu_ref, wd_ref,
         nple_ref, pg_ref, pp_ref,
         out_ref, xe_s, ys_s, act_s, perm_s, gsort_s, meta_v, meta_sm, sem) = refs
    i = pl.program_id(0)
    step = pl.program_id(1)
    nsub = x1_ref.shape[0] // MOE_SUBTILE

    @pl.when((i == 0) & (step == 0))
    def _():
        xe_s[...] = jnp.zeros_like(xe_s)

    @pl.when(step == 0)
    def _():
        _dispatch(x1_ref, nffn_ref, rw_ref, rb_ref,
                  xe_s, perm_s, gsort_s, meta_v, meta_sm, sem)

    _expert_group(step * EXPERTS_PER_STEP, xe_s, act_s, meta_sm, wg_ref, wu_ref, wd_ref)

    @pl.when(step == pl.num_programs(1) - 1)
    def _():
        for s in range(nsub):
            sl = slice(s * MOE_SUBTILE, (s + 1) * MOE_SUBTILE)
            for b in range(SORT_BLOCKS):
                rows = slice(b * SEG_ALIGN, (b + 1) * SEG_ALIGN)
                src = pl.multiple_of(meta_sm[s, b], SEG_ALIGN)
                blk = xe_s[pl.ds(src, SEG_ALIGN), :].astype(F32) * gsort_s[s, rows, :]
                ys_s[rows, :] = blk.astype(BF16)
            out_ref[sl, :] = x1_ref[sl, :] + _dot_tn(perm_s[s], ys_s[...])
        x2 = out_ref[...]
        hp = _rms(x2, nple_ref[...]).astype(BF16)
        gate = _sigmoid(_dot(hp, pg_ref[...]))
        proj = _dot(p_ref[...].astype(BF16), pp_ref[...])
        x3 = x2 + gate * proj
        out_ref[...] = x3
        if with_kv:
            hk = _rms(x3, kvn_ref[...]).astype(BF16)
            kv = _dot(hk, wkv_ref[...])
            k_ref[...] = _head_norm_rope(kv[:, :LANES], kn_ref[...], _same_head(),
                                         *_rope_parts(rope_ref), two_piece=True)
            v_ref[...] = kv[:, LANES:]


def _moe_layer(x1, p_all, layer, w, kv=None):
    n, d = x1.shape
    tm = TOKEN_TILE
    nt = n // tm
    nsub = tm // MOE_SUBTILE
    tok = lambda i, e: (i, 0)
    const2 = lambda i, e: (0, 0)
    lay3 = lambda i, e: (layer, 0, 0)
    exp4 = lambda i, e: (layer, e, 0, 0)
    in_specs = [
        pl.BlockSpec((tm, d), tok),
        pl.BlockSpec((None, tm, D_PLE), lambda i, e: (layer, i, 0)),
        pl.BlockSpec((None, 1, d), lay3),
        pl.BlockSpec((None, d, LANES), lay3),
        pl.BlockSpec((None, 1, LANES), lay3),
        pl.BlockSpec((None, EXPERTS_PER_STEP, d, D_EXPERT), exp4),
        pl.BlockSpec((None, EXPERTS_PER_STEP, d, D_EXPERT), exp4),
        pl.BlockSpec((None, EXPERTS_PER_STEP, D_EXPERT, d), exp4),
        pl.BlockSpec((None, 1, d), lay3),
        pl.BlockSpec((None, d, d), lay3),
        pl.BlockSpec((None, D_PLE, d), lay3),
    ]
    args = [x1, p_all, w["norm_ffn"], w["router_w"], w["router_b"],
            w["exp_gate"], w["exp_up"], w["exp_down"], w["norm_ple"], w["ple_gate"], w["ple_proj"]]
    out_specs = [pl.BlockSpec((tm, d), tok)]
    out_shape = [jax.ShapeDtypeStruct((n, d), F32)]
    if kv is not None:
        tab_blocks = kv["rope"].shape[0] // tm
        in_specs += [
            pl.BlockSpec((1, d), const2),
            pl.BlockSpec((d, 2 * LANES), const2),
            pl.BlockSpec((1, LANES), const2),
            pl.BlockSpec((tm, 3 * LANES), lambda i, e: (i % tab_blocks, 0)),
        ]
        args += [kv["kv_norm"], kv["w_kv"], kv["k_norm"], kv["rope"]]
        out_specs += [pl.BlockSpec((tm, LANES), tok), pl.BlockSpec((tm, LANES), tok)]
        out_shape += [jax.ShapeDtypeStruct((n, LANES), F32), jax.ShapeDtypeStruct((n, LANES), F32)]
    return pl.pallas_call(
        functools.partial(_moe_kernel, kv is not None),
        grid=(nt, N_EXPERTS // EXPERTS_PER_STEP),
        in_specs=in_specs,
        out_specs=out_specs,
        out_shape=out_shape,
        scratch_shapes=[
            pltpu.VMEM((XE_DUMP_ROW + SEG_ALIGN, d), BF16),
            pltpu.VMEM((SORT_ROWS, d), BF16),
            pltpu.VMEM((EXPERTS_PER_STEP, EXPERT_FIXED, D_EXPERT), BF16),
            pltpu.VMEM((nsub, SORT_ROWS, MOE_SUBTILE), BF16),
            pltpu.VMEM((nsub, SORT_ROWS, 1), F32),
            pltpu.VMEM((N_EXPERTS, LANES), jnp.int32),
            pltpu.SMEM((N_EXPERTS, LANES), jnp.int32),
            pltpu.SemaphoreType.DMA,
        ],
        compiler_params=pltpu.CompilerParams(
            dimension_semantics=("arbitrary", "arbitrary"), vmem_limit_bytes=VMEM_LIMIT),
        name="moe_kv" if kv is not None else "moe",
    )(*args)


def _project_q(x3, nmix_ref, wq_ref, qn_ref, cos, sin_next, sin_prev, q_s):
    h = _rms(x3, nmix_ref[...]).astype(BF16)
    same_head = _same_head()
    wide = 4 * LANES
    for blk in range(D_MODEL // wide):
        q = _dot(h, wq_ref[:, blk * wide:(blk + 1) * wide])
        for c in range(wide // LANES):
            cols = slice(blk * wide + c * LANES, blk * wide + (c + 1) * LANES)
            qc = _head_norm_rope(q[:, c * LANES:(c + 1) * LANES], qn_ref[...], same_head,
                                 cos, sin_next, sin_prev, two_piece=False)
            q_s[:, cols] = (qc * (HEAD_DIM ** -0.5 * LOG2_E)).astype(q_s.dtype)


def _both_orders(x):
    return x.astype(BF16), pltpu.roll(x, HEAD_DIM, 1).astype(BF16)


def _kv_order(par, g):
    return 0 if par == g else 1


def _softmax_sink(s, mask, sink):
    sink = sink * LOG2_E
    s = jnp.where(mask, s, NEG_BIG)
    m = jnp.maximum(jnp.max(s, axis=-1, keepdims=True), sink)
    p = jnp.exp2(s - m)
    den = jnp.sum(p, axis=-1, keepdims=True) + jnp.exp2(sink - m)
    return p.astype(BF16), 1.0 / den


def _attn_prompt_kernel(x3_ref, k_ref, v_ref, kprev_ref, vprev_ref, nmix_ref, wq_ref, qn_ref,
                        rope_ref, sinks_ref, wo_ref,
                        x4_ref, q_s, o_s, kd_s, vd_s):
    t = pl.program_id(1)
    tq = x3_ref.shape[1]
    x3 = x3_ref[0]
    _project_q(x3, nmix_ref, wq_ref, qn_ref, *_rope_parts(rope_ref), q_s)

    for src_prev, src, dst in ((kprev_ref, k_ref, kd_s), (vprev_ref, v_ref, vd_s)):
        dst[0, 0:WINDOW, :], dst[1, 0:WINDOW, :] = _both_orders(src_prev[0])
        dst[0, WINDOW:, :], dst[1, WINDOW:, :] = _both_orders(src[0])

    qi = lax.broadcasted_iota(jnp.int32, (WINDOW, 2 * WINDOW), 0)
    ki = lax.broadcasted_iota(jnp.int32, (WINDOW, 2 * WINDOW), 1)
    band = (ki > qi) & (ki <= qi + WINDOW)
    lane = lax.broadcasted_iota(jnp.int32, (WINDOW, LANES), 1)
    first = lane < HEAD_DIM

    def q_block(j, carry):
        r0 = pl.multiple_of(j * WINDOW, WINDOW)
        mask = band & (ki >= (1 - j) * WINDOW - t * tq)
        for c in range(D_MODEL // LANES):
            g = (2 * c) // Q_PER_KV
            cols = slice(c * LANES, (c + 1) * LANES)
            qc = q_s[pl.ds(r0, WINDOW), cols]
            outs = []
            for par in range(2):
                order = _kv_order(par, g)
                qh = jnp.where(first, qc, 0) if par == 0 else jnp.where(first, 0, qc)
                s = _dot_nt(qh, kd_s[order, pl.ds(r0, 2 * WINDOW), :])
                pr, inv = _softmax_sink(s, mask, sinks_ref[2 * c + par])
                outs.append(_dot(pr, vd_s[order, pl.ds(r0, 2 * WINDOW), :]) * inv)
            o_s[pl.ds(r0, WINDOW), cols] = jnp.where(first, outs[0], outs[1]).astype(BF16)
        return carry

    lax.fori_loop(0, tq // WINDOW, q_block, 0)
    x4_ref[0] = x3 + _dot(o_s[...], wo_ref[...])


def _attn_prompt(x3, k, v, w, rope):
    b, seq, d = x3.shape
    tq = ATTN_TILE
    nt = seq // tq
    per = tq // WINDOW
    tile = lambda bi, ti: (bi, ti, 0)
    prev = lambda bi, ti: (bi, jnp.maximum(ti * per - 1, 0), 0)
    const2 = lambda bi, ti: (0, 0)
    tab = lambda bi, ti: (ti, 0)
    return pl.pallas_call(
        _attn_prompt_kernel,
        grid=(b, nt),
        in_specs=[
            pl.BlockSpec((1, tq, d), tile),
            pl.BlockSpec((1, tq, LANES), tile),
            pl.BlockSpec((1, tq, LANES), tile),
            pl.BlockSpec((1, WINDOW, LANES), prev),
            pl.BlockSpec((1, WINDOW, LANES), prev),
            pl.BlockSpec((1, d), const2),
            pl.BlockSpec((d, d), const2),
            pl.BlockSpec((1, LANES), const2),
            pl.BlockSpec((tq, 3 * LANES), tab),
            pl.BlockSpec(memory_space=pltpu.SMEM),
            pl.BlockSpec((d, d), const2),
        ],
        out_specs=pl.BlockSpec((1, tq, d), tile),
        out_shape=jax.ShapeDtypeStruct((b, seq, d), F32),
        scratch_shapes=[
            pltpu.VMEM((tq, d), BF16),
            pltpu.VMEM((tq, d), BF16),
            pltpu.VMEM((N_KV_HEADS, tq + WINDOW, LANES), BF16),
            pltpu.VMEM((N_KV_HEADS, tq + WINDOW, LANES), BF16),
        ],
        compiler_params=pltpu.CompilerParams(
            dimension_semantics=("arbitrary", "arbitrary"), vmem_limit_bytes=VMEM_LIMIT),
        name="attn_prompt",
    )(x3, k, v, k, v, w["norm_mix1"], w["w_q"], w["q_norm"], rope, w["sinks"], w["w_o"])


def _attn_sample_kernel(x3_ref, knew_ref, vnew_ref, kc_ref, vc_ref, nmix_ref, wq_ref, qn_ref,
                        rope_ref, sinks_ref, wo_ref,
                        x4_ref, q_s, o_s):
    rows = x3_ref.shape[0]
    steps = knew_ref.shape[0] // kc_ref.shape[0]
    x3 = x3_ref[...]
    _project_q(x3, nmix_ref, wq_ref, qn_ref, *_rope_parts(rope_ref), q_s)

    stack = N_HEADS * steps
    ri = lax.broadcasted_iota(jnp.int32, (stack, 2 * WINDOW), 0)
    ki = lax.broadcasted_iota(jnp.int32, (stack, 2 * WINDOW), 1)
    tq = ri % steps
    mask = ((ki < WINDOW) & (ki > tq)) | ((ki >= WINDOW) & ((ki - WINDOW) <= tq))
    head_of_row = lax.broadcasted_iota(jnp.int32, (stack, 1), 0) // steps
    sink_col = jnp.zeros((stack, 1), F32)
    for hd in range(N_HEADS):
        sink_col = jnp.where(head_of_row == hd, sinks_ref[hd], sink_col)
    lane = lax.broadcasted_iota(jnp.int32, (steps, LANES), 1)
    first = lane < HEAD_DIM
    pad = jnp.zeros((WINDOW - steps, LANES), F32)

    def to_kv_half(slab, par, g):
        return slab if par == g else pltpu.roll(slab, HEAD_DIM, 1)

    def one_seq(b):
        r0 = pl.multiple_of(b * steps, steps)
        keys = jnp.concatenate([kc_ref[b], knew_ref[pl.ds(r0, steps), :], pad], axis=0).astype(BF16)
        vals = jnp.concatenate([vc_ref[b], vnew_ref[pl.ds(r0, steps), :], pad], axis=0).astype(BF16)
        slabs = []
        for hd in range(N_HEADS):
            par, g = hd % 2, hd // Q_PER_KV
            qc = q_s[pl.ds(r0, steps), (hd // 2) * LANES:(hd // 2 + 1) * LANES]
            own = jnp.where(first, qc, 0.0) if par == 0 else jnp.where(first, 0.0, qc)
            slabs.append(to_kv_half(own, par, g))
        s = _dot_nt(jnp.concatenate(slabs, axis=0).astype(BF16), keys)
        pr, inv = _softmax_sink(s, mask, sink_col)
        out = _dot(pr, vals) * inv
        for c in range(D_MODEL // LANES):
            g = (2 * c) // Q_PER_KV
            even = to_kv_half(out[(2 * c) * steps:(2 * c + 1) * steps], g, 0)
            odd = to_kv_half(out[(2 * c + 1) * steps:(2 * c + 2) * steps], g, 1)
            o_s[pl.ds(r0, steps), c * LANES:(c + 1) * LANES] = jnp.where(first, even, odd)

    def seq_pair(i, carry):
        for u in range(SAMPLE_SEQ_UNROLL):
            one_seq(i * SAMPLE_SEQ_UNROLL + u)
        return carry

    lax.fori_loop(0, rows // steps // SAMPLE_SEQ_UNROLL, seq_pair, 0)
    x4_ref[...] = x3 + _dot(o_s[...].astype(BF16), wo_ref[...])


def _attn_sample(x3, knew, vnew, kcache, vcache, w, rope, steps):
    n, d = x3.shape
    sb = SAMPLE_SEQ_BLOCK
    rows = sb * steps
    tok = lambda i: (i, 0)
    const2 = lambda i: (0, 0)
    return pl.pallas_call(
        _attn_sample_kernel,
        grid=(n // rows,),
        in_specs=[
            pl.BlockSpec((rows, d), tok),
            pl.BlockSpec((rows, LANES), tok),
            pl.BlockSpec((rows, LANES), tok),
            pl.BlockSpec((sb, WINDOW, LANES), lambda i: (i, 0, 0)),
            pl.BlockSpec((sb, WINDOW, LANES), lambda i: (i, 0, 0)),
            pl.BlockSpec((1, d), const2),
            pl.BlockSpec((d, d), const2),
            pl.BlockSpec((1, LANES), const2),
            pl.BlockSpec((rows, 3 * LANES), tok),
            pl.BlockSpec(memory_space=pltpu.SMEM),
            pl.BlockSpec((d, d), const2),
        ],
        out_specs=pl.BlockSpec((rows, d), tok),
        out_shape=jax.ShapeDtypeStruct((n, d), F32),
        scratch_shapes=[
            pltpu.VMEM((rows, d), F32),
            pltpu.VMEM((rows, d), F32),
        ],
        compiler_params=pltpu.CompilerParams(
            dimension_semantics=("arbitrary",), vmem_limit_bytes=VMEM_LIMIT),
        name="attn_sample",
    )(x3, knew, vnew, kcache, vcache, w["norm_mix1"], w["w_q"], w["q_norm"], rope,
      w["sinks"], w["w_o"])


def _rope_table(pos):
    half = ROPE_DIM // 2
    dim = jnp.arange(LANES, dtype=jnp.int32) % HEAD_DIM
    inv = jnp.float32(ROPE_THETA) ** (-((dim % half).astype(jnp.float32) * 2.0 / ROPE_DIM))
    inv = jnp.where(dim < ROPE_DIM, inv, 0.0)
    ang = pos.astype(jnp.float32)[:, None] * inv[None, :]
    cos, sin = jnp.cos(ang), jnp.sin(ang)
    s_next = jnp.where(dim < half, -sin, 0.0)
    s_prev = jnp.where((dim >= half) & (dim < ROPE_DIM), sin, 0.0)
    return jnp.concatenate([cos, s_next, s_prev], axis=1)


def kernel(x_prompt, x_sample, p_prompt, p_sample, state_pool, cache_k_win, cache_v_win, norm_mix, norm_ffn, norm_ple, pool_w, pool_scale, kv_norm, w_kv, k_norm, w_q, q_norm, sinks, w_o, router_g_w, router_g_b, router_e_w, router_e_b, exp_gate, exp_up, exp_down, ple_gate, ple_proj):
    b, seq, d = x_prompt.shape
    sb, steps, _ = x_sample.shape
    depth = norm_mix.shape[0]
    row = lambda v: v.reshape(1, -1)

    gap_w = jnp.zeros((depth, d, EXPERT_LANE0 - N_GROUPS), F32)
    tail_w = jnp.zeros((depth, d, LANES - EXPERT_LANE0 - N_EXPERTS), F32)
    router_w = jnp.concatenate(
        [router_g_w, gap_w, router_e_w.reshape(depth, d, N_EXPERTS), tail_w], axis=2)
    router_b = jnp.concatenate(
        [router_g_b, gap_w[:, 0], router_e_b.reshape(depth, N_EXPERTS), tail_w[:, 0]], axis=1)
    moe_w = {
        "norm_ffn": norm_ffn.reshape(depth, 1, d), "router_w": router_w,
        "router_b": router_b.reshape(depth, 1, LANES),
        "exp_gate": exp_gate.astype(BF16), "exp_up": exp_up.astype(BF16),
        "exp_down": exp_down.astype(BF16), "norm_ple": norm_ple.reshape(depth, 1, d),
        "ple_gate": ple_gate.astype(BF16), "ple_proj": ple_proj.astype(BF16),
    }
    two = lambda a: jnp.concatenate([a, a])
    attn_w = {
        "norm_mix1": row(norm_mix[1]), "w_q": w_q[0].astype(BF16), "q_norm": row(two(q_norm[0])),
        "sinks": sinks[0], "w_o": w_o[0].astype(BF16),
    }
    pw = pool_w[0]
    nmix0 = row(norm_mix[0])
    pscale = row(pool_scale[0])

    rope_p = _rope_table(jnp.arange(seq, dtype=jnp.int32))
    rope_s = jnp.tile(_rope_table(PAST_LEN + jnp.arange(steps, dtype=jnp.int32)), (sb, 1))
    kv_w = {"kv_norm": row(kv_norm), "w_kv": w_kv.astype(BF16), "k_norm": row(two(k_norm))}

    pp_all = p_prompt.reshape(depth, b * seq, D_PLE)
    x1p, pool16 = _mixer0_prompt(x_prompt, nmix0, pw, pscale)
    x3p, kp, vp = _moe_layer(x1p.reshape(b * seq, d), pp_all, 0, moe_w, dict(kv_w, rope=rope_p))
    kp = kp.reshape(b, seq, LANES)
    vp = vp.reshape(b, seq, LANES)
    x4p = _attn_prompt(x3p.reshape(b, seq, d), kp, vp, attn_w, rope_p)
    y_prompt = _moe_layer(x4p.reshape(b * seq, d), pp_all, 1, moe_w)[0].reshape(b, seq, d)
    pool_prompt = pool16[None, :, CTX_ROWS - POOL_CTX:, :]
    k_win_prompt = kp[:, seq - WINDOW:].reshape(b, WINDOW, N_KV_HEADS, HEAD_DIM)
    v_win_prompt = vp[:, seq - WINDOW:].reshape(b, WINDOW, N_KV_HEADS, HEAD_DIM)

    ps_all = p_sample.reshape(depth, sb * steps, D_PLE)
    x1s, pool_s = _mixer0_sample(x_sample, state_pool[0], nmix0, pw, pscale)
    x3s, ks, vs = _moe_layer(x1s.reshape(sb * steps, d), ps_all, 0, moe_w, dict(kv_w, rope=rope_s))
    kc = cache_k_win.reshape(sb, WINDOW, LANES)
    vc = cache_v_win.reshape(sb, WINDOW, LANES)
    x4s = _attn_sample(x3s, ks, vs, kc, vc, attn_w, rope_s, steps)
    y_sample = _moe_layer(x4s, ps_all, 1, moe_w)[0].reshape(sb, steps, d)
    pool_sample = pool_s[None]
    heads = (sb, steps, N_KV_HEADS, HEAD_DIM)
    k_win_sample = jnp.concatenate([cache_k_win[:, steps:], ks.reshape(heads)], axis=1)
    v_win_sample = jnp.concatenate([cache_v_win[:, steps:], vs.reshape(heads)], axis=1)

    return (y_prompt, y_sample, pool_prompt, pool_sample,
            k_win_prompt, v_win_prompt, k_win_sample, v_win_sample)
```

```python
import functools

import jax
import jax.numpy as jnp
from jax import lax
from jax.experimental import pallas as pl
from jax.experimental.pallas import tpu as pltpu

D_MODEL = 1024
PAST_LEN = 8192
POOL_WINDOWS = (2, 4, 8, 16)
POOL_GROUP_DIM = D_MODEL // len(POOL_WINDOWS)
POOL_CTX = max(POOL_WINDOWS) - 1
assert all(win == 2 ** (g + 1) for g, win in enumerate(POOL_WINDOWS))
HEAD_DIM = 64
N_HEADS = D_MODEL // HEAD_DIM
N_KV_HEADS = 2
Q_PER_KV = N_HEADS // N_KV_HEADS
WINDOW = 128
ROPE_DIM = HEAD_DIM // 4
ROPE_THETA = 500000.0
N_GROUPS = 4
E_PER_GROUP = 4
N_EXPERTS = N_GROUPS * E_PER_GROUP
D_EXPERT = 256
D_PLE = 256
EPS = 1e-6

LANES = 128
SUBLANES = 8
CTX_ROWS = 16
TOKEN_TILE = 1024
ATTN_TILE = 1024
SAMPLE_SEQ_BLOCK = 16
SAMPLE_SEQ_UNROLL = 8
VMEM_LIMIT = 60 * 1024 * 1024
NEG_BIG = -1e30
LOG2_E = 1.4426950408889634
F32 = jnp.float32
BF16 = jnp.bfloat16

EXPERT_LANE0 = SUBLANES
MOE_SUBTILE = 256
SEG_ALIGN = 2 * SUBLANES
SORT_ROWS = 768
EXPERT_FIXED = 192
EXPERT_STEP = 32
EXPERTS_PER_STEP = 4
SORT_BLOCKS = SORT_ROWS // SEG_ALIGN
MOE_SUBTILES = TOKEN_TILE // MOE_SUBTILE
XE_DUMP_ROW = (2 * TOKEN_TILE + MOE_SUBTILES * N_EXPERTS * (SEG_ALIGN - 1)
               + N_EXPERTS * (EXPERT_STEP - SEG_ALIGN + EXPERT_FIXED))
META_BASE, META_ROWS = 64, 65
assert SORT_ROWS >= 2 * MOE_SUBTILE + N_EXPERTS * (SEG_ALIGN - 1) and SORT_ROWS % LANES == 0
assert TOKEN_TILE % MOE_SUBTILE == 0 and MOE_SUBTILES <= N_EXPERTS and SORT_BLOCKS <= META_BASE
assert EXPERT_FIXED % EXPERT_STEP == 0 and EXPERT_STEP % SEG_ALIGN == 0
assert N_EXPERTS % EXPERTS_PER_STEP == 0 and XE_DUMP_ROW % SEG_ALIGN == 0


def _rms(x, g):
    return x * lax.rsqrt(jnp.mean(x * x, axis=-1, keepdims=True) + EPS) * g


def _dot(a, b):
    return jnp.dot(a, b, preferred_element_type=F32)


def _dot_nt(a, b):
    return lax.dot_general(a, b, (((1,), (1,)), ((), ())), preferred_element_type=F32)


def _dot_tn(a, b):
    return lax.dot_general(a, b, (((0,), (0,)), ((), ())), preferred_element_type=F32)


def _split_bf16(x):
    hi = x.astype(BF16)
    return hi, (x - hi.astype(F32)).astype(BF16)


def _dot3(a_hi, a_lo, b):
    b_hi, b_lo = _split_bf16(b)
    return _dot(a_hi, b_hi) + _dot(a_lo, b_hi) + _dot(a_hi, b_lo)


def _sigmoid(x):
    return 1.0 / (1.0 + jnp.exp(-x))


def _rope_parts(rope_ref):
    return rope_ref[:, 0:LANES], rope_ref[:, LANES:2 * LANES], rope_ref[:, 2 * LANES:3 * LANES]


def _same_head():
    hi = lax.broadcasted_iota(jnp.int32, (LANES, LANES), 0) // HEAD_DIM
    hj = lax.broadcasted_iota(jnp.int32, (LANES, LANES), 1) // HEAD_DIM
    return jnp.where(hi == hj, 1.0, 0.0).astype(BF16)


def _head_norm_rope(x, gain, same_head, cos, sin_next, sin_prev, two_piece):
    sq_hi, sq_lo = _split_bf16(x * x)
    ssq = _dot(sq_hi, same_head)
    if two_piece:
        ssq = ssq + _dot(sq_lo, same_head)
    y = x * lax.rsqrt(ssq * (1.0 / HEAD_DIM) + EPS) * gain
    half = ROPE_DIM // 2
    return (y * cos + pltpu.roll(y, LANES - half, 1) * sin_next
            + pltpu.roll(y, half, 1) * sin_prev)


def _pool_mix(h, window_sum, pos, x, pw_ref, pscale_ref, store):
    for g, win in enumerate(POOL_WINDOWS):
        cols = slice(g * POOL_GROUP_DIM, (g + 1) * POOL_GROUP_DIM)
        hg = h[..., cols]
        inv_cnt = 1.0 / jnp.minimum(win, pos + 1).astype(F32)
        pooled = (window_sum(g, cols) * inv_cnt - hg).reshape(-1, POOL_GROUP_DIM)
        hi, lo = _split_bf16(pooled)
        mixed = _dot3(hi, lo, pw_ref[g])
        store(cols, x[..., cols] + mixed.reshape(hg.shape) * pscale_ref[:, cols])


def _mixer0_prompt_kernel(x_ref, xprev_ref, nmix_ref, pw_ref, pscale_ref,
                          x1_ref, pool_ref, hs_ref, ua_ref, ub_ref):
    t = pl.program_id(1)
    tm = x_ref.shape[1]
    x = x_ref[0]
    h = _rms(x, nmix_ref[...])
    hp = _rms(xprev_ref[0], nmix_ref[...])
    top = 2 * CTX_ROWS
    hs_ref[0:CTX_ROWS, :] = jnp.zeros((CTX_ROWS, x.shape[1]), F32)
    hs_ref[CTX_ROWS:top, :] = jnp.where(t > 0, hp, 0.0)
    hs_ref[top:, :] = h
    ua_ref[0:SUBLANES, :] = jnp.zeros((SUBLANES, x.shape[1]), F32)
    ub_ref[0:SUBLANES, :] = jnp.zeros((SUBLANES, x.shape[1]), F32)

    n = tm + top - SUBLANES
    src, level_of = hs_ref, []
    for g, win in enumerate(POOL_WINDOWS):
        dst = ua_ref if g % 2 == 0 else ub_ref
        c0 = g * POOL_GROUP_DIM
        back = win // 2
        dst[SUBLANES:, c0:] = src[SUBLANES:, c0:] + src[SUBLANES - back:SUBLANES - back + n, c0:]
        level_of.append(dst)
        src = dst

    def window_sum(g, cols):
        return level_of[g][top:, cols]

    def store(cols, val):
        x1_ref[0, :, cols] = val

    pos = t * tm + lax.broadcasted_iota(jnp.int32, (tm, 1), 0)
    _pool_mix(h, window_sum, pos, x, pw_ref, pscale_ref, store)

    @pl.when(t == pl.num_programs(1) - 1)
    def _():
        pool_ref[0] = hs_ref[tm + top - CTX_ROWS:tm + top, :]


def _mixer0_sample_kernel(x_ref, ctx_ref, nmix_ref, pw_ref, pscale_ref,
                          x1_ref, pool_ref, hs_ref):
    x = x_ref[...]
    steps = x.shape[1]
    h = _rms(x, nmix_ref[...])
    hs_ref[:, CTX_ROWS - POOL_CTX:CTX_ROWS, :] = ctx_ref[...]
    hs_ref[:, CTX_ROWS:, :] = h
    pool_ref[:, 0:POOL_CTX - steps, :] = ctx_ref[:, steps:, :]
    pool_ref[:, POOL_CTX - steps:, :] = h

    def window_sum(g, cols):
        s = h[..., cols]
        for j in range(1, POOL_WINDOWS[g]):
            s = s + hs_ref[:, CTX_ROWS - j:CTX_ROWS - j + steps, cols]
        return s

    def store(cols, val):
        x1_ref[:, :, cols] = val

    pos = PAST_LEN + lax.broadcasted_iota(jnp.int32, (1, steps, 1), 1)
    _pool_mix(h, window_sum, pos, x, pw_ref, pscale_ref, store)


def _mixer0_prompt(x, nmix, pw, pscale):
    b, seq, d = x.shape
    tm = TOKEN_TILE
    nt = seq // tm
    per = tm // CTX_ROWS
    const2 = lambda bi, ti: (0, 0)
    return pl.pallas_call(
        _mixer0_prompt_kernel,
        grid=(b, nt),
        in_specs=[
            pl.BlockSpec((1, tm, d), lambda bi, ti: (bi, ti, 0)),
            pl.BlockSpec((1, CTX_ROWS, d), lambda bi, ti: (bi, jnp.maximum(ti * per - 1, 0), 0)),
            pl.BlockSpec((1, d), const2),
            pl.BlockSpec(pw.shape, lambda bi, ti: (0, 0, 0)),
            pl.BlockSpec((1, d), const2),
        ],
        out_specs=[
            pl.BlockSpec((1, tm, d), lambda bi, ti: (bi, ti, 0)),
            pl.BlockSpec((1, CTX_ROWS, d), lambda bi, ti: (bi, 0, 0)),
        ],
        out_shape=[
            jax.ShapeDtypeStruct((b, seq, d), F32),
            jax.ShapeDtypeStruct((b, CTX_ROWS, d), F32),
        ],
        scratch_shapes=[pltpu.VMEM((tm + 2 * CTX_ROWS, d), F32)] * 3,
        compiler_params=pltpu.CompilerParams(
            dimension_semantics=("arbitrary", "arbitrary"), vmem_limit_bytes=VMEM_LIMIT),
        name="mixer0_prompt",
    )(x, x, nmix, pw, pscale)


def _mixer0_sample(x, ctx_layers, nmix, pw, pscale):
    b, steps, d = x.shape
    sb = SAMPLE_SEQ_BLOCK
    const2 = lambda i: (0, 0)
    return pl.pallas_call(
        _mixer0_sample_kernel,
        grid=(b // sb,),
        in_specs=[
            pl.BlockSpec((sb, steps, d), lambda i: (i, 0, 0)),
            pl.BlockSpec((None, sb, POOL_CTX, d), lambda i: (0, i, 0, 0)),
            pl.BlockSpec((1, d), const2),
            pl.BlockSpec(pw.shape, lambda i: (0, 0, 0)),
            pl.BlockSpec((1, d), const2),
        ],
        out_specs=[
            pl.BlockSpec((sb, steps, d), lambda i: (i, 0, 0)),
            pl.BlockSpec((sb, POOL_CTX, d), lambda i: (i, 0, 0)),
        ],
        out_shape=[
            jax.ShapeDtypeStruct((b, steps, d), F32),
            jax.ShapeDtypeStruct((b, POOL_CTX, d), F32),
        ],
        scratch_shapes=[pltpu.VMEM((sb, CTX_ROWS + steps, d), F32)],
        compiler_params=pltpu.CompilerParams(
            dimension_semantics=("arbitrary",), vmem_limit_bytes=VMEM_LIMIT),
        name="mixer0_sample",
    )(x, ctx_layers, nmix, pw, pscale)


def _route_rows(glog, elog):
    sub8 = lax.broadcasted_iota(jnp.int32, glog.shape, 0).astype(F32)
    is_grp = sub8 < N_GROUPS
    gmax = jnp.max(jnp.where(is_grp, glog, NEG_BIG), axis=0, keepdims=True)
    gsum = jnp.sum(jnp.where(is_grp, jnp.exp(jnp.minimum(glog - gmax, 0.0)), 0.0),
                   axis=0, keepdims=True)
    g_w = 1.0 / gsum
    g_sel = jnp.min(jnp.where(is_grp & (glog == gmax), sub8, float(N_GROUPS)),
                    axis=0, keepdims=True)
    sub16 = lax.broadcasted_iota(jnp.int32, elog.shape, 0).astype(F32)
    lo = E_PER_GROUP * g_sel
    in_grp = (sub16 >= lo) & (sub16 < lo + E_PER_GROUP)
    none = float(N_EXPERTS)
    v1 = jnp.max(jnp.where(in_grp, elog, NEG_BIG), axis=0, keepdims=True)
    i1 = jnp.min(jnp.where(in_grp & (elog == v1), sub16, none), axis=0, keepdims=True)
    rest = in_grp & (sub16 != i1)
    v2 = jnp.max(jnp.where(rest, elog, NEG_BIG), axis=0, keepdims=True)
    i2 = jnp.min(jnp.where(rest & (elog == v2), sub16, none), axis=0, keepdims=True)
    e2 = jnp.exp(v2 - v1)
    den = 1.0 + e2
    return i1, i2, (1.0 / den) * g_w, (e2 / den) * g_w


def _dispatch(x1_ref, nffn_ref, rw_ref, rb_ref,
              xe_s, perm_s, gsort_s, meta_v, meta_sm, sem):
    tm = x1_ref.shape[0]
    sub = MOE_SUBTILE
    nsub = tm // sub
    hi, lo = _split_bf16(_rms(x1_ref[...], nffn_ref[...]))
    w_hi, w_lo = _split_bf16(rw_ref[...])
    wide = _dot(hi, jnp.concatenate([w_hi, w_lo], axis=1))
    logits = wide[:, :LANES] + wide[:, LANES:] + _dot(lo, w_hi) + rb_ref[...]
    lt = logits.T
    i1, i2, w1, w2 = _route_rows(lt[0:SUBLANES], lt[EXPERT_LANE0:EXPERT_LANE0 + N_EXPERTS])

    sub16 = lax.broadcasted_iota(jnp.int32, (N_EXPERTS, 2 * sub), 0).astype(F32)
    esub = lax.broadcasted_iota(jnp.int32, (N_EXPERTS, LANES), 0)
    mlane = lax.broadcasted_iota(jnp.int32, (N_EXPERTS, LANES), 1)

    def round_up(x, step):
        return jnp.floor((x + (step - 1)) * (1.0 / step)) * step

    def starts(sizes):
        inc = sizes
        for k in (1, 2, 4, 8):
            inc = inc + jnp.where(esub >= k, pltpu.roll(inc, k, 0), 0.0)
        return inc - sizes

    onehots, sizes = [], []
    for s in range(nsub):
        sl = slice(s * sub, (s + 1) * sub)
        sel = jnp.concatenate([i1[:, sl], i2[:, sl]], axis=1)
        at = jnp.where(sub16 == sel, 1.0, 0.0)
        cnt = jnp.sum(at, axis=1, keepdims=True)
        onehots.append(at)
        sizes.append(jnp.broadcast_to(round_up(cnt, SEG_ALIGN), (N_EXPERTS, LANES)))

    region = jnp.maximum(round_up(sum(sizes), EXPERT_STEP), float(EXPERT_FIXED))
    base = starts(region)
    meta = jnp.where(mlane == META_BASE, base, jnp.where(mlane == META_ROWS, region, 0.0))
    blk0 = mlane.astype(F32) * SEG_ALIGN
    seg_starts, at_row = [], base
    for s in range(nsub):
        seg0 = starts(sizes[s])
        seg_starts.append(seg0)
        inside = (blk0 >= seg0) & (blk0 < seg0 + sizes[s])
        dst = jnp.sum(jnp.where(inside, at_row - seg0 + blk0, 0.0), axis=0, keepdims=True)
        used = jnp.sum(jnp.where(inside, 1.0, 0.0), axis=0, keepdims=True)
        dst = jnp.where(used > 0.0, dst, float(XE_DUMP_ROW))
        meta = jnp.where((esub == s) & (mlane < SORT_BLOCKS), dst, meta)
        at_row = at_row + sizes[s]
    meta_v[...] = meta.astype(jnp.int32)
    cp = pltpu.make_async_copy(meta_v, meta_sm, sem)
    cp.start()

    jr = lax.broadcasted_iota(jnp.int32, (2 * sub, 2 * sub), 0)
    jc = lax.broadcasted_iota(jnp.int32, (2 * sub, 2 * sub), 1)
    earlier = jnp.where(jr < jc, 1.0, 0.0).astype(BF16)
    rsub = lax.broadcasted_iota(jnp.int32, (SORT_ROWS, sub), 0).astype(F32)
    for s in range(nsub):
        sl = slice(s * sub, (s + 1) * sub)
        at = onehots[s]
        rank = _dot(at.astype(BF16), earlier)
        dest = jnp.sum(at * (seg_starts[s][:, 0:1] + rank), axis=0, keepdims=True)
        p1 = rsub == dest[:, :sub]
        p2 = rsub == dest[:, sub:]
        perm = jnp.where(p1 | p2, 1.0, 0.0).astype(BF16)
        gsort_s[s] = jnp.sum(jnp.where(p1, w1[:, sl], 0.0) + jnp.where(p2, w2[:, sl], 0.0),
                             axis=1, keepdims=True)
        perm_s[s] = perm
        xs = _dot(perm, hi[sl, :]).astype(BF16)
        if s == 0:
            cp.wait()
        for b in range(SORT_BLOCKS):
            row = pl.multiple_of(meta_sm[s, b], SEG_ALIGN)
            xe_s[pl.ds(row, SEG_ALIGN), :] = xs[b * SEG_ALIGN:(b + 1) * SEG_ALIGN, :]


def _expert_group(first, xe_s, act_s, meta_sm, wg_ref, wu_ref, wd_ref):
    def hidden(j, r0, m):
        x = xe_s[pl.ds(r0, m), :]
        gt = _dot(x, wg_ref[j])
        up = _dot(x, wu_ref[j])
        return ((gt * _sigmoid(gt)) * up).astype(BF16)

    bases = [pl.multiple_of(meta_sm[first + j, META_BASE], EXPERT_STEP)
             for j in range(EXPERTS_PER_STEP)]
    for j in range(EXPERTS_PER_STEP):
        act_s[j] = hidden(j, bases[j], EXPERT_FIXED)
    for j in range(EXPERTS_PER_STEP):
        xe_s[pl.ds(bases[j], EXPERT_FIXED), :] = _dot(act_s[j], wd_ref[j]).astype(BF16)
    for j in range(EXPERTS_PER_STEP):
        def piece(c, carry, j=j):
            r0 = pl.multiple_of(bases[j] + EXPERT_FIXED + c * EXPERT_STEP, EXPERT_STEP)
            xe_s[pl.ds(r0, EXPERT_STEP), :] = _dot(hidden(j, r0, EXPERT_STEP), wd_ref[j]).astype(BF16)
            return carry
        lax.fori_loop(0, (meta_sm[first + j, META_ROWS] - EXPERT_FIXED) // EXPERT_STEP, piece, 0)


def _moe_kernel(with_kv, *refs):
    if with_kv:
        (x1_ref, p_ref, nffn_ref, rw_ref, rb_ref, wg_ref, wu_ref, wd_ref,
         nple_ref, pg_ref, pp_ref, kvn_ref, wkv_ref, kn_ref, rope_ref,
         out_ref, k_ref, v_ref, xe_s, ys_s, act_s, perm_s, gsort_s, meta_v, meta_sm, sem) = refs
    else:
        (x1_ref, p_ref, nffn_ref, rw_ref, rb_ref, wg_ref, wu_ref, wd_ref,
         nple_ref, pg_ref, pp_ref,
         out_ref, xe_s, ys_s, act_s, perm_s, gsort_s, meta_v, meta_sm, sem) = refs
    i = pl.program_id(0)
    step = pl.program_id(1)
    nsub = x1_ref.shape[0] // MOE_SUBTILE

    @pl.when((i == 0) & (step == 0))
    def _():
        xe_s[...] = jnp.zeros_like(xe_s)

    @pl.when(step == 0)
    def _():
        _dispatch(x1_ref, nffn_ref, rw_ref, rb_ref,
                  xe_s, perm_s, gsort_s, meta_v, meta_sm, sem)

    _expert_group(step * EXPERTS_PER_STEP, xe_s, act_s, meta_sm, wg_ref, wu_ref, wd_ref)

    @pl.when(step == pl.num_programs(1) - 1)
    def _():
        for s in range(nsub):
            sl = slice(s * MOE_SUBTILE, (s + 1) * MOE_SUBTILE)
            for b in range(SORT_BLOCKS):
                rows = slice(b * SEG_ALIGN, (b + 1) * SEG_ALIGN)
                src = pl.multiple_of(meta_sm[s, b], SEG_ALIGN)
                blk = xe_s[pl.ds(src, SEG_ALIGN), :].astype(F32) * gsort_s[s, rows, :]
                ys_s[rows, :] = blk.astype(BF16)
            out_ref[sl, :] = x1_ref[sl, :] + _dot_tn(perm_s[s], ys_s[...])
        x2 = out_ref[...]
        hp = _rms(x2, nple_ref[...]).astype(BF16)
        gate = _sigmoid(_dot(hp, pg_ref[...]))
        proj = _dot(p_ref[...].astype(BF16), pp_ref[...])
        x3 = x2 + gate * proj
        out_ref[...] = x3
        if with_kv:
            hk = _rms(x3, kvn_ref[...]).astype(BF16)
            kv = _dot(hk, wkv_ref[...])
            k_ref[...] = _head_norm_rope(kv[:, :LANES], kn_ref[...], _same_head(),
                                         *_rope_parts(rope_ref), two_piece=True)
            v_ref[...] = kv[:, LANES:]


def _moe_layer(x1, p_all, layer, w, kv=None):
    n, d = x1.shape
    tm = TOKEN_TILE
    nt = n // tm
    nsub = tm // MOE_SUBTILE
    tok = lambda i, e: (i, 0)
    const2 = lambda i, e: (0, 0)
    lay3 = lambda i, e: (layer, 0, 0)
    exp4 = lambda i, e: (layer, e, 0, 0)
    in_specs = [
        pl.BlockSpec((tm, d), tok),
        pl.BlockSpec((None, tm, D_PLE), lambda i, e: (layer, i, 0)),
        pl.BlockSpec((None, 1, d), lay3),
        pl.BlockSpec((None, d, LANES), lay3),
        pl.BlockSpec((None, 1, LANES), lay3),
        pl.BlockSpec((None, EXPERTS_PER_STEP, d, D_EXPERT), exp4),
        pl.BlockSpec((None, EXPERTS_PER_STEP, d, D_EXPERT), exp4),
        pl.BlockSpec((None, EXPERTS_PER_STEP, D_EXPERT, d), exp4),
        pl.BlockSpec((None, 1, d), lay3),
        pl.BlockSpec((None, d, d), lay3),
        pl.BlockSpec((None, D_PLE, d), lay3),
    ]
    args = [x1, p_all, w["norm_ffn"], w["router_w"], w["router_b"],
            w["exp_gate"], w["exp_up"], w["exp_down"], w["norm_ple"], w["ple_gate"], w["ple_proj"]]
    out_specs = [pl.BlockSpec((tm, d), tok)]
    out_shape = [jax.ShapeDtypeStruct((n, d), F32)]
    if kv is not None:
        tab_blocks = kv["rope"].shape[0] // tm
        in_specs += [
            pl.BlockSpec((1, d), const2),
            pl.BlockSpec((d, 2 * LANES), const2),
            pl.BlockSpec((1, LANES), const2),
            pl.BlockSpec((tm, 3 * LANES), lambda i, e: (i % tab_blocks, 0)),
        ]
        args += [kv["kv_norm"], kv["w_kv"], kv["k_norm"], kv["rope"]]
        out_specs += [pl.BlockSpec((tm, LANES), tok), pl.BlockSpec((tm, LANES), tok)]
        out_shape += [jax.ShapeDtypeStruct((n, LANES), F32), jax.ShapeDtypeStruct((n, LANES), F32)]
    return pl.pallas_call(
        functools.partial(_moe_kernel, kv is not None),
        grid=(nt, N_EXPERTS // EXPERTS_PER_STEP),
        in_specs=in_specs,
        out_specs=out_specs,
        out_shape=out_shape,
        scratch_shapes=[
            pltpu.VMEM((XE_DUMP_ROW + SEG_ALIGN, d), BF16),
            pltpu.VMEM((SORT_ROWS, d), BF16),
            pltpu.VMEM((EXPERTS_PER_STEP, EXPERT_FIXED, D_EXPERT), BF16),
            pltpu.VMEM((nsub, SORT_ROWS, MOE_SUBTILE), BF16),
            pltpu.VMEM((nsub, SORT_ROWS, 1), F32),
            pltpu.VMEM((N_EXPERTS, LANES), jnp.int32),
            pltpu.SMEM((N_EXPERTS, LANES), jnp.int32),
            pltpu.SemaphoreType.DMA,
        ],
        compiler_params=pltpu.CompilerParams(
            dimension_semantics=("arbitrary", "arbitrary"), vmem_limit_bytes=VMEM_LIMIT),
        name="moe_kv" if kv is not None else "moe",
    )(*args)


def _project_q(x3, nmix_ref, wq_ref, qn_ref, cos, sin_next, sin_prev, q_s):
    h = _rms(x3, nmix_ref[...]).astype(BF16)
    same_head = _same_head()
    wide = 4 * LANES
    for blk in range(D_MODEL // wide):
        q = _dot(h, wq_ref[:, blk * wide:(blk + 1) * wide])
        for c in range(wide // LANES):
            cols = slice(blk * wide + c * LANES, blk * wide + (c + 1) * LANES)
            qc = _head_norm_rope(q[:, c * LANES:(c + 1) * LANES], qn_ref[...], same_head,
                                 cos, sin_next, sin_prev, two_piece=False)
            q_s[:, cols] = (qc * (HEAD_DIM ** -0.5 * LOG2_E)).astype(q_s.dtype)


def _both_orders(x):
    return x.astype(BF16), pltpu.roll(x, HEAD_DIM, 1).astype(BF16)


def _kv_order(par, g):
    return 0 if par == g else 1


def _softmax_sink(s, mask, sink):
    col0 = lax.broadcasted_iota(jnp.int32, (s.shape[0], LANES), 1) == 0
    s = jnp.where(mask, s, NEG_BIG)
    s = jnp.concatenate([jnp.where(col0, sink * LOG2_E, s[:, :LANES]), s[:, LANES:]], axis=1)
    p = jnp.exp2(s - jnp.max(s, axis=-1, keepdims=True))
    return p.astype(BF16), 1.0 / jnp.sum(p, axis=-1, keepdims=True)


def _drop_first_row(vals, tile_rows):
    row = lax.broadcasted_iota(jnp.int32, (tile_rows, vals.shape[1]), 0)
    head = jnp.where(row == 0, jnp.zeros_like(vals[:tile_rows]), vals[:tile_rows])
    return jnp.concatenate([head, vals[tile_rows:]], axis=0)


def _attn_prompt_kernel(x3_ref, k_ref, v_ref, kprev_ref, vprev_ref, nmix_ref, wq_ref, qn_ref,
                        rope_ref, sinks_ref, wo_ref,
                        x4_ref, q_s, o_s, kd_s, vd_s):
    t = pl.program_id(1)
    tq = x3_ref.shape[1]
    x3 = x3_ref[0]
    _project_q(x3, nmix_ref, wq_ref, qn_ref, *_rope_parts(rope_ref), q_s)

    for src_prev, src, dst in ((kprev_ref, k_ref, kd_s), (vprev_ref, v_ref, vd_s)):
        dst[0, 0:WINDOW, :], dst[1, 0:WINDOW, :] = _both_orders(src_prev[0])
        dst[0, WINDOW:, :], dst[1, WINDOW:, :] = _both_orders(src[0])

    qi = lax.broadcasted_iota(jnp.int32, (WINDOW, 2 * WINDOW), 0)
    ki = lax.broadcasted_iota(jnp.int32, (WINDOW, 2 * WINDOW), 1)
    band = (ki > qi) & (ki <= qi + WINDOW)
    lane = lax.broadcasted_iota(jnp.int32, (WINDOW, LANES), 1)
    first = lane < HEAD_DIM

    def q_block(j, carry):
        r0 = pl.multiple_of(j * WINDOW, WINDOW)
        mask = band & (ki >= (1 - j) * WINDOW - t * tq)
        vals = [_drop_first_row(vd_s[order, pl.ds(r0, 2 * WINDOW), :], SEG_ALIGN)
                for order in range(2)]
        for c in range(D_MODEL // LANES):
            g = (2 * c) // Q_PER_KV
            cols = slice(c * LANES, (c + 1) * LANES)
            qc = q_s[pl.ds(r0, WINDOW), cols]
            outs = []
            for par in range(2):
                order = _kv_order(par, g)
                qh = jnp.where(first, qc, 0) if par == 0 else jnp.where(first, 0, qc)
                s = _dot_nt(qh, kd_s[order, pl.ds(r0, 2 * WINDOW), :])
                pr, inv = _softmax_sink(s, mask, sinks_ref[2 * c + par])
                outs.append(_dot(pr, vals[order]) * inv)
            o_s[pl.ds(r0, WINDOW), cols] = jnp.where(first, outs[0], outs[1]).astype(BF16)
        return carry

    lax.fori_loop(0, tq // WINDOW, q_block, 0)
    x4_ref[0] = x3 + _dot(o_s[...], wo_ref[...])


def _attn_prompt(x3, k, v, w, rope):
    b, seq, d = x3.shape
    tq = ATTN_TILE
    nt = seq // tq
    per = tq // WINDOW
    tile = lambda bi, ti: (bi, ti, 0)
    prev = lambda bi, ti: (bi, jnp.maximum(ti * per - 1, 0), 0)
    const2 = lambda bi, ti: (0, 0)
    tab = lambda bi, ti: (ti, 0)
    return pl.pallas_call(
        _attn_prompt_kernel,
        grid=(b, nt),
        in_specs=[
            pl.BlockSpec((1, tq, d), tile),
            pl.BlockSpec((1, tq, LANES), tile),
            pl.BlockSpec((1, tq, LANES), tile),
            pl.BlockSpec((1, WINDOW, LANES), prev),
            pl.BlockSpec((1, WINDOW, LANES), prev),
            pl.BlockSpec((1, d), const2),
            pl.BlockSpec((d, d), const2),
            pl.BlockSpec((1, LANES), const2),
            pl.BlockSpec((tq, 3 * LANES), tab),
            pl.BlockSpec(memory_space=pltpu.SMEM),
            pl.BlockSpec((d, d), const2),
        ],
        out_specs=pl.BlockSpec((1, tq, d), tile),
        out_shape=jax.ShapeDtypeStruct((b, seq, d), F32),
        scratch_shapes=[
            pltpu.VMEM((tq, d), BF16),
            pltpu.VMEM((tq, d), BF16),
            pltpu.VMEM((N_KV_HEADS, tq + WINDOW, LANES), BF16),
            pltpu.VMEM((N_KV_HEADS, tq + WINDOW, LANES), BF16),
        ],
        compiler_params=pltpu.CompilerParams(
            dimension_semantics=("arbitrary", "arbitrary"), vmem_limit_bytes=VMEM_LIMIT),
        name="attn_prompt",
    )(x3, k, v, k, v, w["norm_mix1"], w["w_q"], w["q_norm"], rope, w["sinks"], w["w_o"])


def _attn_sample_kernel(x3_ref, knew_ref, vnew_ref, kc_ref, vc_ref, nmix_ref, wq_ref, qn_ref,
                        rope_ref, sinks_ref, wo_ref,
                        x4_ref, q_s, o_s):
    rows = x3_ref.shape[0]
    steps = knew_ref.shape[0] // kc_ref.shape[0]
    x3 = x3_ref[...]
    _project_q(x3, nmix_ref, wq_ref, qn_ref, *_rope_parts(rope_ref), q_s)

    stack = N_HEADS * steps
    ri = lax.broadcasted_iota(jnp.int32, (stack, 2 * WINDOW), 0)
    ki = lax.broadcasted_iota(jnp.int32, (stack, 2 * WINDOW), 1)
    tq = ri % steps
    mask = ((ki < WINDOW) & (ki > tq)) | ((ki >= WINDOW) & ((ki - WINDOW) <= tq))
    head_of_row = lax.broadcasted_iota(jnp.int32, (stack, 1), 0) // steps
    sink_col = jnp.zeros((stack, 1), F32)
    for hd in range(N_HEADS):
        sink_col = jnp.where(head_of_row == hd, sinks_ref[hd], sink_col)
    lane = lax.broadcasted_iota(jnp.int32, (steps, LANES), 1)
    first = lane < HEAD_DIM
    pad = jnp.zeros((WINDOW - steps, LANES), F32)

    def to_kv_half(slab, par, g):
        return slab if par == g else pltpu.roll(slab, HEAD_DIM, 1)

    def one_seq(b):
        r0 = pl.multiple_of(b * steps, steps)
        keys = jnp.concatenate([kc_ref[b], knew_ref[pl.ds(r0, steps), :], pad], axis=0).astype(BF16)
        vals = jnp.concatenate([_drop_first_row(vc_ref[b], SUBLANES), vnew_ref[pl.ds(r0, steps), :],
                                pad], axis=0).astype(BF16)
        slabs = []
        for hd in range(N_HEADS):
            par, g = hd % 2, hd // Q_PER_KV
            qc = q_s[pl.ds(r0, steps), (hd // 2) * LANES:(hd // 2 + 1) * LANES]
            own = jnp.where(first, qc, 0.0) if par == 0 else jnp.where(first, 0.0, qc)
            slabs.append(to_kv_half(own, par, g))
        s = _dot_nt(jnp.concatenate(slabs, axis=0).astype(BF16), keys)
        pr, inv = _softmax_sink(s, mask, sink_col)
        out = _dot(pr, vals) * inv
        for c in range(D_MODEL // LANES):
            g = (2 * c) // Q_PER_KV
            even = to_kv_half(out[(2 * c) * steps:(2 * c + 1) * steps], g, 0)
            odd = to_kv_half(out[(2 * c + 1) * steps:(2 * c + 2) * steps], g, 1)
            o_s[pl.ds(r0, steps), c * LANES:(c + 1) * LANES] = jnp.where(first, even, odd)

    def seq_pair(i, carry):
        for u in range(SAMPLE_SEQ_UNROLL):
            one_seq(i * SAMPLE_SEQ_UNROLL + u)
        return carry

    lax.fori_loop(0, rows // steps // SAMPLE_SEQ_UNROLL, seq_pair, 0)
    x4_ref[...] = x3 + _dot(o_s[...].astype(BF16), wo_ref[...])


def _attn_sample(x3, knew, vnew, kcache, vcache, w, rope, steps):
    n, d = x3.shape
    sb = SAMPLE_SEQ_BLOCK
    rows = sb * steps
    tok = lambda i: (i, 0)
    const2 = lambda i: (0, 0)
    return pl.pallas_call(
        _attn_sample_kernel,
        grid=(n // rows,),
        in_specs=[
            pl.BlockSpec((rows, d), tok),
            pl.BlockSpec((rows, LANES), tok),
            pl.BlockSpec((rows, LANES), tok),
            pl.BlockSpec((sb, WINDOW, LANES), lambda i: (i, 0, 0)),
            pl.BlockSpec((sb, WINDOW, LANES), lambda i: (i, 0, 0)),
            pl.BlockSpec((1, d), const2),
            pl.BlockSpec((d, d), const2),
            pl.BlockSpec((1, LANES), const2),
            pl.BlockSpec((rows, 3 * LANES), tok),
            pl.BlockSpec(memory_space=pltpu.SMEM),
            pl.BlockSpec((d, d), const2),
        ],
        out_specs=pl.BlockSpec((rows, d), tok),
        out_shape=jax.ShapeDtypeStruct((n, d), F32),
        scratch_shapes=[
            pltpu.VMEM((rows, d), F32),
            pltpu.VMEM((rows, d), F32),
        ],
        compiler_params=pltpu.CompilerParams(
            dimension_semantics=("arbitrary",), vmem_limit_bytes=VMEM_LIMIT),
        name="attn_sample",
    )(x3, knew, vnew, kcache, vcache, w["norm_mix1"], w["w_q"], w["q_norm"], rope,
      w["sinks"], w["w_o"])


def _rope_table(pos):
    half = ROPE_DIM // 2
    dim = jnp.arange(LANES, dtype=jnp.int32) % HEAD_DIM
    inv = jnp.float32(ROPE_THETA) ** (-((dim % half).astype(jnp.float32) * 2.0 / ROPE_DIM))
    inv = jnp.where(dim < ROPE_DIM, inv, 0.0)
    ang = pos.astype(jnp.float32)[:, None] * inv[None, :]
    cos, sin = jnp.cos(ang), jnp.sin(ang)
    s_next = jnp.where(dim < half, -sin, 0.0)
    s_prev = jnp.where((dim >= half) & (dim < ROPE_DIM), sin, 0.0)
    return jnp.concatenate([cos, s_next, s_prev], axis=1)


def kernel(x_prompt, x_sample, p_prompt, p_sample, state_pool, cache_k_win, cache_v_win, norm_mix, norm_ffn, norm_ple, pool_w, pool_scale, kv_norm, w_kv, k_norm, w_q, q_norm, sinks, w_o, router_g_w, router_g_b, router_e_w, router_e_b, exp_gate, exp_up, exp_down, ple_gate, ple_proj):
    b, seq, d = x_prompt.shape
    sb, steps, _ = x_sample.shape
    depth = norm_mix.shape[0]
    row = lambda v: v.reshape(1, -1)

    gap_w = jnp.zeros((depth, d, EXPERT_LANE0 - N_GROUPS), F32)
    tail_w = jnp.zeros((depth, d, LANES - EXPERT_LANE0 - N_EXPERTS), F32)
    router_w = jnp.concatenate(
        [router_g_w, gap_w, router_e_w.reshape(depth, d, N_EXPERTS), tail_w], axis=2)
    router_b = jnp.concatenate(
        [router_g_b, gap_w[:, 0], router_e_b.reshape(depth, N_EXPERTS), tail_w[:, 0]], axis=1)
    moe_w = {
        "norm_ffn": norm_ffn.reshape(depth, 1, d), "router_w": router_w,
        "router_b": router_b.reshape(depth, 1, LANES),
        "exp_gate": exp_gate.astype(BF16), "exp_up": exp_up.astype(BF16),
        "exp_down": exp_down.astype(BF16), "norm_ple": norm_ple.reshape(depth, 1, d),
        "ple_gate": ple_gate.astype(BF16), "ple_proj": ple_proj.astype(BF16),
    }
    two = lambda a: jnp.concatenate([a, a])
    attn_w = {
        "norm_mix1": row(norm_mix[1]), "w_q": w_q[0].astype(BF16), "q_norm": row(two(q_norm[0])),
        "sinks": sinks[0], "w_o": w_o[0].astype(BF16),
    }
    pw = pool_w[0]
    nmix0 = row(norm_mix[0])
    pscale = row(pool_scale[0])

    rope_p = _rope_table(jnp.arange(seq, dtype=jnp.int32))
    rope_s = jnp.tile(_rope_table(PAST_LEN + jnp.arange(steps, dtype=jnp.int32)), (sb, 1))
    kv_w = {"kv_norm": row(kv_norm), "w_kv": w_kv.astype(BF16), "k_norm": row(two(k_norm))}

    pp_all = p_prompt.reshape(depth, b * seq, D_PLE)
    x1p, pool16 = _mixer0_prompt(x_prompt, nmix0, pw, pscale)
    x3p, kp, vp = _moe_layer(x1p.reshape(b * seq, d), pp_all, 0, moe_w, dict(kv_w, rope=rope_p))
    kp = kp.reshape(b, seq, LANES)
    vp = vp.reshape(b, seq, LANES)
    x4p = _attn_prompt(x3p.reshape(b, seq, d), kp, vp, attn_w, rope_p)
    y_prompt = _moe_layer(x4p.reshape(b * seq, d), pp_all, 1, moe_w)[0].reshape(b, seq, d)
    pool_prompt = pool16[None, :, CTX_ROWS - POOL_CTX:, :]
    k_win_prompt = kp[:, seq - WINDOW:].reshape(b, WINDOW, N_KV_HEADS, HEAD_DIM)
    v_win_prompt = vp[:, seq - WINDOW:].reshape(b, WINDOW, N_KV_HEADS, HEAD_DIM)

    ps_all = p_sample.reshape(depth, sb * steps, D_PLE)
    x1s, pool_s = _mixer0_sample(x_sample, state_pool, nmix0, pw, pscale)
    x3s, ks, vs = _moe_layer(x1s.reshape(sb * steps, d), ps_all, 0, moe_w, dict(kv_w, rope=rope_s))
    kc = cache_k_win.reshape(sb, WINDOW, LANES)
    vc = cache_v_win.reshape(sb, WINDOW, LANES)
    x4s = _attn_sample(x3s, ks, vs, kc, vc, attn_w, rope_s, steps)
    y_sample = _moe_layer(x4s, ps_all, 1, moe_w)[0].reshape(sb, steps, d)
    pool_sample = pool_s[None]
    heads = (sb, steps, N_KV_HEADS, HEAD_DIM)
    k_win_sample = jnp.concatenate([cache_k_win[:, steps:], ks.reshape(heads)], axis=1)
    v_win_sample = jnp.concatenate([cache_v_win[:, steps:], vs.reshape(heads)], axis=1)

    return (y_prompt, y_sample, pool_prompt, pool_sample,
            k_win_prompt, v_win_prompt, k_win_sample, v_win_sample)
```

```python
import functools

import jax
import jax.numpy as jnp
from jax import lax
from jax.experimental import pallas as pl
from jax.experimental.pallas import tpu as pltpu

D_MODEL = 1024
PAST_LEN = 8192
POOL_WINDOWS = (2, 4, 8, 16)
POOL_GROUP_DIM = D_MODEL // len(POOL_WINDOWS)
POOL_CTX = max(POOL_WINDOWS) - 1
assert all(win == 2 ** (g + 1) for g, win in enumerate(POOL_WINDOWS))
HEAD_DIM = 64
N_HEADS = D_MODEL // HEAD_DIM
N_KV_HEADS = 2
Q_PER_KV = N_HEADS // N_KV_HEADS
WINDOW = 128
ROPE_DIM = HEAD_DIM // 4
ROPE_THETA = 500000.0
N_GROUPS = 4
E_PER_GROUP = 4
N_EXPERTS = N_GROUPS * E_PER_GROUP
D_EXPERT = 256
D_PLE = 256
EPS = 1e-6

LANES = 128
SUBLANES = 8
CTX_ROWS = 16
TOKEN_TILE = 1024
ATTN_TILE = 1024
ATTN_BLOCK_UNROLL = 4
SAMPLE_SEQ_BLOCK = 16
SAMPLE_SEQ_UNROLL = 8
VMEM_LIMIT = 60 * 1024 * 1024
NEG_BIG = -1e30
LOG2_E = 1.4426950408889634
F32 = jnp.float32
BF16 = jnp.bfloat16

EXPERT_LANE0 = SUBLANES
MOE_SUBTILE = 256
SEG_ALIGN = 2 * SUBLANES
SORT_ROWS = 768
EXPERT_FIXED = 192
EXPERT_STEP = 32
EXPERTS_PER_STEP = 4
SORT_BLOCKS = SORT_ROWS // SEG_ALIGN
MOE_SUBTILES = TOKEN_TILE // MOE_SUBTILE
XE_DUMP_ROW = (2 * TOKEN_TILE + MOE_SUBTILES * N_EXPERTS * (SEG_ALIGN - 1)
               + N_EXPERTS * (EXPERT_STEP - SEG_ALIGN + EXPERT_FIXED))
META_BASE, META_ROWS = 64, 65
assert SORT_ROWS >= 2 * MOE_SUBTILE + N_EXPERTS * (SEG_ALIGN - 1) and SORT_ROWS % LANES == 0
assert TOKEN_TILE % MOE_SUBTILE == 0 and MOE_SUBTILES <= N_EXPERTS and SORT_BLOCKS <= META_BASE
assert EXPERT_FIXED % EXPERT_STEP == 0 and EXPERT_STEP % SEG_ALIGN == 0
assert N_EXPERTS % EXPERTS_PER_STEP == 0 and XE_DUMP_ROW % SEG_ALIGN == 0


def _rms(x, g):
    return x * lax.rsqrt(jnp.mean(x * x, axis=-1, keepdims=True) + EPS) * g


def _dot(a, b):
    return jnp.dot(a, b, preferred_element_type=F32)


def _dot_nt(a, b):
    return lax.dot_general(a, b, (((1,), (1,)), ((), ())), preferred_element_type=F32)


def _dot_tn(a, b):
    return lax.dot_general(a, b, (((0,), (0,)), ((), ())), preferred_element_type=F32)


def _split_bf16(x):
    hi = x.astype(BF16)
    return hi, (x - hi.astype(F32)).astype(BF16)


def _dot3(a_hi, a_lo, b):
    b_hi, b_lo = _split_bf16(b)
    return _dot(a_hi, b_hi) + _dot(a_lo, b_hi) + _dot(a_hi, b_lo)


def _sigmoid(x):
    return 1.0 / (1.0 + jnp.exp(-x))


def _rope_parts(rope_ref):
    return rope_ref[:, 0:LANES], rope_ref[:, LANES:2 * LANES], rope_ref[:, 2 * LANES:3 * LANES]


def _same_head():
    hi = lax.broadcasted_iota(jnp.int32, (LANES, LANES), 0) // HEAD_DIM
    hj = lax.broadcasted_iota(jnp.int32, (LANES, LANES), 1) // HEAD_DIM
    return jnp.where(hi == hj, 1.0, 0.0).astype(BF16)


def _head_norm_rope(x, gain, same_head, cos, sin_next, sin_prev, two_piece):
    sq_hi, sq_lo = _split_bf16(x * x)
    ssq = _dot(sq_hi, same_head)
    if two_piece:
        ssq = ssq + _dot(sq_lo, same_head)
    y = x * lax.rsqrt(ssq * (1.0 / HEAD_DIM) + EPS) * gain
    half = ROPE_DIM // 2
    return (y * cos + pltpu.roll(y, LANES - half, 1) * sin_next
            + pltpu.roll(y, half, 1) * sin_prev)


def _pool_mix(h, window_sum, pos, x, pw_ref, pscale_ref, store):
    for g, win in enumerate(POOL_WINDOWS):
        cols = slice(g * POOL_GROUP_DIM, (g + 1) * POOL_GROUP_DIM)
        hg = h[..., cols]
        inv_cnt = 1.0 / jnp.minimum(win, pos + 1).astype(F32)
        pooled = (window_sum(g, cols) * inv_cnt - hg).reshape(-1, POOL_GROUP_DIM)
        hi, lo = _split_bf16(pooled)
        mixed = _dot3(hi, lo, pw_ref[g])
        store(cols, x[..., cols] + mixed.reshape(hg.shape) * pscale_ref[:, cols])


def _mixer0_prompt_kernel(x_ref, xprev_ref, nmix_ref, pw_ref, pscale_ref,
                          x1_ref, pool_ref, hs_ref, ua_ref, ub_ref):
    t = pl.program_id(1)
    tm = x_ref.shape[1]
    x = x_ref[0]
    h = _rms(x, nmix_ref[...])
    hp = _rms(xprev_ref[0], nmix_ref[...])
    top = 2 * CTX_ROWS
    hs_ref[0:CTX_ROWS, :] = jnp.zeros((CTX_ROWS, x.shape[1]), F32)
    hs_ref[CTX_ROWS:top, :] = jnp.where(t > 0, hp, 0.0)
    hs_ref[top:, :] = h
    ua_ref[0:SUBLANES, :] = jnp.zeros((SUBLANES, x.shape[1]), F32)
    ub_ref[0:SUBLANES, :] = jnp.zeros((SUBLANES, x.shape[1]), F32)

    n = tm + top - SUBLANES
    src, level_of = hs_ref, []
    for g, win in enumerate(POOL_WINDOWS):
        dst = ua_ref if g % 2 == 0 else ub_ref
        c0 = g * POOL_GROUP_DIM
        back = win // 2
        dst[SUBLANES:, c0:] = src[SUBLANES:, c0:] + src[SUBLANES - back:SUBLANES - back + n, c0:]
        level_of.append(dst)
        src = dst

    def window_sum(g, cols):
        return level_of[g][top:, cols]

    def store(cols, val):
        x1_ref[0, :, cols] = val

    pos = t * tm + lax.broadcasted_iota(jnp.int32, (tm, 1), 0)
    _pool_mix(h, window_sum, pos, x, pw_ref, pscale_ref, store)

    @pl.when(t == pl.num_programs(1) - 1)
    def _():
        pool_ref[0] = hs_ref[tm + top - CTX_ROWS:tm + top, :]


def _mixer0_sample_kernel(x_ref, ctx_ref, nmix_ref, pw_ref, pscale_ref,
                          x1_ref, pool_ref, hs_ref):
    x = x_ref[...]
    steps = x.shape[1]
    h = _rms(x, nmix_ref[...])
    hs_ref[:, CTX_ROWS - POOL_CTX:CTX_ROWS, :] = ctx_ref[...]
    hs_ref[:, CTX_ROWS:, :] = h
    pool_ref[:, 0:POOL_CTX - steps, :] = ctx_ref[:, steps:, :]
    pool_ref[:, POOL_CTX - steps:, :] = h

    def window_sum(g, cols):
        s = h[..., cols]
        for j in range(1, POOL_WINDOWS[g]):
            s = s + hs_ref[:, CTX_ROWS - j:CTX_ROWS - j + steps, cols]
        return s

    def store(cols, val):
        x1_ref[:, :, cols] = val

    pos = PAST_LEN + lax.broadcasted_iota(jnp.int32, (1, steps, 1), 1)
    _pool_mix(h, window_sum, pos, x, pw_ref, pscale_ref, store)


def _mixer0_prompt(x, nmix, pw, pscale):
    b, seq, d = x.shape
    tm = TOKEN_TILE
    nt = seq // tm
    per = tm // CTX_ROWS
    const2 = lambda bi, ti: (0, 0)
    return pl.pallas_call(
        _mixer0_prompt_kernel,
        grid=(b, nt),
        in_specs=[
            pl.BlockSpec((1, tm, d), lambda bi, ti: (bi, ti, 0)),
            pl.BlockSpec((1, CTX_ROWS, d), lambda bi, ti: (bi, jnp.maximum(ti * per - 1, 0), 0)),
            pl.BlockSpec((1, d), const2),
            pl.BlockSpec(pw.shape, lambda bi, ti: (0, 0, 0)),
            pl.BlockSpec((1, d), const2),
        ],
        out_specs=[
            pl.BlockSpec((1, tm, d), lambda bi, ti: (bi, ti, 0)),
            pl.BlockSpec((1, CTX_ROWS, d), lambda bi, ti: (bi, 0, 0)),
        ],
        out_shape=[
            jax.ShapeDtypeStruct((b, seq, d), F32),
            jax.ShapeDtypeStruct((b, CTX_ROWS, d), F32),
        ],
        scratch_shapes=[pltpu.VMEM((tm + 2 * CTX_ROWS, d), F32)] * 3,
        compiler_params=pltpu.CompilerParams(
            dimension_semantics=("arbitrary", "arbitrary"), vmem_limit_bytes=VMEM_LIMIT),
        name="mixer0_prompt",
    )(x, x, nmix, pw, pscale)


def _mixer0_sample(x, ctx_layers, nmix, pw, pscale):
    b, steps, d = x.shape
    sb = SAMPLE_SEQ_BLOCK
    const2 = lambda i: (0, 0)
    return pl.pallas_call(
        _mixer0_sample_kernel,
        grid=(b // sb,),
        in_specs=[
            pl.BlockSpec((sb, steps, d), lambda i: (i, 0, 0)),
            pl.BlockSpec((None, sb, POOL_CTX, d), lambda i: (0, i, 0, 0)),
            pl.BlockSpec((1, d), const2),
            pl.BlockSpec(pw.shape, lambda i: (0, 0, 0)),
            pl.BlockSpec((1, d), const2),
        ],
        out_specs=[
            pl.BlockSpec((sb, steps, d), lambda i: (i, 0, 0)),
            pl.BlockSpec((sb, POOL_CTX, d), lambda i: (i, 0, 0)),
        ],
        out_shape=[
            jax.ShapeDtypeStruct((b, steps, d), F32),
            jax.ShapeDtypeStruct((b, POOL_CTX, d), F32),
        ],
        scratch_shapes=[pltpu.VMEM((sb, CTX_ROWS + steps, d), F32)],
        compiler_params=pltpu.CompilerParams(
            dimension_semantics=("arbitrary",), vmem_limit_bytes=VMEM_LIMIT),
        name="mixer0_sample",
    )(x, ctx_layers, nmix, pw, pscale)


def _route_rows(glog, elog):
    sub8 = lax.broadcasted_iota(jnp.int32, glog.shape, 0).astype(F32)
    is_grp = sub8 < N_GROUPS
    gmax = jnp.max(jnp.where(is_grp, glog, NEG_BIG), axis=0, keepdims=True)
    gsum = jnp.sum(jnp.where(is_grp, jnp.exp(jnp.minimum(glog - gmax, 0.0)), 0.0),
                   axis=0, keepdims=True)
    g_w = 1.0 / gsum
    g_sel = jnp.min(jnp.where(is_grp & (glog == gmax), sub8, float(N_GROUPS)),
                    axis=0, keepdims=True)
    sub16 = lax.broadcasted_iota(jnp.int32, elog.shape, 0).astype(F32)
    lo = E_PER_GROUP * g_sel
    in_grp = (sub16 >= lo) & (sub16 < lo + E_PER_GROUP)
    none = float(N_EXPERTS)
    v1 = jnp.max(jnp.where(in_grp, elog, NEG_BIG), axis=0, keepdims=True)
    i1 = jnp.min(jnp.where(in_grp & (elog == v1), sub16, none), axis=0, keepdims=True)
    rest = in_grp & (sub16 != i1)
    v2 = jnp.max(jnp.where(rest, elog, NEG_BIG), axis=0, keepdims=True)
    i2 = jnp.min(jnp.where(rest & (elog == v2), sub16, none), axis=0, keepdims=True)
    e2 = jnp.exp(v2 - v1)
    den = 1.0 + e2
    return i1, i2, (1.0 / den) * g_w, (e2 / den) * g_w


def _dispatch(x1_ref, nffn_ref, rw_ref, rb_ref,
              xe_s, perm_s, gsort_s, meta_v, meta_sm, sem):
    tm = x1_ref.shape[0]
    sub = MOE_SUBTILE
    nsub = tm // sub
    hi, lo = _split_bf16(_rms(x1_ref[...], nffn_ref[...]))
    w_hi, w_lo = _split_bf16(rw_ref[...])
    wide = _dot(hi, jnp.concatenate([w_hi, w_lo], axis=1))
    logits = wide[:, :LANES] + wide[:, LANES:] + _dot(lo, w_hi) + rb_ref[...]
    lt = logits.T
    i1, i2, w1, w2 = _route_rows(lt[0:SUBLANES], lt[EXPERT_LANE0:EXPERT_LANE0 + N_EXPERTS])

    sub16 = lax.broadcasted_iota(jnp.int32, (N_EXPERTS, 2 * sub), 0).astype(F32)
    esub = lax.broadcasted_iota(jnp.int32, (N_EXPERTS, LANES), 0)
    mlane = lax.broadcasted_iota(jnp.int32, (N_EXPERTS, LANES), 1)

    def round_up(x, step):
        return jnp.floor((x + (step - 1)) * (1.0 / step)) * step

    def starts(sizes):
        inc = sizes
        for k in (1, 2, 4, 8):
            inc = inc + jnp.where(esub >= k, pltpu.roll(inc, k, 0), 0.0)
        return inc - sizes

    onehots, sizes = [], []
    for s in range(nsub):
        sl = slice(s * sub, (s + 1) * sub)
        sel = jnp.concatenate([i1[:, sl], i2[:, sl]], axis=1)
        at = jnp.where(sub16 == sel, 1.0, 0.0)
        cnt = jnp.sum(at, axis=1, keepdims=True)
        onehots.append(at)
        sizes.append(jnp.broadcast_to(round_up(cnt, SEG_ALIGN), (N_EXPERTS, LANES)))

    region = jnp.maximum(round_up(sum(sizes), EXPERT_STEP), float(EXPERT_FIXED))
    base = starts(region)
    meta = jnp.where(mlane == META_BASE, base, jnp.where(mlane == META_ROWS, region, 0.0))
    blk0 = mlane.astype(F32) * SEG_ALIGN
    seg_starts, at_row = [], base
    for s in range(nsub):
        seg0 = starts(sizes[s])
        seg_starts.append(seg0)
        inside = (blk0 >= seg0) & (blk0 < seg0 + sizes[s])
        dst = jnp.sum(jnp.where(inside, at_row - seg0 + blk0, 0.0), axis=0, keepdims=True)
        used = jnp.sum(jnp.where(inside, 1.0, 0.0), axis=0, keepdims=True)
        dst = jnp.where(used > 0.0, dst, float(XE_DUMP_ROW))
        meta = jnp.where((esub == s) & (mlane < SORT_BLOCKS), dst, meta)
        at_row = at_row + sizes[s]
    meta_v[...] = meta.astype(jnp.int32)
    cp = pltpu.make_async_copy(meta_v, meta_sm, sem)
    cp.start()

    jr = lax.broadcasted_iota(jnp.int32, (2 * sub, 2 * sub), 0)
    jc = lax.broadcasted_iota(jnp.int32, (2 * sub, 2 * sub), 1)
    earlier = jnp.where(jr < jc, 1.0, 0.0).astype(BF16)
    rsub = lax.broadcasted_iota(jnp.int32, (SORT_ROWS, sub), 0).astype(F32)
    for s in range(nsub):
        sl = slice(s * sub, (s + 1) * sub)
        at = onehots[s]
        rank = _dot(at.astype(BF16), earlier)
        dest = jnp.sum(at * (seg_starts[s][:, 0:1] + rank), axis=0, keepdims=True)
        p1 = rsub == dest[:, :sub]
        p2 = rsub == dest[:, sub:]
        perm = jnp.where(p1 | p2, 1.0, 0.0).astype(BF16)
        gsort_s[s] = jnp.sum(jnp.where(p1, w1[:, sl], 0.0) + jnp.where(p2, w2[:, sl], 0.0),
                             axis=1, keepdims=True)
        perm_s[s] = perm
        xs = _dot(perm, hi[sl, :]).astype(BF16)
        if s == 0:
            cp.wait()
        for b in range(SORT_BLOCKS):
            row = pl.multiple_of(meta_sm[s, b], SEG_ALIGN)
            xe_s[pl.ds(row, SEG_ALIGN), :] = xs[b * SEG_ALIGN:(b + 1) * SEG_ALIGN, :]


def _expert_group(first, xe_s, act_s, meta_sm, wg_ref, wu_ref, wd_ref):
    def hidden(j, r0, m):
        x = xe_s[pl.ds(r0, m), :]
        gt = _dot(x, wg_ref[j])
        up = _dot(x, wu_ref[j])
        return ((gt * _sigmoid(gt)) * up).astype(BF16)

    bases = [pl.multiple_of(meta_sm[first + j, META_BASE], EXPERT_STEP)
             for j in range(EXPERTS_PER_STEP)]
    for j in range(EXPERTS_PER_STEP):
        act_s[j] = hidden(j, bases[j], EXPERT_FIXED)
    for j in range(EXPERTS_PER_STEP):
        xe_s[pl.ds(bases[j], EXPERT_FIXED), :] = _dot(act_s[j], wd_ref[j]).astype(BF16)
    for j in range(EXPERTS_PER_STEP):
        def piece(c, carry, j=j):
            r0 = pl.multiple_of(bases[j] + EXPERT_FIXED + c * EXPERT_STEP, EXPERT_STEP)
            xe_s[pl.ds(r0, EXPERT_STEP), :] = _dot(hidden(j, r0, EXPERT_STEP), wd_ref[j]).astype(BF16)
            return carry
        lax.fori_loop(0, (meta_sm[first + j, META_ROWS] - EXPERT_FIXED) // EXPERT_STEP, piece, 0)


def _moe_kernel(with_kv, *refs):
    if with_kv:
        (x1_ref, p_ref, nffn_ref, rw_ref, rb_ref, wg_ref, wu_ref, wd_ref,
         nple_ref, pg_ref, pp_ref, kvn_ref, wkv_ref, kn_ref, rope_ref,
         out_ref, k_ref, v_ref, xe_s, ys_s, act_s, perm_s, gsort_s, meta_v, meta_sm, sem) = refs
    else:
        (x1_ref, p_ref, nffn_ref, rw_ref, rb_ref, wg_ref, wu_ref, wd_ref,
         nple_ref, pg_ref, pp_ref,
         out_ref, xe_s, ys_s, act_s, perm_s, gsort_s, meta_v, meta_sm, sem) = refs
    i = pl.program_id(0)
    step = pl.program_id(1)
    nsub = x1_ref.shape[0] // MOE_SUBTILE

    @pl.when((i == 0) & (step == 0))
    def _():
        xe_s[...] = jnp.zeros_like(xe_s)

    @pl.when(step == 0)
    def _():
        _dispatch(x1_ref, nffn_ref, rw_ref, rb_ref,
                  xe_s, perm_s, gsort_s, meta_v, meta_sm, sem)

    _expert_group(step * EXPERTS_PER_STEP, xe_s, act_s, meta_sm, wg_ref, wu_ref, wd_ref)

    @pl.when(step == pl.num_programs(1) - 1)
    def _():
        for s in range(nsub):
            sl = slice(s * MOE_SUBTILE, (s + 1) * MOE_SUBTILE)
            for b in range(SORT_BLOCKS):
                rows = slice(b * SEG_ALIGN, (b + 1) * SEG_ALIGN)
                src = pl.multiple_of(meta_sm[s, b], SEG_ALIGN)
                blk = xe_s[pl.ds(src, SEG_ALIGN), :].astype(F32) * gsort_s[s, rows, :]
                ys_s[rows, :] = blk.astype(BF16)
            out_ref[sl, :] = x1_ref[sl, :] + _dot_tn(perm_s[s], ys_s[...])
        x2 = out_ref[...]
        hp = _rms(x2, nple_ref[...]).astype(BF16)
        gate = _sigmoid(_dot(hp, pg_ref[...]))
        proj = _dot(p_ref[...].astype(BF16), pp_ref[...])
        x3 = x2 + gate * proj
        out_ref[...] = x3
        if with_kv:
            hk = _rms(x3, kvn_ref[...]).astype(BF16)
            kv = _dot(hk, wkv_ref[...])
            k_ref[...] = _head_norm_rope(kv[:, :LANES], kn_ref[...], _same_head(),
                                         *_rope_parts(rope_ref), two_piece=True)
            v_ref[...] = kv[:, LANES:]


def _moe_layer(x1, p_all, layer, w, kv=None):
    n, d = x1.shape
    tm = TOKEN_TILE
    nt = n // tm
    nsub = tm // MOE_SUBTILE
    tok = lambda i, e: (i, 0)
    const2 = lambda i, e: (0, 0)
    lay3 = lambda i, e: (layer, 0, 0)
    exp4 = lambda i, e: (layer, e, 0, 0)
    in_specs = [
        pl.BlockSpec((tm, d), tok),
        pl.BlockSpec((None, tm, D_PLE), lambda i, e: (layer, i, 0)),
        pl.BlockSpec((None, 1, d), lay3),
        pl.BlockSpec((None, d, LANES), lay3),
        pl.BlockSpec((None, 1, LANES), lay3),
        pl.BlockSpec((None, EXPERTS_PER_STEP, d, D_EXPERT), exp4),
        pl.BlockSpec((None, EXPERTS_PER_STEP, d, D_EXPERT), exp4),
        pl.BlockSpec((None, EXPERTS_PER_STEP, D_EXPERT, d), exp4),
        pl.BlockSpec((None, 1, d), lay3),
        pl.BlockSpec((None, d, d), lay3),
        pl.BlockSpec((None, D_PLE, d), lay3),
    ]
    args = [x1, p_all, w["norm_ffn"], w["router_w"], w["router_b"],
            w["exp_gate"], w["exp_up"], w["exp_down"], w["norm_ple"], w["ple_gate"], w["ple_proj"]]
    out_specs = [pl.BlockSpec((tm, d), tok)]
    out_shape = [jax.ShapeDtypeStruct((n, d), F32)]
    if kv is not None:
        tab_blocks = kv["rope"].shape[0] // tm
        in_specs += [
            pl.BlockSpec((1, d), const2),
            pl.BlockSpec((d, 2 * LANES), const2),
            pl.BlockSpec((1, LANES), const2),
            pl.BlockSpec((tm, 3 * LANES), lambda i, e: (i % tab_blocks, 0)),
        ]
        args += [kv["kv_norm"], kv["w_kv"], kv["k_norm"], kv["rope"]]
        out_specs += [pl.BlockSpec((tm, LANES), tok), pl.BlockSpec((tm, LANES), tok)]
        out_shape += [jax.ShapeDtypeStruct((n, LANES), F32), jax.ShapeDtypeStruct((n, LANES), F32)]
    return pl.pallas_call(
        functools.partial(_moe_kernel, kv is not None),
        grid=(nt, N_EXPERTS // EXPERTS_PER_STEP),
        in_specs=in_specs,
        out_specs=out_specs,
        out_shape=out_shape,
        scratch_shapes=[
            pltpu.VMEM((XE_DUMP_ROW + SEG_ALIGN, d), BF16),
            pltpu.VMEM((SORT_ROWS, d), BF16),
            pltpu.VMEM((EXPERTS_PER_STEP, EXPERT_FIXED, D_EXPERT), BF16),
            pltpu.VMEM((nsub, SORT_ROWS, MOE_SUBTILE), BF16),
            pltpu.VMEM((nsub, SORT_ROWS, 1), F32),
            pltpu.VMEM((N_EXPERTS, LANES), jnp.int32),
            pltpu.SMEM((N_EXPERTS, LANES), jnp.int32),
            pltpu.SemaphoreType.DMA,
        ],
        compiler_params=pltpu.CompilerParams(
            dimension_semantics=("arbitrary", "arbitrary"), vmem_limit_bytes=VMEM_LIMIT),
        name="moe_kv" if kv is not None else "moe",
    )(*args)


def _project_q(x3, nmix_ref, wq_ref, qn_ref, cos, sin_next, sin_prev, q_s):
    h = _rms(x3, nmix_ref[...]).astype(BF16)
    same_head = _same_head()
    wide = 4 * LANES
    for blk in range(D_MODEL // wide):
        q = _dot(h, wq_ref[:, blk * wide:(blk + 1) * wide])
        for c in range(wide // LANES):
            cols = slice(blk * wide + c * LANES, blk * wide + (c + 1) * LANES)
            qc = _head_norm_rope(q[:, c * LANES:(c + 1) * LANES], qn_ref[...], same_head,
                                 cos, sin_next, sin_prev, two_piece=False)
            q_s[:, cols] = (qc * (HEAD_DIM ** -0.5 * LOG2_E)).astype(q_s.dtype)


def _both_orders(x):
    return x.astype(BF16), pltpu.roll(x, HEAD_DIM, 1).astype(BF16)


def _kv_order(par, g):
    return 0 if par == g else 1


def _softmax_sink(s, mask, sink):
    col0 = lax.broadcasted_iota(jnp.int32, (s.shape[0], LANES), 1) == 0
    s = jnp.where(mask, s, NEG_BIG)
    s = jnp.concatenate([jnp.where(col0, sink * LOG2_E, s[:, :LANES]), s[:, LANES:]], axis=1)
    p = jnp.exp2(s - jnp.max(s, axis=-1, keepdims=True))
    return p.astype(BF16), 1.0 / jnp.sum(p, axis=-1, keepdims=True)


def _drop_first_row(vals, tile_rows):
    row = lax.broadcasted_iota(jnp.int32, (tile_rows, vals.shape[1]), 0)
    head = jnp.where(row == 0, jnp.zeros_like(vals[:tile_rows]), vals[:tile_rows])
    return jnp.concatenate([head, vals[tile_rows:]], axis=0)


def _attn_prompt_kernel(x3_ref, k_ref, v_ref, kprev_ref, vprev_ref, nmix_ref, wq_ref, qn_ref,
                        rope_ref, sinks_ref, wo_ref,
                        x4_ref, q_s, o_s, kd_s, vd_s):
    t = pl.program_id(1)
    tq = x3_ref.shape[1]
    x3 = x3_ref[0]
    _project_q(x3, nmix_ref, wq_ref, qn_ref, *_rope_parts(rope_ref), q_s)

    for src_prev, src, dst in ((kprev_ref, k_ref, kd_s), (vprev_ref, v_ref, vd_s)):
        dst[0, 0:WINDOW, :], dst[1, 0:WINDOW, :] = _both_orders(src_prev[0])
        dst[0, WINDOW:, :], dst[1, WINDOW:, :] = _both_orders(src[0])

    qi = lax.broadcasted_iota(jnp.int32, (WINDOW, 2 * WINDOW), 0)
    ki = lax.broadcasted_iota(jnp.int32, (WINDOW, 2 * WINDOW), 1)
    band = (ki > qi) & (ki <= qi + WINDOW)
    lane = lax.broadcasted_iota(jnp.int32, (WINDOW, LANES), 1)
    first = lane < HEAD_DIM

    def q_block(j, carry):
        r0 = pl.multiple_of(j * WINDOW, WINDOW)
        mask = band & (ki >= (1 - j) * WINDOW - t * tq)
        vals = [_drop_first_row(vd_s[order, pl.ds(r0, 2 * WINDOW), :], SEG_ALIGN)
                for order in range(2)]
        for c in range(D_MODEL // LANES):
            g = (2 * c) // Q_PER_KV
            cols = slice(c * LANES, (c + 1) * LANES)
            qc = q_s[pl.ds(r0, WINDOW), cols]
            outs = []
            for par in range(2):
                order = _kv_order(par, g)
                qh = jnp.where(first, qc, 0) if par == 0 else jnp.where(first, 0, qc)
                s = _dot_nt(qh, kd_s[order, pl.ds(r0, 2 * WINDOW), :])
                pr, inv = _softmax_sink(s, mask, sinks_ref[2 * c + par])
                outs.append(_dot(pr, vals[order]) * inv)
            o_s[pl.ds(r0, WINDOW), cols] = jnp.where(first, outs[0], outs[1]).astype(BF16)
        return carry

    lax.fori_loop(0, tq // WINDOW, q_block, 0, unroll=ATTN_BLOCK_UNROLL)
    x4_ref[0] = x3 + _dot(o_s[...], wo_ref[...])


def _attn_prompt(x3, k, v, w, rope):
    b, seq, d = x3.shape
    tq = ATTN_TILE
    nt = seq // tq
    per = tq // WINDOW
    tile = lambda bi, ti: (bi, ti, 0)
    prev = lambda bi, ti: (bi, jnp.maximum(ti * per - 1, 0), 0)
    const2 = lambda bi, ti: (0, 0)
    tab = lambda bi, ti: (ti, 0)
    return pl.pallas_call(
        _attn_prompt_kernel,
        grid=(b, nt),
        in_specs=[
            pl.BlockSpec((1, tq, d), tile),
            pl.BlockSpec((1, tq, LANES), tile),
            pl.BlockSpec((1, tq, LANES), tile),
            pl.BlockSpec((1, WINDOW, LANES), prev),
            pl.BlockSpec((1, WINDOW, LANES), prev),
            pl.BlockSpec((1, d), const2),
            pl.BlockSpec((d, d), const2),
            pl.BlockSpec((1, LANES), const2),
            pl.BlockSpec((tq, 3 * LANES), tab),
            pl.BlockSpec(memory_space=pltpu.SMEM),
            pl.BlockSpec((d, d), const2),
        ],
        out_specs=pl.BlockSpec((1, tq, d), tile),
        out_shape=jax.ShapeDtypeStruct((b, seq, d), F32),
        scratch_shapes=[
            pltpu.VMEM((tq, d), BF16),
            pltpu.VMEM((tq, d), BF16),
            pltpu.VMEM((N_KV_HEADS, tq + WINDOW, LANES), BF16),
            pltpu.VMEM((N_KV_HEADS, tq + WINDOW, LANES), BF16),
        ],
        compiler_params=pltpu.CompilerParams(
            dimension_semantics=("arbitrary", "arbitrary"), vmem_limit_bytes=VMEM_LIMIT),
        name="attn_prompt",
    )(x3, k, v, k, v, w["norm_mix1"], w["w_q"], w["q_norm"], rope, w["sinks"], w["w_o"])


def _attn_sample_kernel(x3_ref, knew_ref, vnew_ref, kc_ref, vc_ref, nmix_ref, wq_ref, qn_ref,
                        rope_ref, sinks_ref, wo_ref,
                        x4_ref, q_s, o_s):
    rows = x3_ref.shape[0]
    steps = knew_ref.shape[0] // kc_ref.shape[0]
    x3 = x3_ref[...]
    _project_q(x3, nmix_ref, wq_ref, qn_ref, *_rope_parts(rope_ref), q_s)

    stack = N_HEADS * steps
    ri = lax.broadcasted_iota(jnp.int32, (stack, 2 * WINDOW), 0)
    ki = lax.broadcasted_iota(jnp.int32, (stack, 2 * WINDOW), 1)
    tq = ri % steps
    mask = ((ki < WINDOW) & (ki > tq)) | ((ki >= WINDOW) & ((ki - WINDOW) <= tq))
    head_of_row = lax.broadcasted_iota(jnp.int32, (stack, 1), 0) // steps
    sink_col = jnp.zeros((stack, 1), F32)
    for hd in range(N_HEADS):
        sink_col = jnp.where(head_of_row == hd, sinks_ref[hd], sink_col)
    lane = lax.broadcasted_iota(jnp.int32, (steps, LANES), 1)
    first = lane < HEAD_DIM
    pad = jnp.zeros((WINDOW - steps, LANES), F32)

    def to_kv_half(slab, par, g):
        return slab if par == g else pltpu.roll(slab, HEAD_DIM, 1)

    def one_seq(b):
        r0 = pl.multiple_of(b * steps, steps)
        keys = jnp.concatenate([kc_ref[b], knew_ref[pl.ds(r0, steps), :], pad], axis=0).astype(BF16)
        vals = jnp.concatenate([_drop_first_row(vc_ref[b], SUBLANES), vnew_ref[pl.ds(r0, steps), :],
                                pad], axis=0).astype(BF16)
        slabs = []
        for hd in range(N_HEADS):
            par, g = hd % 2, hd // Q_PER_KV
            qc = q_s[pl.ds(r0, steps), (hd // 2) * LANES:(hd // 2 + 1) * LANES]
            own = jnp.where(first, qc, 0.0) if par == 0 else jnp.where(first, 0.0, qc)
            slabs.append(to_kv_half(own, par, g))
        s = _dot_nt(jnp.concatenate(slabs, axis=0).astype(BF16), keys)
        pr, inv = _softmax_sink(s, mask, sink_col)
        out = _dot(pr, vals) * inv
        for c in range(D_MODEL // LANES):
            g = (2 * c) // Q_PER_KV
            even = to_kv_half(out[(2 * c) * steps:(2 * c + 1) * steps], g, 0)
            odd = to_kv_half(out[(2 * c + 1) * steps:(2 * c + 2) * steps], g, 1)
            o_s[pl.ds(r0, steps), c * LANES:(c + 1) * LANES] = jnp.where(first, even, odd)

    def seq_pair(i, carry):
        for u in range(SAMPLE_SEQ_UNROLL):
            one_seq(i * SAMPLE_SEQ_UNROLL + u)
        return carry

    lax.fori_loop(0, rows // steps // SAMPLE_SEQ_UNROLL, seq_pair, 0)
    x4_ref[...] = x3 + _dot(o_s[...].astype(BF16), wo_ref[...])


def _attn_sample(x3, knew, vnew, kcache, vcache, w, rope, steps):
    n, d = x3.shape
    sb = SAMPLE_SEQ_BLOCK
    rows = sb * steps
    tok = lambda i: (i, 0)
    const2 = lambda i: (0, 0)
    return pl.pallas_call(
        _attn_sample_kernel,
        grid=(n // rows,),
        in_specs=[
            pl.BlockSpec((rows, d), tok),
            pl.BlockSpec((rows, LANES), tok),
            pl.BlockSpec((rows, LANES), tok),
            pl.BlockSpec((sb, WINDOW, LANES), lambda i: (i, 0, 0)),
            pl.BlockSpec((sb, WINDOW, LANES), lambda i: (i, 0, 0)),
            pl.BlockSpec((1, d), const2),
            pl.BlockSpec((d, d), const2),
            pl.BlockSpec((1, LANES), const2),
            pl.BlockSpec((rows, 3 * LANES), tok),
            pl.BlockSpec(memory_space=pltpu.SMEM),
            pl.BlockSpec((d, d), const2),
        ],
        out_specs=pl.BlockSpec((rows, d), tok),
        out_shape=jax.ShapeDtypeStruct((n, d), F32),
        scratch_shapes=[
            pltpu.VMEM((rows, d), F32),
            pltpu.VMEM((rows, d), F32),
        ],
        compiler_params=pltpu.CompilerParams(
            dimension_semantics=("arbitrary",), vmem_limit_bytes=VMEM_LIMIT),
        name="attn_sample",
    )(x3, knew, vnew, kcache, vcache, w["norm_mix1"], w["w_q"], w["q_norm"], rope,
      w["sinks"], w["w_o"])


def _rope_table(pos):
    half = ROPE_DIM // 2
    dim = jnp.arange(LANES, dtype=jnp.int32) % HEAD_DIM
    inv = jnp.float32(ROPE_THETA) ** (-((dim % half).astype(jnp.float32) * 2.0 / ROPE_DIM))
    inv = jnp.where(dim < ROPE_DIM, inv, 0.0)
    ang = pos.astype(jnp.float32)[:, None] * inv[None, :]
    cos, sin = jnp.cos(ang), jnp.sin(ang)
    s_next = jnp.where(dim < half, -sin, 0.0)
    s_prev = jnp.where((dim >= half) & (dim < ROPE_DIM), sin, 0.0)
    return jnp.concatenate([cos, s_next, s_prev], axis=1)


def kernel(x_prompt, x_sample, p_prompt, p_sample, state_pool, cache_k_win, cache_v_win, norm_mix, norm_ffn, norm_ple, pool_w, pool_scale, kv_norm, w_kv, k_norm, w_q, q_norm, sinks, w_o, router_g_w, router_g_b, router_e_w, router_e_b, exp_gate, exp_up, exp_down, ple_gate, ple_proj):
    b, seq, d = x_prompt.shape
    sb, steps, _ = x_sample.shape
    depth = norm_mix.shape[0]
    row = lambda v: v.reshape(1, -1)

    gap_w = jnp.zeros((depth, d, EXPERT_LANE0 - N_GROUPS), F32)
    tail_w = jnp.zeros((depth, d, LANES - EXPERT_LANE0 - N_EXPERTS), F32)
    router_w = jnp.concatenate(
        [router_g_w, gap_w, router_e_w.reshape(depth, d, N_EXPERTS), tail_w], axis=2)
    router_b = jnp.concatenate(
        [router_g_b, gap_w[:, 0], router_e_b.reshape(depth, N_EXPERTS), tail_w[:, 0]], axis=1)
    moe_w = {
        "norm_ffn": norm_ffn.reshape(depth, 1, d), "router_w": router_w,
        "router_b": router_b.reshape(depth, 1, LANES),
        "exp_gate": exp_gate.astype(BF16), "exp_up": exp_up.astype(BF16),
        "exp_down": exp_down.astype(BF16), "norm_ple": norm_ple.reshape(depth, 1, d),
        "ple_gate": ple_gate.astype(BF16), "ple_proj": ple_proj.astype(BF16),
    }
    two = lambda a: jnp.concatenate([a, a])
    attn_w = {
        "norm_mix1": row(norm_mix[1]), "w_q": w_q[0].astype(BF16), "q_norm": row(two(q_norm[0])),
        "sinks": sinks[0], "w_o": w_o[0].astype(BF16),
    }
    pw = pool_w[0]
    nmix0 = row(norm_mix[0])
    pscale = row(pool_scale[0])

    rope_p = _rope_table(jnp.arange(seq, dtype=jnp.int32))
    rope_s = jnp.tile(_rope_table(PAST_LEN + jnp.arange(steps, dtype=jnp.int32)), (sb, 1))
    kv_w = {"kv_norm": row(kv_norm), "w_kv": w_kv.astype(BF16), "k_norm": row(two(k_norm))}

    pp_all = p_prompt.reshape(depth, b * seq, D_PLE)
    x1p, pool16 = _mixer0_prompt(x_prompt, nmix0, pw, pscale)
    x3p, kp, vp = _moe_layer(x1p.reshape(b * seq, d), pp_all, 0, moe_w, dict(kv_w, rope=rope_p))
    kp = kp.reshape(b, seq, LANES)
    vp = vp.reshape(b, seq, LANES)
    x4p = _attn_prompt(x3p.reshape(b, seq, d), kp, vp, attn_w, rope_p)
    y_prompt = _moe_layer(x4p.reshape(b * seq, d), pp_all, 1, moe_w)[0].reshape(b, seq, d)
    pool_prompt = pool16[None, :, CTX_ROWS - POOL_CTX:, :]
    k_win_prompt = kp[:, seq - WINDOW:].reshape(b, WINDOW, N_KV_HEADS, HEAD_DIM)
    v_win_prompt = vp[:, seq - WINDOW:].reshape(b, WINDOW, N_KV_HEADS, HEAD_DIM)

    ps_all = p_sample.reshape(depth, sb * steps, D_PLE)
    x1s, pool_s = _mixer0_sample(x_sample, state_pool, nmix0, pw, pscale)
    x3s, ks, vs = _moe_layer(x1s.reshape(sb * steps, d), ps_all, 0, moe_w, dict(kv_w, rope=rope_s))
    kc = cache_k_win.reshape(sb, WINDOW, LANES)
    vc = cache_v_win.reshape(sb, WINDOW, LANES)
    x4s = _attn_sample(x3s, ks, vs, kc, vc, attn_w, rope_s, steps)
    y_sample = _moe_layer(x4s, ps_all, 1, moe_w)[0].reshape(sb, steps, d)
    pool_sample = pool_s[None]
    heads = (sb, steps, N_KV_HEADS, HEAD_DIM)
    k_win_sample = jnp.concatenate([cache_k_win[:, steps:], ks.reshape(heads)], axis=1)
    v_win_sample = jnp.concatenate([cache_v_win[:, steps:], vs.reshape(heads)], axis=1)

    return (y_prompt, y_sample, pool_prompt, pool_sample,
            k_win_prompt, v_win_prompt, k_win_sample, v_win_sample)
```

```python
import functools

import jax
import jax.numpy as jnp
from jax import lax
from jax.experimental import pallas as pl
from jax.experimental.pallas import tpu as pltpu

D_MODEL = 1024
PAST_LEN = 8192
POOL_WINDOWS = (2, 4, 8, 16)
POOL_GROUP_DIM = D_MODEL // len(POOL_WINDOWS)
POOL_CTX = max(POOL_WINDOWS) - 1
assert all(win == 2 ** (g + 1) for g, win in enumerate(POOL_WINDOWS))
HEAD_DIM = 64
N_HEADS = D_MODEL // HEAD_DIM
N_KV_HEADS = 2
Q_PER_KV = N_HEADS // N_KV_HEADS
WINDOW = 128
ROPE_DIM = HEAD_DIM // 4
ROPE_THETA = 500000.0
N_GROUPS = 4
E_PER_GROUP = 4
N_EXPERTS = N_GROUPS * E_PER_GROUP
D_EXPERT = 256
D_PLE = 256
EPS = 1e-6

LANES = 128
SUBLANES = 8
CTX_ROWS = 16
TOKEN_TILE = 1024
ATTN_TILE = 1024
ATTN_BLOCK_UNROLL = 4
SAMPLE_SEQ_BLOCK = 32
SAMPLE_SEQ_UNROLL = 8
VMEM_LIMIT = 60 * 1024 * 1024
NEG_BIG = -1e30
LOG2_E = 1.4426950408889634
F32 = jnp.float32
BF16 = jnp.bfloat16

EXPERT_LANE0 = SUBLANES
MOE_SUBTILE = 256
BF16_TILE_ROWS = 2 * SUBLANES
SEG_ALIGN = BF16_TILE_ROWS
SORT_ROWS = 768
EXPERT_FIXED = 192
EXPERT_STEP = 32
EXPERTS_PER_STEP = 4
SORT_BLOCKS = SORT_ROWS // SEG_ALIGN
MOE_SUBTILES = TOKEN_TILE // MOE_SUBTILE
XE_DUMP_ROW = (2 * TOKEN_TILE + MOE_SUBTILES * N_EXPERTS * (SEG_ALIGN - 1)
               + N_EXPERTS * (EXPERT_STEP - SEG_ALIGN + EXPERT_FIXED))
META_BASE, META_ROWS = 64, 65
assert SORT_ROWS >= 2 * MOE_SUBTILE + N_EXPERTS * (SEG_ALIGN - 1) and SORT_ROWS % LANES == 0
assert TOKEN_TILE % MOE_SUBTILE == 0 and MOE_SUBTILES <= N_EXPERTS and SORT_BLOCKS <= META_BASE
assert EXPERT_FIXED % EXPERT_STEP == 0 and EXPERT_STEP % SEG_ALIGN == 0
assert N_EXPERTS % EXPERTS_PER_STEP == 0 and XE_DUMP_ROW % SEG_ALIGN == 0


def _rms(x, g):
    return x * lax.rsqrt(jnp.mean(x * x, axis=-1, keepdims=True) + EPS) * g


def _dot(a, b):
    return jnp.dot(a, b, preferred_element_type=F32)


def _dot_nt(a, b):
    return lax.dot_general(a, b, (((1,), (1,)), ((), ())), preferred_element_type=F32)


def _dot_tn(a, b):
    return lax.dot_general(a, b, (((0,), (0,)), ((), ())), preferred_element_type=F32)


def _split_bf16(x):
    hi = x.astype(BF16)
    return hi, (x - hi.astype(F32)).astype(BF16)


def _dot3(a_hi, a_lo, b):
    b_hi, b_lo = _split_bf16(b)
    return _dot(a_hi, b_hi) + _dot(a_lo, b_hi) + _dot(a_hi, b_lo)


def _sigmoid(x):
    return 1.0 / (1.0 + jnp.exp(-x))


def _rope_parts(rope_ref):
    return rope_ref[:, 0:LANES], rope_ref[:, LANES:2 * LANES], rope_ref[:, 2 * LANES:3 * LANES]


def _same_head():
    hi = lax.broadcasted_iota(jnp.int32, (LANES, LANES), 0) // HEAD_DIM
    hj = lax.broadcasted_iota(jnp.int32, (LANES, LANES), 1) // HEAD_DIM
    return jnp.where(hi == hj, 1.0, 0.0).astype(BF16)


def _head_norm_rope(x, gain, same_head, cos, sin_next, sin_prev, two_piece):
    sq_hi, sq_lo = _split_bf16(x * x)
    ssq = _dot(sq_hi, same_head)
    if two_piece:
        ssq = ssq + _dot(sq_lo, same_head)
    y = x * lax.rsqrt(ssq * (1.0 / HEAD_DIM) + EPS) * gain
    half = ROPE_DIM // 2
    return (y * cos + pltpu.roll(y, LANES - half, 1) * sin_next
            + pltpu.roll(y, half, 1) * sin_prev)


def _pool_mix(h, window_sum, pos, x, pw_ref, pscale_ref, store):
    for g, win in enumerate(POOL_WINDOWS):
        cols = slice(g * POOL_GROUP_DIM, (g + 1) * POOL_GROUP_DIM)
        hg = h[..., cols]
        inv_cnt = 1.0 / jnp.minimum(win, pos + 1).astype(F32)
        pooled = (window_sum(g, cols) * inv_cnt - hg).reshape(-1, POOL_GROUP_DIM)
        hi, lo = _split_bf16(pooled)
        mixed = _dot3(hi, lo, pw_ref[g])
        store(cols, x[..., cols] + mixed.reshape(hg.shape) * pscale_ref[:, cols])


def _mixer0_prompt_kernel(x_ref, xprev_ref, nmix_ref, pw_ref, pscale_ref,
                          x1_ref, pool_ref, hs_ref, ua_ref, ub_ref):
    t = pl.program_id(1)
    tm = x_ref.shape[1]
    x = x_ref[0]
    h = _rms(x, nmix_ref[...])
    hp = _rms(xprev_ref[0], nmix_ref[...])
    top = 2 * CTX_ROWS
    hs_ref[0:CTX_ROWS, :] = jnp.zeros((CTX_ROWS, x.shape[1]), F32)
    hs_ref[CTX_ROWS:top, :] = jnp.where(t > 0, hp, 0.0)
    hs_ref[top:, :] = h
    ua_ref[0:SUBLANES, :] = jnp.zeros((SUBLANES, x.shape[1]), F32)
    ub_ref[0:SUBLANES, :] = jnp.zeros((SUBLANES, x.shape[1]), F32)

    n = tm + top - SUBLANES
    src, level_of = hs_ref, []
    for g, win in enumerate(POOL_WINDOWS):
        dst = ua_ref if g % 2 == 0 else ub_ref
        c0 = g * POOL_GROUP_DIM
        back = win // 2
        dst[SUBLANES:, c0:] = src[SUBLANES:, c0:] + src[SUBLANES - back:SUBLANES - back + n, c0:]
        level_of.append(dst)
        src = dst

    def window_sum(g, cols):
        return level_of[g][top:, cols]

    def store(cols, val):
        x1_ref[0, :, cols] = val

    pos = t * tm + lax.broadcasted_iota(jnp.int32, (tm, 1), 0)
    _pool_mix(h, window_sum, pos, x, pw_ref, pscale_ref, store)

    @pl.when(t == pl.num_programs(1) - 1)
    def _():
        pool_ref[0] = hs_ref[tm + top - CTX_ROWS:tm + top, :]


def _mixer0_sample_kernel(x_ref, ctx_ref, nmix_ref, pw_ref, pscale_ref,
                          x1_ref, pool_ref, hs_ref):
    x = x_ref[...]
    steps = x.shape[1]
    h = _rms(x, nmix_ref[...])
    hs_ref[:, CTX_ROWS - POOL_CTX:CTX_ROWS, :] = ctx_ref[...]
    hs_ref[:, CTX_ROWS:, :] = h
    pool_ref[:, 0:POOL_CTX - steps, :] = ctx_ref[:, steps:, :]
    pool_ref[:, POOL_CTX - steps:, :] = h

    def window_sum(g, cols):
        s = h[..., cols]
        for j in range(1, POOL_WINDOWS[g]):
            s = s + hs_ref[:, CTX_ROWS - j:CTX_ROWS - j + steps, cols]
        return s

    def store(cols, val):
        x1_ref[:, :, cols] = val

    pos = PAST_LEN + lax.broadcasted_iota(jnp.int32, (1, steps, 1), 1)
    _pool_mix(h, window_sum, pos, x, pw_ref, pscale_ref, store)


def _mixer0_prompt(x, nmix, pw, pscale):
    b, seq, d = x.shape
    tm = TOKEN_TILE
    nt = seq // tm
    per = tm // CTX_ROWS
    const2 = lambda bi, ti: (0, 0)
    return pl.pallas_call(
        _mixer0_prompt_kernel,
        grid=(b, nt),
        in_specs=[
            pl.BlockSpec((1, tm, d), lambda bi, ti: (bi, ti, 0)),
            pl.BlockSpec((1, CTX_ROWS, d), lambda bi, ti: (bi, jnp.maximum(ti * per - 1, 0), 0)),
            pl.BlockSpec((1, d), const2),
            pl.BlockSpec(pw.shape, lambda bi, ti: (0, 0, 0)),
            pl.BlockSpec((1, d), const2),
        ],
        out_specs=[
            pl.BlockSpec((1, tm, d), lambda bi, ti: (bi, ti, 0)),
            pl.BlockSpec((1, CTX_ROWS, d), lambda bi, ti: (bi, 0, 0)),
        ],
        out_shape=[
            jax.ShapeDtypeStruct((b, seq, d), F32),
            jax.ShapeDtypeStruct((b, CTX_ROWS, d), F32),
        ],
        scratch_shapes=[pltpu.VMEM((tm + 2 * CTX_ROWS, d), F32)] * 3,
        compiler_params=pltpu.CompilerParams(
            dimension_semantics=("arbitrary", "arbitrary"), vmem_limit_bytes=VMEM_LIMIT),
        name="mixer0_prompt",
    )(x, x, nmix, pw, pscale)


def _mixer0_sample(x, ctx_layers, nmix, pw, pscale):
    b, steps, d = x.shape
    sb = SAMPLE_SEQ_BLOCK
    const2 = lambda i: (0, 0)
    return pl.pallas_call(
        _mixer0_sample_kernel,
        grid=(b // sb,),
        in_specs=[
            pl.BlockSpec((sb, steps, d), lambda i: (i, 0, 0)),
            pl.BlockSpec((None, sb, POOL_CTX, d), lambda i: (0, i, 0, 0)),
            pl.BlockSpec((1, d), const2),
            pl.BlockSpec(pw.shape, lambda i: (0, 0, 0)),
            pl.BlockSpec((1, d), const2),
        ],
        out_specs=[
            pl.BlockSpec((sb, steps, d), lambda i: (i, 0, 0)),
            pl.BlockSpec((sb, POOL_CTX, d), lambda i: (i, 0, 0)),
        ],
        out_shape=[
            jax.ShapeDtypeStruct((b, steps, d), F32),
            jax.ShapeDtypeStruct((b, POOL_CTX, d), F32),
        ],
        scratch_shapes=[pltpu.VMEM((sb, CTX_ROWS + steps, d), F32)],
        compiler_params=pltpu.CompilerParams(
            dimension_semantics=("arbitrary",), vmem_limit_bytes=VMEM_LIMIT),
        name="mixer0_sample",
    )(x, ctx_layers, nmix, pw, pscale)


def _route_rows(glog, elog):
    sub8 = lax.broadcasted_iota(jnp.int32, glog.shape, 0).astype(F32)
    is_grp = sub8 < N_GROUPS
    gmax = jnp.max(jnp.where(is_grp, glog, NEG_BIG), axis=0, keepdims=True)
    gsum = jnp.sum(jnp.where(is_grp, jnp.exp(jnp.minimum(glog - gmax, 0.0)), 0.0),
                   axis=0, keepdims=True)
    g_w = 1.0 / gsum
    g_sel = jnp.min(jnp.where(is_grp & (glog == gmax), sub8, float(N_GROUPS)),
                    axis=0, keepdims=True)
    sub16 = lax.broadcasted_iota(jnp.int32, elog.shape, 0).astype(F32)
    lo = E_PER_GROUP * g_sel
    in_grp = (sub16 >= lo) & (sub16 < lo + E_PER_GROUP)
    none = float(N_EXPERTS)
    v1 = jnp.max(jnp.where(in_grp, elog, NEG_BIG), axis=0, keepdims=True)
    i1 = jnp.min(jnp.where(in_grp & (elog == v1), sub16, none), axis=0, keepdims=True)
    rest = in_grp & (sub16 != i1)
    v2 = jnp.max(jnp.where(rest, elog, NEG_BIG), axis=0, keepdims=True)
    i2 = jnp.min(jnp.where(rest & (elog == v2), sub16, none), axis=0, keepdims=True)
    e2 = jnp.exp(v2 - v1)
    den = 1.0 + e2
    return i1, i2, (1.0 / den) * g_w, (e2 / den) * g_w


def _dispatch(x1_ref, nffn_ref, rw_ref, rb_ref,
              xe_s, perm_s, gsort_s, meta_v, meta_sm, sem):
    tm = x1_ref.shape[0]
    sub = MOE_SUBTILE
    nsub = tm // sub
    hi, lo = _split_bf16(_rms(x1_ref[...], nffn_ref[...]))
    w_hi, w_lo = _split_bf16(rw_ref[...])
    wide = _dot(hi, jnp.concatenate([w_hi, w_lo], axis=1))
    logits = wide[:, :LANES] + wide[:, LANES:] + _dot(lo, w_hi) + rb_ref[...]
    lt = logits.T
    i1, i2, w1, w2 = _route_rows(lt[0:SUBLANES], lt[EXPERT_LANE0:EXPERT_LANE0 + N_EXPERTS])

    sub16 = lax.broadcasted_iota(jnp.int32, (N_EXPERTS, 2 * sub), 0).astype(F32)
    esub = lax.broadcasted_iota(jnp.int32, (N_EXPERTS, LANES), 0)
    mlane = lax.broadcasted_iota(jnp.int32, (N_EXPERTS, LANES), 1)

    def round_up(x, step):
        return jnp.floor((x + (step - 1)) * (1.0 / step)) * step

    def starts(sizes):
        inc, k = sizes, 1
        while k < N_EXPERTS:
            inc = inc + jnp.where(esub >= k, pltpu.roll(inc, k, 0), 0.0)
            k *= 2
        return inc - sizes

    onehots, sizes = [], []
    for s in range(nsub):
        sl = slice(s * sub, (s + 1) * sub)
        sel = jnp.concatenate([i1[:, sl], i2[:, sl]], axis=1)
        at = jnp.where(sub16 == sel, 1.0, 0.0)
        cnt = jnp.sum(at, axis=1, keepdims=True)
        onehots.append(at)
        sizes.append(jnp.broadcast_to(round_up(cnt, SEG_ALIGN), (N_EXPERTS, LANES)))

    region = jnp.maximum(round_up(sum(sizes), EXPERT_STEP), float(EXPERT_FIXED))
    base = starts(region)
    meta = jnp.where(mlane == META_BASE, base, jnp.where(mlane == META_ROWS, region, 0.0))
    blk0 = mlane.astype(F32) * SEG_ALIGN
    seg_starts, at_row = [], base
    for s in range(nsub):
        seg0 = starts(sizes[s])
        seg_starts.append(seg0)
        inside = (blk0 >= seg0) & (blk0 < seg0 + sizes[s])
        dst = jnp.sum(jnp.where(inside, at_row - seg0 + blk0, 0.0), axis=0, keepdims=True)
        used = jnp.sum(jnp.where(inside, 1.0, 0.0), axis=0, keepdims=True)
        dst = jnp.where(used > 0.0, dst, float(XE_DUMP_ROW))
        meta = jnp.where((esub == s) & (mlane < SORT_BLOCKS), dst, meta)
        at_row = at_row + sizes[s]
    meta_v[...] = meta.astype(jnp.int32)
    cp = pltpu.make_async_copy(meta_v, meta_sm, sem)
    cp.start()

    jr = lax.broadcasted_iota(jnp.int32, (2 * sub, 2 * sub), 0)
    jc = lax.broadcasted_iota(jnp.int32, (2 * sub, 2 * sub), 1)
    earlier = jnp.where(jr < jc, 1.0, 0.0).astype(BF16)
    rsub = lax.broadcasted_iota(jnp.int32, (SORT_ROWS, sub), 0).astype(F32)
    for s in range(nsub):
        sl = slice(s * sub, (s + 1) * sub)
        at = onehots[s]
        rank = _dot(at.astype(BF16), earlier)
        dest = jnp.sum(at * (seg_starts[s][:, 0:1] + rank), axis=0, keepdims=True)
        p1 = rsub == dest[:, :sub]
        p2 = rsub == dest[:, sub:]
        perm = jnp.where(p1 | p2, 1.0, 0.0).astype(BF16)
        gsort_s[s] = jnp.sum(jnp.where(p1, w1[:, sl], 0.0) + jnp.where(p2, w2[:, sl], 0.0),
                             axis=1, keepdims=True)
        perm_s[s] = perm
        xs = _dot(perm, hi[sl, :]).astype(BF16)
        if s == 0:
            cp.wait()
        for b in range(SORT_BLOCKS):
            row = pl.multiple_of(meta_sm[s, b], SEG_ALIGN)
            xe_s[pl.ds(row, SEG_ALIGN), :] = xs[b * SEG_ALIGN:(b + 1) * SEG_ALIGN, :]


def _expert_group(first, xe_s, act_s, meta_sm, wg_ref, wu_ref, wd_ref):
    def hidden(j, r0, m):
        x = xe_s[pl.ds(r0, m), :]
        gt = _dot(x, wg_ref[j])
        up = _dot(x, wu_ref[j])
        return ((gt * _sigmoid(gt)) * up).astype(BF16)

    bases = [pl.multiple_of(meta_sm[first + j, META_BASE], EXPERT_STEP)
             for j in range(EXPERTS_PER_STEP)]
    for j in range(EXPERTS_PER_STEP):
        act_s[j] = hidden(j, bases[j], EXPERT_FIXED)
    for j in range(EXPERTS_PER_STEP):
        xe_s[pl.ds(bases[j], EXPERT_FIXED), :] = _dot(act_s[j], wd_ref[j]).astype(BF16)
    for j in range(EXPERTS_PER_STEP):
        def piece(c, carry, j=j):
            r0 = pl.multiple_of(bases[j] + EXPERT_FIXED + c * EXPERT_STEP, EXPERT_STEP)
            xe_s[pl.ds(r0, EXPERT_STEP), :] = _dot(hidden(j, r0, EXPERT_STEP), wd_ref[j]).astype(BF16)
            return carry
        lax.fori_loop(0, (meta_sm[first + j, META_ROWS] - EXPERT_FIXED) // EXPERT_STEP, piece, 0)


def _moe_kernel(with_kv, *refs):
    if with_kv:
        (x1_ref, p_ref, nffn_ref, rw_ref, rb_ref, wg_ref, wu_ref, wd_ref,
         nple_ref, pg_ref, pp_ref, kvn_ref, wkv_ref, kn_ref, rope_ref,
         out_ref, k_ref, v_ref, xe_s, ys_s, act_s, perm_s, gsort_s, meta_v, meta_sm, sem) = refs
    else:
        (x1_ref, p_ref, nffn_ref, rw_ref, rb_ref, wg_ref, wu_ref, wd_ref,
         nple_ref, pg_ref, pp_ref,
         out_ref, xe_s, ys_s, act_s, perm_s, gsort_s, meta_v, meta_sm, sem) = refs
    i = pl.program_id(0)
    step = pl.program_id(1)
    nsub = x1_ref.shape[0] // MOE_SUBTILE

    @pl.when((i == 0) & (step == 0))
    def _():
        xe_s[...] = jnp.zeros_like(xe_s)

    @pl.when(step == 0)
    def _():
        _dispatch(x1_ref, nffn_ref, rw_ref, rb_ref,
                  xe_s, perm_s, gsort_s, meta_v, meta_sm, sem)

    _expert_group(step * EXPERTS_PER_STEP, xe_s, act_s, meta_sm, wg_ref, wu_ref, wd_ref)

    @pl.when(step == pl.num_programs(1) - 1)
    def _():
        for s in range(nsub):
            sl = slice(s * MOE_SUBTILE, (s + 1) * MOE_SUBTILE)
            for b in range(SORT_BLOCKS):
                rows = slice(b * SEG_ALIGN, (b + 1) * SEG_ALIGN)
                src = pl.multiple_of(meta_sm[s, b], SEG_ALIGN)
                blk = xe_s[pl.ds(src, SEG_ALIGN), :].astype(F32) * gsort_s[s, rows, :]
                ys_s[rows, :] = blk.astype(BF16)
            out_ref[sl, :] = x1_ref[sl, :] + _dot_tn(perm_s[s], ys_s[...])
        x2 = out_ref[...]
        hp = _rms(x2, nple_ref[...]).astype(BF16)
        gate = _sigmoid(_dot(hp, pg_ref[...]))
        proj = _dot(p_ref[...].astype(BF16), pp_ref[...])
        x3 = x2 + gate * proj
        out_ref[...] = x3
        if with_kv:
            hk = _rms(x3, kvn_ref[...]).astype(BF16)
            kv = _dot(hk, wkv_ref[...])
            k_ref[...] = _head_norm_rope(kv[:, :LANES], kn_ref[...], _same_head(),
                                         *_rope_parts(rope_ref), two_piece=True)
            v_ref[...] = kv[:, LANES:]


def _moe_layer(x1, p_all, layer, w, kv=None):
    n, d = x1.shape
    tm = TOKEN_TILE
    nt = n // tm
    nsub = tm // MOE_SUBTILE
    tok = lambda i, e: (i, 0)
    const2 = lambda i, e: (0, 0)
    lay3 = lambda i, e: (layer, 0, 0)
    exp4 = lambda i, e: (layer, e, 0, 0)
    in_specs = [
        pl.BlockSpec((tm, d), tok),
        pl.BlockSpec((None, tm, D_PLE), lambda i, e: (layer, i, 0)),
        pl.BlockSpec((None, 1, d), lay3),
        pl.BlockSpec((None, d, LANES), lay3),
        pl.BlockSpec((None, 1, LANES), lay3),
        pl.BlockSpec((None, EXPERTS_PER_STEP, d, D_EXPERT), exp4),
        pl.BlockSpec((None, EXPERTS_PER_STEP, d, D_EXPERT), exp4),
        pl.BlockSpec((None, EXPERTS_PER_STEP, D_EXPERT, d), exp4),
        pl.BlockSpec((None, 1, d), lay3),
        pl.BlockSpec((None, d, d), lay3),
        pl.BlockSpec((None, D_PLE, d), lay3),
    ]
    args = [x1, p_all, w["norm_ffn"], w["router_w"], w["router_b"],
            w["exp_gate"], w["exp_up"], w["exp_down"], w["norm_ple"], w["ple_gate"], w["ple_proj"]]
    out_specs = [pl.BlockSpec((tm, d), tok)]
    out_shape = [jax.ShapeDtypeStruct((n, d), F32)]
    if kv is not None:
        tab_blocks = kv["rope"].shape[0] // tm
        in_specs += [
            pl.BlockSpec((1, d), const2),
            pl.BlockSpec((d, 2 * LANES), const2),
            pl.BlockSpec((1, LANES), const2),
            pl.BlockSpec((tm, 3 * LANES), lambda i, e: (i % tab_blocks, 0)),
        ]
        args += [kv["kv_norm"], kv["w_kv"], kv["k_norm"], kv["rope"]]
        out_specs += [pl.BlockSpec((tm, LANES), tok), pl.BlockSpec((tm, LANES), tok)]
        out_shape += [jax.ShapeDtypeStruct((n, LANES), F32), jax.ShapeDtypeStruct((n, LANES), F32)]
    return pl.pallas_call(
        functools.partial(_moe_kernel, kv is not None),
        grid=(nt, N_EXPERTS // EXPERTS_PER_STEP),
        in_specs=in_specs,
        out_specs=out_specs,
        out_shape=out_shape,
        scratch_shapes=[
            pltpu.VMEM((XE_DUMP_ROW + SEG_ALIGN, d), BF16),
            pltpu.VMEM((SORT_ROWS, d), BF16),
            pltpu.VMEM((EXPERTS_PER_STEP, EXPERT_FIXED, D_EXPERT), BF16),
            pltpu.VMEM((nsub, SORT_ROWS, MOE_SUBTILE), BF16),
            pltpu.VMEM((nsub, SORT_ROWS, 1), F32),
            pltpu.VMEM((N_EXPERTS, LANES), jnp.int32),
            pltpu.SMEM((N_EXPERTS, LANES), jnp.int32),
            pltpu.SemaphoreType.DMA,
        ],
        compiler_params=pltpu.CompilerParams(
            dimension_semantics=("arbitrary", "arbitrary"), vmem_limit_bytes=VMEM_LIMIT),
        name="moe_kv" if kv is not None else "moe",
    )(*args)


def _project_q(x3, nmix_ref, wq_ref, qn_ref, cos, sin_next, sin_prev, q_s):
    h = _rms(x3, nmix_ref[...]).astype(BF16)
    same_head = _same_head()
    wide = 4 * LANES
    for blk in range(D_MODEL // wide):
        q = _dot(h, wq_ref[:, blk * wide:(blk + 1) * wide])
        for c in range(wide // LANES):
            cols = slice(blk * wide + c * LANES, blk * wide + (c + 1) * LANES)
            qc = _head_norm_rope(q[:, c * LANES:(c + 1) * LANES], qn_ref[...], same_head,
                                 cos, sin_next, sin_prev, two_piece=False)
            q_s[:, cols] = (qc * (HEAD_DIM ** -0.5 * LOG2_E)).astype(q_s.dtype)


def _both_orders(x):
    return x.astype(BF16), pltpu.roll(x, HEAD_DIM, 1).astype(BF16)


def _kv_order(par, g):
    return 0 if par == g else 1


def _softmax_sink(s, mask, sink):
    col0 = lax.broadcasted_iota(jnp.int32, (s.shape[0], LANES), 1) == 0
    s = jnp.where(mask, s, NEG_BIG)
    s = jnp.concatenate([jnp.where(col0, sink * LOG2_E, s[:, :LANES]), s[:, LANES:]], axis=1)
    p = jnp.exp2(s - jnp.max(s, axis=-1, keepdims=True))
    return p.astype(BF16), 1.0 / jnp.sum(p, axis=-1, keepdims=True)


def _drop_first_row(vals, tile_rows):
    row = lax.broadcasted_iota(jnp.int32, (tile_rows, vals.shape[1]), 0)
    head = jnp.where(row == 0, jnp.zeros_like(vals[:tile_rows]), vals[:tile_rows])
    return jnp.concatenate([head, vals[tile_rows:]], axis=0)


def _attn_prompt_kernel(x3_ref, k_ref, v_ref, kprev_ref, vprev_ref, nmix_ref, wq_ref, qn_ref,
                        rope_ref, sinks_ref, wo_ref,
                        x4_ref, q_s, o_s, kd_s, vd_s):
    t = pl.program_id(1)
    tq = x3_ref.shape[1]
    x3 = x3_ref[0]
    _project_q(x3, nmix_ref, wq_ref, qn_ref, *_rope_parts(rope_ref), q_s)

    for src_prev, src, dst in ((kprev_ref, k_ref, kd_s), (vprev_ref, v_ref, vd_s)):
        dst[0, 0:WINDOW, :], dst[1, 0:WINDOW, :] = _both_orders(src_prev[0])
        dst[0, WINDOW:, :], dst[1, WINDOW:, :] = _both_orders(src[0])

    qi = lax.broadcasted_iota(jnp.int32, (WINDOW, 2 * WINDOW), 0)
    ki = lax.broadcasted_iota(jnp.int32, (WINDOW, 2 * WINDOW), 1)
    band = (ki > qi) & (ki <= qi + WINDOW)
    lane = lax.broadcasted_iota(jnp.int32, (WINDOW, LANES), 1)
    first = lane < HEAD_DIM

    def q_block(j, carry):
        r0 = pl.multiple_of(j * WINDOW, WINDOW)
        mask = band & (ki >= (1 - j) * WINDOW - t * tq)
        vals = [_drop_first_row(vd_s[order, pl.ds(r0, 2 * WINDOW), :], BF16_TILE_ROWS)
                for order in range(2)]
        for c in range(D_MODEL // LANES):
            g = (2 * c) // Q_PER_KV
            cols = slice(c * LANES, (c + 1) * LANES)
            qc = q_s[pl.ds(r0, WINDOW), cols]
            outs = []
            for par in range(2):
                order = _kv_order(par, g)
                qh = jnp.where(first, qc, 0) if par == 0 else jnp.where(first, 0, qc)
                s = _dot_nt(qh, kd_s[order, pl.ds(r0, 2 * WINDOW), :])
                pr, inv = _softmax_sink(s, mask, sinks_ref[2 * c + par])
                outs.append(_dot(pr, vals[order]) * inv)
            o_s[pl.ds(r0, WINDOW), cols] = jnp.where(first, outs[0], outs[1]).astype(BF16)
        return carry

    lax.fori_loop(0, tq // WINDOW, q_block, 0, unroll=ATTN_BLOCK_UNROLL)
    x4_ref[0] = x3 + _dot(o_s[...], wo_ref[...])


def _attn_prompt(x3, k, v, w, rope):
    b, seq, d = x3.shape
    tq = ATTN_TILE
    nt = seq // tq
    per = tq // WINDOW
    tile = lambda bi, ti: (bi, ti, 0)
    prev = lambda bi, ti: (bi, jnp.maximum(ti * per - 1, 0), 0)
    const2 = lambda bi, ti: (0, 0)
    tab = lambda bi, ti: (ti, 0)
    return pl.pallas_call(
        _attn_prompt_kernel,
        grid=(b, nt),
        in_specs=[
            pl.BlockSpec((1, tq, d), tile),
            pl.BlockSpec((1, tq, LANES), tile),
            pl.BlockSpec((1, tq, LANES), tile),
            pl.BlockSpec((1, WINDOW, LANES), prev),
            pl.BlockSpec((1, WINDOW, LANES), prev),
            pl.BlockSpec((1, d), const2),
            pl.BlockSpec((d, d), const2),
            pl.BlockSpec((1, LANES), const2),
            pl.BlockSpec((tq, 3 * LANES), tab),
            pl.BlockSpec(memory_space=pltpu.SMEM),
            pl.BlockSpec((d, d), const2),
        ],
        out_specs=pl.BlockSpec((1, tq, d), tile),
        out_shape=jax.ShapeDtypeStruct((b, seq, d), F32),
        scratch_shapes=[
            pltpu.VMEM((tq, d), BF16),
            pltpu.VMEM((tq, d), BF16),
            pltpu.VMEM((N_KV_HEADS, tq + WINDOW, LANES), BF16),
            pltpu.VMEM((N_KV_HEADS, tq + WINDOW, LANES), BF16),
        ],
        compiler_params=pltpu.CompilerParams(
            dimension_semantics=("arbitrary", "arbitrary"), vmem_limit_bytes=VMEM_LIMIT),
        name="attn_prompt",
    )(x3, k, v, k, v, w["norm_mix1"], w["w_q"], w["q_norm"], rope, w["sinks"], w["w_o"])


def _attn_sample_kernel(x3_ref, knew_ref, vnew_ref, kc_ref, vc_ref, nmix_ref, wq_ref, qn_ref,
                        rope_ref, sinks_ref, wo_ref,
                        x4_ref, kwin_ref, vwin_ref, q_s, o_s):
    rows = x3_ref.shape[0]
    steps = knew_ref.shape[0] // kc_ref.shape[0]
    x3 = x3_ref[...]
    _project_q(x3, nmix_ref, wq_ref, qn_ref, *_rope_parts(rope_ref), q_s)

    stack = N_HEADS * steps
    ri = lax.broadcasted_iota(jnp.int32, (stack, 2 * WINDOW), 0)
    ki = lax.broadcasted_iota(jnp.int32, (stack, 2 * WINDOW), 1)
    tq = ri % steps
    mask = ((ki < WINDOW) & (ki > tq)) | ((ki >= WINDOW) & ((ki - WINDOW) <= tq))
    head_of_row = lax.broadcasted_iota(jnp.int32, (stack, 1), 0) // steps
    sink_col = jnp.zeros((stack, 1), F32)
    for hd in range(N_HEADS):
        sink_col = jnp.where(head_of_row == hd, sinks_ref[hd], sink_col)
    lane = lax.broadcasted_iota(jnp.int32, (steps, LANES), 1)
    first = lane < HEAD_DIM
    pad = jnp.zeros((WINDOW - steps, LANES), F32)

    def to_kv_half(slab, par, g):
        return slab if par == g else pltpu.roll(slab, HEAD_DIM, 1)

    def one_seq(b):
        r0 = pl.multiple_of(b * steps, steps)
        keys = jnp.concatenate([kc_ref[b], knew_ref[pl.ds(r0, steps), :], pad], axis=0).astype(BF16)
        vals = jnp.concatenate([_drop_first_row(vc_ref[b], SUBLANES), vnew_ref[pl.ds(r0, steps), :],
                                pad], axis=0).astype(BF16)
        slabs = []
        for hd in range(N_HEADS):
            par, g = hd % 2, hd // Q_PER_KV
            qc = q_s[pl.ds(r0, steps), (hd // 2) * LANES:(hd // 2 + 1) * LANES]
            own = jnp.where(first, qc, 0.0) if par == 0 else jnp.where(first, 0.0, qc)
            slabs.append(to_kv_half(own, par, g))
        s = _dot_nt(jnp.concatenate(slabs, axis=0).astype(BF16), keys)
        pr, inv = _softmax_sink(s, mask, sink_col)
        out = _dot(pr, vals) * inv
        for c in range(D_MODEL // LANES):
            g = (2 * c) // Q_PER_KV
            even = to_kv_half(out[(2 * c) * steps:(2 * c + 1) * steps], g, 0)
            odd = to_kv_half(out[(2 * c + 1) * steps:(2 * c + 2) * steps], g, 1)
            o_s[pl.ds(r0, steps), c * LANES:(c + 1) * LANES] = jnp.where(first, even, odd)

    def seq_pair(i, carry):
        for u in range(SAMPLE_SEQ_UNROLL):
            one_seq(i * SAMPLE_SEQ_UNROLL + u)
        return carry

    lax.fori_loop(0, rows // steps // SAMPLE_SEQ_UNROLL, seq_pair, 0)
    x4_ref[...] = x3 + _dot(o_s[...].astype(BF16), wo_ref[...])

    seqs = kc_ref.shape[0]
    for cache_ref, new_ref, win_ref in ((kc_ref, knew_ref, kwin_ref), (vc_ref, vnew_ref, vwin_ref)):
        win_ref[:, 0:WINDOW - steps, :] = cache_ref[:, steps:, :]
        win_ref[:, WINDOW - steps:, :] = new_ref[...].reshape(seqs, steps, LANES)


def _attn_sample(x3, knew, vnew, kcache, vcache, w, rope, steps):
    n, d = x3.shape
    sb = SAMPLE_SEQ_BLOCK
    rows = sb * steps
    tok = lambda i: (i, 0)
    const2 = lambda i: (0, 0)
    return pl.pallas_call(
        _attn_sample_kernel,
        grid=(n // rows,),
        in_specs=[
            pl.BlockSpec((rows, d), tok),
            pl.BlockSpec((rows, LANES), tok),
            pl.BlockSpec((rows, LANES), tok),
            pl.BlockSpec((sb, WINDOW, LANES), lambda i: (i, 0, 0)),
            pl.BlockSpec((sb, WINDOW, LANES), lambda i: (i, 0, 0)),
            pl.BlockSpec((1, d), const2),
            pl.BlockSpec((d, d), const2),
            pl.BlockSpec((1, LANES), const2),
            pl.BlockSpec((rows, 3 * LANES), tok),
            pl.BlockSpec(memory_space=pltpu.SMEM),
            pl.BlockSpec((d, d), const2),
        ],
        out_specs=[
            pl.BlockSpec((rows, d), tok),
            pl.BlockSpec((sb, WINDOW, LANES), lambda i: (i, 0, 0)),
            pl.BlockSpec((sb, WINDOW, LANES), lambda i: (i, 0, 0)),
        ],
        out_shape=[
            jax.ShapeDtypeStruct((n, d), F32),
            jax.ShapeDtypeStruct(kcache.shape, F32),
            jax.ShapeDtypeStruct(vcache.shape, F32),
        ],
        scratch_shapes=[
            pltpu.VMEM((rows, d), F32),
            pltpu.VMEM((rows, d), F32),
        ],
        compiler_params=pltpu.CompilerParams(
            dimension_semantics=("arbitrary",), vmem_limit_bytes=VMEM_LIMIT),
        name="attn_sample",
    )(x3, knew, vnew, kcache, vcache, w["norm_mix1"], w["w_q"], w["q_norm"], rope,
      w["sinks"], w["w_o"])


def _rope_table(pos):
    half = ROPE_DIM // 2
    dim = jnp.arange(LANES, dtype=jnp.int32) % HEAD_DIM
    inv = jnp.float32(ROPE_THETA) ** (-((dim % half).astype(jnp.float32) * 2.0 / ROPE_DIM))
    inv = jnp.where(dim < ROPE_DIM, inv, 0.0)
    ang = pos.astype(jnp.float32)[:, None] * inv[None, :]
    cos, sin = jnp.cos(ang), jnp.sin(ang)
    s_next = jnp.where(dim < half, -sin, 0.0)
    s_prev = jnp.where((dim >= half) & (dim < ROPE_DIM), sin, 0.0)
    return jnp.concatenate([cos, s_next, s_prev], axis=1)


def kernel(x_prompt, x_sample, p_prompt, p_sample, state_pool, cache_k_win, cache_v_win, norm_mix, norm_ffn, norm_ple, pool_w, pool_scale, kv_norm, w_kv, k_norm, w_q, q_norm, sinks, w_o, router_g_w, router_g_b, router_e_w, router_e_b, exp_gate, exp_up, exp_down, ple_gate, ple_proj):
    b, seq, d = x_prompt.shape
    sb, steps, _ = x_sample.shape
    depth = norm_mix.shape[0]
    row = lambda v: v.reshape(1, -1)

    gap_w = jnp.zeros((depth, d, EXPERT_LANE0 - N_GROUPS), F32)
    tail_w = jnp.zeros((depth, d, LANES - EXPERT_LANE0 - N_EXPERTS), F32)
    router_w = jnp.concatenate(
        [router_g_w, gap_w, router_e_w.reshape(depth, d, N_EXPERTS), tail_w], axis=2)
    router_b = jnp.concatenate(
        [router_g_b, gap_w[:, 0], router_e_b.reshape(depth, N_EXPERTS), tail_w[:, 0]], axis=1)
    moe_w = {
        "norm_ffn": norm_ffn.reshape(depth, 1, d), "router_w": router_w,
        "router_b": router_b.reshape(depth, 1, LANES),
        "exp_gate": exp_gate.astype(BF16), "exp_up": exp_up.astype(BF16),
        "exp_down": exp_down.astype(BF16), "norm_ple": norm_ple.reshape(depth, 1, d),
        "ple_gate": ple_gate.astype(BF16), "ple_proj": ple_proj.astype(BF16),
    }
    two = lambda a: jnp.concatenate([a, a])
    attn_w = {
        "norm_mix1": row(norm_mix[1]), "w_q": w_q[0].astype(BF16), "q_norm": row(two(q_norm[0])),
        "sinks": sinks[0], "w_o": w_o[0].astype(BF16),
    }
    pw = pool_w[0]
    nmix0 = row(norm_mix[0])
    pscale = row(pool_scale[0])

    rope_p = _rope_table(jnp.arange(seq, dtype=jnp.int32))
    rope_s = jnp.tile(_rope_table(PAST_LEN + jnp.arange(steps, dtype=jnp.int32)), (sb, 1))
    kv_w = {"kv_norm": row(kv_norm), "w_kv": w_kv.astype(BF16), "k_norm": row(two(k_norm))}

    pp_all = p_prompt.reshape(depth, b * seq, D_PLE)
    x1p, pool16 = _mixer0_prompt(x_prompt, nmix0, pw, pscale)
    x3p, kp, vp = _moe_layer(x1p.reshape(b * seq, d), pp_all, 0, moe_w, dict(kv_w, rope=rope_p))
    kp = kp.reshape(b, seq, LANES)
    vp = vp.reshape(b, seq, LANES)
    x4p = _attn_prompt(x3p.reshape(b, seq, d), kp, vp, attn_w, rope_p)
    y_prompt = _moe_layer(x4p.reshape(b * seq, d), pp_all, 1, moe_w)[0].reshape(b, seq, d)
    pool_prompt = pool16[None, :, CTX_ROWS - POOL_CTX:, :]
    k_win_prompt = kp[:, seq - WINDOW:].reshape(b, WINDOW, N_KV_HEADS, HEAD_DIM)
    v_win_prompt = vp[:, seq - WINDOW:].reshape(b, WINDOW, N_KV_HEADS, HEAD_DIM)

    ps_all = p_sample.reshape(depth, sb * steps, D_PLE)
    x1s, pool_s = _mixer0_sample(x_sample, state_pool, nmix0, pw, pscale)
    x3s, ks, vs = _moe_layer(x1s.reshape(sb * steps, d), ps_all, 0, moe_w, dict(kv_w, rope=rope_s))
    kc = cache_k_win.reshape(sb, WINDOW, LANES)
    vc = cache_v_win.reshape(sb, WINDOW, LANES)
    x4s, kwin, vwin = _attn_sample(x3s, ks, vs, kc, vc, attn_w, rope_s, steps)
    y_sample = _moe_layer(x4s, ps_all, 1, moe_w)[0].reshape(sb, steps, d)
    pool_sample = pool_s[None]
    k_win_sample = kwin.reshape(cache_k_win.shape)
    v_win_sample = vwin.reshape(cache_v_win.shape)

    return (y_prompt, y_sample, pool_prompt, pool_sample,
            k_win_prompt, v_win_prompt, k_win_sample, v_win_sample)
```

```python
import functools

import jax
import jax.numpy as jnp
from jax import lax
from jax.experimental import pallas as pl
from jax.experimental.pallas import tpu as pltpu

D_MODEL = 1024
PAST_LEN = 8192
POOL_WINDOWS = (2, 4, 8, 16)
POOL_GROUP_DIM = D_MODEL // len(POOL_WINDOWS)
POOL_CTX = max(POOL_WINDOWS) - 1
assert all(win == 2 ** (g + 1) for g, win in enumerate(POOL_WINDOWS))
HEAD_DIM = 64
N_HEADS = D_MODEL // HEAD_DIM
N_KV_HEADS = 2
Q_PER_KV = N_HEADS // N_KV_HEADS
WINDOW = 128
ROPE_DIM = HEAD_DIM // 4
ROPE_THETA = 500000.0
N_GROUPS = 4
E_PER_GROUP = 4
N_EXPERTS = N_GROUPS * E_PER_GROUP
D_EXPERT = 256
D_PLE = 256
EPS = 1e-6

LANES = 128
SUBLANES = 8
CTX_ROWS = 16
TOKEN_TILE = 1024
ATTN_TILE = 1024
ATTN_BLOCK_UNROLL = 4
SAMPLE_SEQ_BLOCK = 32
SAMPLE_SEQ_UNROLL = 8
VMEM_LIMIT = 60 * 1024 * 1024
NEG_BIG = -1e30
LOG2_E = 1.4426950408889634
F32 = jnp.float32
BF16 = jnp.bfloat16

EXPERT_LANE0 = SUBLANES
MOE_SUBTILE = 256
BF16_TILE_ROWS = 2 * SUBLANES
SEG_ALIGN = BF16_TILE_ROWS
SORT_ROWS = 768
EXPERT_FIXED = 192
EXPERT_STEP = 32
EXPERTS_PER_STEP = 4
SORT_BLOCKS = SORT_ROWS // SEG_ALIGN
MOE_SUBTILES = TOKEN_TILE // MOE_SUBTILE
XE_DUMP_ROW = (2 * TOKEN_TILE + MOE_SUBTILES * N_EXPERTS * (SEG_ALIGN - 1)
               + N_EXPERTS * (EXPERT_STEP - SEG_ALIGN + EXPERT_FIXED))
META_BASE, META_ROWS = 64, 65
assert SORT_ROWS >= 2 * MOE_SUBTILE + N_EXPERTS * (SEG_ALIGN - 1) and SORT_ROWS % LANES == 0
assert TOKEN_TILE % MOE_SUBTILE == 0 and MOE_SUBTILES <= N_EXPERTS and SORT_BLOCKS <= META_BASE
assert EXPERT_FIXED % EXPERT_STEP == 0 and EXPERT_STEP % SEG_ALIGN == 0
assert N_EXPERTS % EXPERTS_PER_STEP == 0 and XE_DUMP_ROW % SEG_ALIGN == 0


def _rms(x, g):
    return x * lax.rsqrt(jnp.mean(x * x, axis=-1, keepdims=True) + EPS) * g


def _dot(a, b):
    return jnp.dot(a, b, preferred_element_type=F32)


def _dot_nt(a, b):
    return lax.dot_general(a, b, (((1,), (1,)), ((), ())), preferred_element_type=F32)


def _dot_tn(a, b):
    return lax.dot_general(a, b, (((0,), (0,)), ((), ())), preferred_element_type=F32)


def _split_bf16(x):
    hi = x.astype(BF16)
    return hi, (x - hi.astype(F32)).astype(BF16)


def _dot3(a_hi, a_lo, b):
    b_hi, b_lo = _split_bf16(b)
    return _dot(a_hi, b_hi) + _dot(a_lo, b_hi) + _dot(a_hi, b_lo)


def _sigmoid(x):
    return 1.0 / (1.0 + jnp.exp(-x))


def _rope_parts(rope_ref):
    return rope_ref[:, 0:LANES], rope_ref[:, LANES:2 * LANES], rope_ref[:, 2 * LANES:3 * LANES]


def _same_head():
    hi = lax.broadcasted_iota(jnp.int32, (LANES, LANES), 0) // HEAD_DIM
    hj = lax.broadcasted_iota(jnp.int32, (LANES, LANES), 1) // HEAD_DIM
    return jnp.where(hi == hj, 1.0, 0.0).astype(BF16)


def _head_norm_rope(x, gain, same_head, cos, sin_next, sin_prev, two_piece):
    sq_hi, sq_lo = _split_bf16(x * x)
    ssq = _dot(sq_hi, same_head)
    if two_piece:
        ssq = ssq + _dot(sq_lo, same_head)
    y = x * lax.rsqrt(ssq * (1.0 / HEAD_DIM) + EPS) * gain
    half = ROPE_DIM // 2
    return (y * cos + pltpu.roll(y, LANES - half, 1) * sin_next
            + pltpu.roll(y, half, 1) * sin_prev)


def _pool_mix(h, window_sum, pos, x, pw_ref, pscale_ref, store):
    for g, win in enumerate(POOL_WINDOWS):
        cols = slice(g * POOL_GROUP_DIM, (g + 1) * POOL_GROUP_DIM)
        hg = h[..., cols]
        inv_cnt = 1.0 / jnp.minimum(win, pos + 1).astype(F32)
        pooled = (window_sum(g, cols) * inv_cnt - hg).reshape(-1, POOL_GROUP_DIM)
        hi, lo = _split_bf16(pooled)
        mixed = _dot3(hi, lo, pw_ref[g])
        store(cols, x[..., cols] + mixed.reshape(hg.shape) * pscale_ref[:, cols])


def _mixer0_prompt_kernel(x_ref, xprev_ref, nmix_ref, pw_ref, pscale_ref,
                          x1_ref, pool_ref, hs_ref, ua_ref, ub_ref):
    t = pl.program_id(1)
    tm = x_ref.shape[1]
    x = x_ref[0]
    h = _rms(x, nmix_ref[...])
    hp = _rms(xprev_ref[0], nmix_ref[...])
    top = 2 * CTX_ROWS
    hs_ref[0:CTX_ROWS, :] = jnp.zeros((CTX_ROWS, x.shape[1]), F32)
    hs_ref[CTX_ROWS:top, :] = jnp.where(t > 0, hp, 0.0)
    hs_ref[top:, :] = h
    ua_ref[0:SUBLANES, :] = jnp.zeros((SUBLANES, x.shape[1]), F32)
    ub_ref[0:SUBLANES, :] = jnp.zeros((SUBLANES, x.shape[1]), F32)

    n = tm + top - SUBLANES
    src, level_of = hs_ref, []
    for g, win in enumerate(POOL_WINDOWS):
        dst = ua_ref if g % 2 == 0 else ub_ref
        c0 = g * POOL_GROUP_DIM
        back = win // 2
        dst[SUBLANES:, c0:] = src[SUBLANES:, c0:] + src[SUBLANES - back:SUBLANES - back + n, c0:]
        level_of.append(dst)
        src = dst

    def window_sum(g, cols):
        return level_of[g][top:, cols]

    def store(cols, val):
        x1_ref[0, :, cols] = val

    pos = t * tm + lax.broadcasted_iota(jnp.int32, (tm, 1), 0)
    _pool_mix(h, window_sum, pos, x, pw_ref, pscale_ref, store)

    @pl.when(t == pl.num_programs(1) - 1)
    def _():
        pool_ref[0] = hs_ref[tm + top - CTX_ROWS:tm + top, :]


def _mixer0_sample_kernel(x_ref, ctx_ref, nmix_ref, pw_ref, pscale_ref,
                          x1_ref, pool_ref, hs_ref):
    x = x_ref[...]
    steps = x.shape[1]
    h = _rms(x, nmix_ref[...])
    hs_ref[:, CTX_ROWS - POOL_CTX:CTX_ROWS, :] = ctx_ref[...]
    hs_ref[:, CTX_ROWS:, :] = h
    pool_ref[:, 0:POOL_CTX - steps, :] = ctx_ref[:, steps:, :]
    pool_ref[:, POOL_CTX - steps:, :] = h

    def window_sum(g, cols):
        s = h[..., cols]
        for j in range(1, POOL_WINDOWS[g]):
            s = s + hs_ref[:, CTX_ROWS - j:CTX_ROWS - j + steps, cols]
        return s

    def store(cols, val):
        x1_ref[:, :, cols] = val

    pos = PAST_LEN + lax.broadcasted_iota(jnp.int32, (1, steps, 1), 1)
    _pool_mix(h, window_sum, pos, x, pw_ref, pscale_ref, store)


def _mixer0_prompt(x, nmix, pw, pscale):
    b, seq, d = x.shape
    tm = TOKEN_TILE
    nt = seq // tm
    per = tm // CTX_ROWS
    const2 = lambda bi, ti: (0, 0)
    return pl.pallas_call(
        _mixer0_prompt_kernel,
        grid=(b, nt),
        in_specs=[
            pl.BlockSpec((1, tm, d), lambda bi, ti: (bi, ti, 0)),
            pl.BlockSpec((1, CTX_ROWS, d), lambda bi, ti: (bi, jnp.maximum(ti * per - 1, 0), 0)),
            pl.BlockSpec((1, d), const2),
            pl.BlockSpec(pw.shape, lambda bi, ti: (0, 0, 0)),
            pl.BlockSpec((1, d), const2),
        ],
        out_specs=[
            pl.BlockSpec((1, tm, d), lambda bi, ti: (bi, ti, 0)),
            pl.BlockSpec((1, CTX_ROWS, d), lambda bi, ti: (bi, 0, 0)),
        ],
        out_shape=[
            jax.ShapeDtypeStruct((b, seq, d), F32),
            jax.ShapeDtypeStruct((b, CTX_ROWS, d), F32),
        ],
        scratch_shapes=[pltpu.VMEM((tm + 2 * CTX_ROWS, d), F32)] * 3,
        compiler_params=pltpu.CompilerParams(
            dimension_semantics=("arbitrary", "arbitrary"), vmem_limit_bytes=VMEM_LIMIT),
        name="mixer0_prompt",
    )(x, x, nmix, pw, pscale)


def _mixer0_sample(x, ctx_layers, nmix, pw, pscale):
    b, steps, d = x.shape
    sb = SAMPLE_SEQ_BLOCK
    const2 = lambda i: (0, 0)
    return pl.pallas_call(
        _mixer0_sample_kernel,
        grid=(b // sb,),
        in_specs=[
            pl.BlockSpec((sb, steps, d), lambda i: (i, 0, 0)),
            pl.BlockSpec((None, sb, POOL_CTX, d), lambda i: (0, i, 0, 0)),
            pl.BlockSpec((1, d), const2),
            pl.BlockSpec(pw.shape, lambda i: (0, 0, 0)),
            pl.BlockSpec((1, d), const2),
        ],
        out_specs=[
            pl.BlockSpec((sb, steps, d), lambda i: (i, 0, 0)),
            pl.BlockSpec((sb, POOL_CTX, d), lambda i: (i, 0, 0)),
        ],
        out_shape=[
            jax.ShapeDtypeStruct((b, steps, d), F32),
            jax.ShapeDtypeStruct((b, POOL_CTX, d), F32),
        ],
        scratch_shapes=[pltpu.VMEM((sb, CTX_ROWS + steps, d), F32)],
        compiler_params=pltpu.CompilerParams(
            dimension_semantics=("arbitrary",), vmem_limit_bytes=VMEM_LIMIT),
        name="mixer0_sample",
    )(x, ctx_layers, nmix, pw, pscale)


def _route_rows(glog, elog):
    sub8 = lax.broadcasted_iota(jnp.int32, glog.shape, 0).astype(F32)
    is_grp = sub8 < N_GROUPS
    gmax = jnp.max(jnp.where(is_grp, glog, NEG_BIG), axis=0, keepdims=True)
    gsum = jnp.sum(jnp.where(is_grp, jnp.exp(jnp.minimum(glog - gmax, 0.0)), 0.0),
                   axis=0, keepdims=True)
    g_w = 1.0 / gsum
    g_sel = jnp.min(jnp.where(is_grp & (glog == gmax), sub8, float(N_GROUPS)),
                    axis=0, keepdims=True)
    sub16 = lax.broadcasted_iota(jnp.int32, elog.shape, 0).astype(F32)
    lo = E_PER_GROUP * g_sel
    in_grp = (sub16 >= lo) & (sub16 < lo + E_PER_GROUP)
    none = float(N_EXPERTS)
    v1 = jnp.max(jnp.where(in_grp, elog, NEG_BIG), axis=0, keepdims=True)
    i1 = jnp.min(jnp.where(in_grp & (elog == v1), sub16, none), axis=0, keepdims=True)
    rest = in_grp & (sub16 != i1)
    v2 = jnp.max(jnp.where(rest, elog, NEG_BIG), axis=0, keepdims=True)
    i2 = jnp.min(jnp.where(rest & (elog == v2), sub16, none), axis=0, keepdims=True)
    e2 = jnp.exp(v2 - v1)
    den = 1.0 + e2
    return i1, i2, (1.0 / den) * g_w, (e2 / den) * g_w


def _dispatch(x1_ref, nffn_ref, rw_ref, rb_ref,
              xe_s, perm_s, gsort_s, meta_v, meta_sm, sem):
    tm = x1_ref.shape[0]
    sub = MOE_SUBTILE
    nsub = tm // sub
    hi, lo = _split_bf16(_rms(x1_ref[...], nffn_ref[...]))
    w_hi, w_lo = _split_bf16(rw_ref[...])
    wide = _dot(hi, jnp.concatenate([w_hi, w_lo], axis=1))
    logits = wide[:, :LANES] + wide[:, LANES:] + _dot(lo, w_hi) + rb_ref[...]
    lt = logits.T
    i1, i2, w1, w2 = _route_rows(lt[0:SUBLANES], lt[EXPERT_LANE0:EXPERT_LANE0 + N_EXPERTS])

    sub16 = lax.broadcasted_iota(jnp.int32, (N_EXPERTS, 2 * sub), 0).astype(F32)
    esub = lax.broadcasted_iota(jnp.int32, (N_EXPERTS, LANES), 0)
    mlane = lax.broadcasted_iota(jnp.int32, (N_EXPERTS, LANES), 1)

    def round_up(x, step):
        return jnp.floor((x + (step - 1)) * (1.0 / step)) * step

    def starts(sizes):
        inc, k = sizes, 1
        while k < N_EXPERTS:
            inc = inc + jnp.where(esub >= k, pltpu.roll(inc, k, 0), 0.0)
            k *= 2
        return inc - sizes

    onehots, sizes = [], []
    for s in range(nsub):
        sl = slice(s * sub, (s + 1) * sub)
        sel = jnp.concatenate([i1[:, sl], i2[:, sl]], axis=1)
        at = jnp.where(sub16 == sel, 1.0, 0.0)
        cnt = jnp.sum(at, axis=1, keepdims=True)
        onehots.append(at)
        sizes.append(jnp.broadcast_to(round_up(cnt, SEG_ALIGN), (N_EXPERTS, LANES)))

    region = jnp.maximum(round_up(sum(sizes), EXPERT_STEP), float(EXPERT_FIXED))
    base = starts(region)
    meta = jnp.where(mlane == META_BASE, base, jnp.where(mlane == META_ROWS, region, 0.0))
    blk0 = mlane.astype(F32) * SEG_ALIGN
    seg_starts, at_row = [], base
    for s in range(nsub):
        seg0 = starts(sizes[s])
        seg_starts.append(seg0)
        inside = (blk0 >= seg0) & (blk0 < seg0 + sizes[s])
        dst = jnp.sum(jnp.where(inside, at_row - seg0 + blk0, 0.0), axis=0, keepdims=True)
        used = jnp.sum(jnp.where(inside, 1.0, 0.0), axis=0, keepdims=True)
        dst = jnp.where(used > 0.0, dst, float(XE_DUMP_ROW))
        meta = jnp.where((esub == s) & (mlane < SORT_BLOCKS), dst, meta)
        at_row = at_row + sizes[s]
    meta_v[...] = meta.astype(jnp.int32)
    cp = pltpu.make_async_copy(meta_v, meta_sm, sem)
    cp.start()

    jr = lax.broadcasted_iota(jnp.int32, (2 * sub, 2 * sub), 0)
    jc = lax.broadcasted_iota(jnp.int32, (2 * sub, 2 * sub), 1)
    earlier = jnp.where(jr < jc, 1.0, 0.0).astype(BF16)
    rsub = lax.broadcasted_iota(jnp.int32, (SORT_ROWS, sub), 0).astype(F32)
    for s in range(nsub):
        sl = slice(s * sub, (s + 1) * sub)
        at = onehots[s]
        rank = _dot(at.astype(BF16), earlier)
        dest = jnp.sum(at * (seg_starts[s][:, 0:1] + rank), axis=0, keepdims=True)
        p1 = rsub == dest[:, :sub]
        p2 = rsub == dest[:, sub:]
        perm = jnp.where(p1 | p2, 1.0, 0.0).astype(BF16)
        gsort_s[s] = jnp.sum(jnp.where(p1, w1[:, sl], 0.0) + jnp.where(p2, w2[:, sl], 0.0),
                             axis=1, keepdims=True)
        perm_s[s] = perm
        xs = _dot(perm, hi[sl, :]).astype(BF16)
        if s == 0:
            cp.wait()
        for b in range(SORT_BLOCKS):
            row = pl.multiple_of(meta_sm[s, b], SEG_ALIGN)
            xe_s[pl.ds(row, SEG_ALIGN), :] = xs[b * SEG_ALIGN:(b + 1) * SEG_ALIGN, :]


def _expert_group(first, xe_s, act_s, meta_sm, wg_ref, wu_ref, wd_ref):
    def hidden(j, r0, m):
        x = xe_s[pl.ds(r0, m), :]
        gt = _dot(x, wg_ref[j])
        up = _dot(x, wu_ref[j])
        return ((gt * _sigmoid(gt)) * up).astype(BF16)

    bases = [pl.multiple_of(meta_sm[first + j, META_BASE], EXPERT_STEP)
             for j in range(EXPERTS_PER_STEP)]
    for j in range(EXPERTS_PER_STEP):
        act_s[j] = hidden(j, bases[j], EXPERT_FIXED)
    for j in range(EXPERTS_PER_STEP):
        xe_s[pl.ds(bases[j], EXPERT_FIXED), :] = _dot(act_s[j], wd_ref[j]).astype(BF16)
    for j in range(EXPERTS_PER_STEP):
        def piece(c, carry, j=j):
            r0 = pl.multiple_of(bases[j] + EXPERT_FIXED + c * EXPERT_STEP, EXPERT_STEP)
            xe_s[pl.ds(r0, EXPERT_STEP), :] = _dot(hidden(j, r0, EXPERT_STEP), wd_ref[j]).astype(BF16)
            return carry
        lax.fori_loop(0, (meta_sm[first + j, META_ROWS] - EXPERT_FIXED) // EXPERT_STEP, piece, 0)


def _moe_kernel(with_kv, *refs):
    if with_kv:
        (x1_ref, p_ref, nffn_ref, rw_ref, rb_ref, wg_ref, wu_ref, wd_ref,
         nple_ref, pg_ref, pp_ref, kvn_ref, wkv_ref, kn_ref, rope_ref,
         out_ref, k_ref, v_ref, xe_s, ys_s, act_s, perm_s, gsort_s, meta_v, meta_sm, sem) = refs
    else:
        (x1_ref, p_ref, nffn_ref, rw_ref, rb_ref, wg_ref, wu_ref, wd_ref,
         nple_ref, pg_ref, pp_ref,
         out_ref, xe_s, ys_s, act_s, perm_s, gsort_s, meta_v, meta_sm, sem) = refs
    i = pl.program_id(0)
    step = pl.program_id(1)
    nsub = x1_ref.shape[0] // MOE_SUBTILE

    @pl.when((i == 0) & (step == 0))
    def _():
        xe_s[...] = jnp.zeros_like(xe_s)

    @pl.when(step == 0)
    def _():
        _dispatch(x1_ref, nffn_ref, rw_ref, rb_ref,
                  xe_s, perm_s, gsort_s, meta_v, meta_sm, sem)

    _expert_group(step * EXPERTS_PER_STEP, xe_s, act_s, meta_sm, wg_ref, wu_ref, wd_ref)

    @pl.when(step == pl.num_programs(1) - 1)
    def _():
        for s in range(nsub):
            sl = slice(s * MOE_SUBTILE, (s + 1) * MOE_SUBTILE)
            for b in range(SORT_BLOCKS):
                rows = slice(b * SEG_ALIGN, (b + 1) * SEG_ALIGN)
                src = pl.multiple_of(meta_sm[s, b], SEG_ALIGN)
                blk = xe_s[pl.ds(src, SEG_ALIGN), :].astype(F32) * gsort_s[s, rows, :]
                ys_s[rows, :] = blk.astype(BF16)
            out_ref[sl, :] = x1_ref[sl, :] + _dot_tn(perm_s[s], ys_s[...])
        x2 = out_ref[...]
        hp = _rms(x2, nple_ref[...]).astype(BF16)
        gate = _sigmoid(_dot(hp, pg_ref[...]))
        proj = _dot(p_ref[...].astype(BF16), pp_ref[...])
        x3 = x2 + gate * proj
        out_ref[...] = x3
        if with_kv:
            hk = _rms(x3, kvn_ref[...]).astype(BF16)
            kv = _dot(hk, wkv_ref[...])
            k_ref[...] = _head_norm_rope(kv[:, :LANES], kn_ref[...], _same_head(),
                                         *_rope_parts(rope_ref), two_piece=True)
            v_ref[...] = kv[:, LANES:]


def _moe_layer(x1, p_all, layer, w, kv=None):
    n, d = x1.shape
    tm = TOKEN_TILE
    nt = n // tm
    nsub = tm // MOE_SUBTILE
    tok = lambda i, e: (i, 0)
    const2 = lambda i, e: (0, 0)
    lay3 = lambda i, e: (layer, 0, 0)
    exp4 = lambda i, e: (layer, e, 0, 0)
    in_specs = [
        pl.BlockSpec((tm, d), tok),
        pl.BlockSpec((None, tm, D_PLE), lambda i, e: (layer, i, 0)),
        pl.BlockSpec((None, 1, d), lay3),
        pl.BlockSpec((None, d, LANES), lay3),
        pl.BlockSpec((None, 1, LANES), lay3),
        pl.BlockSpec((None, EXPERTS_PER_STEP, d, D_EXPERT), exp4),
        pl.BlockSpec((None, EXPERTS_PER_STEP, d, D_EXPERT), exp4),
        pl.BlockSpec((None, EXPERTS_PER_STEP, D_EXPERT, d), exp4),
        pl.BlockSpec((None, 1, d), lay3),
        pl.BlockSpec((None, d, d), lay3),
        pl.BlockSpec((None, D_PLE, d), lay3),
    ]
    args = [x1, p_all, w["norm_ffn"], w["router_w"], w["router_b"],
            w["exp_gate"], w["exp_up"], w["exp_down"], w["norm_ple"], w["ple_gate"], w["ple_proj"]]
    out_specs = [pl.BlockSpec((tm, d), tok)]
    out_shape = [jax.ShapeDtypeStruct((n, d), F32)]
    if kv is not None:
        tab_blocks = kv["rope"].shape[0] // tm
        in_specs += [
            pl.BlockSpec((1, d), const2),
            pl.BlockSpec((d, 2 * LANES), const2),
            pl.BlockSpec((1, LANES), const2),
            pl.BlockSpec((tm, 3 * LANES), lambda i, e: (i % tab_blocks, 0)),
        ]
        args += [kv["kv_norm"], kv["w_kv"], kv["k_norm"], kv["rope"]]
        out_specs += [pl.BlockSpec((tm, LANES), tok), pl.BlockSpec((tm, LANES), tok)]
        out_shape += [jax.ShapeDtypeStruct((n, LANES), F32), jax.ShapeDtypeStruct((n, LANES), F32)]
    return pl.pallas_call(
        functools.partial(_moe_kernel, kv is not None),
        grid=(nt, N_EXPERTS // EXPERTS_PER_STEP),
        in_specs=in_specs,
        out_specs=out_specs,
        out_shape=out_shape,
        scratch_shapes=[
            pltpu.VMEM((XE_DUMP_ROW + SEG_ALIGN, d), BF16),
            pltpu.VMEM((SORT_ROWS, d), BF16),
            pltpu.VMEM((EXPERTS_PER_STEP, EXPERT_FIXED, D_EXPERT), BF16),
            pltpu.VMEM((nsub, SORT_ROWS, MOE_SUBTILE), BF16),
            pltpu.VMEM((nsub, SORT_ROWS, 1), F32),
            pltpu.VMEM((N_EXPERTS, LANES), jnp.int32),
            pltpu.SMEM((N_EXPERTS, LANES), jnp.int32),
            pltpu.SemaphoreType.DMA,
        ],
        compiler_params=pltpu.CompilerParams(
            dimension_semantics=("arbitrary", "arbitrary"), vmem_limit_bytes=VMEM_LIMIT),
        name="moe_kv" if kv is not None else "moe",
    )(*args)


def _project_q(x3, nmix_ref, wq_ref, qn_ref, cos, sin_next, sin_prev, q_s):
    h = _rms(x3, nmix_ref[...]).astype(BF16)
    same_head = _same_head()
    wide = 4 * LANES
    for blk in range(D_MODEL // wide):
        q = _dot(h, wq_ref[:, blk * wide:(blk + 1) * wide].astype(BF16))
        for c in range(wide // LANES):
            cols = slice(blk * wide + c * LANES, blk * wide + (c + 1) * LANES)
            qc = _head_norm_rope(q[:, c * LANES:(c + 1) * LANES], qn_ref[...], same_head,
                                 cos, sin_next, sin_prev, two_piece=False)
            q_s[:, cols] = (qc * (HEAD_DIM ** -0.5 * LOG2_E)).astype(q_s.dtype)


def _both_orders(x):
    return x.astype(BF16), pltpu.roll(x, HEAD_DIM, 1).astype(BF16)


def _kv_order(par, g):
    return 0 if par == g else 1


def _softmax_sink(s, mask, sink):
    col0 = lax.broadcasted_iota(jnp.int32, (s.shape[0], LANES), 1) == 0
    s = jnp.where(mask, s, NEG_BIG)
    s = jnp.concatenate([jnp.where(col0, sink * LOG2_E, s[:, :LANES]), s[:, LANES:]], axis=1)
    p = jnp.exp2(s - jnp.max(s, axis=-1, keepdims=True))
    return p.astype(BF16), 1.0 / jnp.sum(p, axis=-1, keepdims=True)


def _drop_first_row(vals, tile_rows):
    row = lax.broadcasted_iota(jnp.int32, (tile_rows, vals.shape[1]), 0)
    head = jnp.where(row == 0, jnp.zeros_like(vals[:tile_rows]), vals[:tile_rows])
    return jnp.concatenate([head, vals[tile_rows:]], axis=0)


def _attn_prompt_kernel(x3_ref, k_ref, v_ref, kprev_ref, vprev_ref, nmix_ref, wq_ref, qn_ref,
                        rope_ref, sinks_ref, wo_ref,
                        x4_ref, q_s, o_s, kd_s, vd_s):
    t = pl.program_id(1)
    tq = x3_ref.shape[1]
    x3 = x3_ref[0]
    _project_q(x3, nmix_ref, wq_ref, qn_ref, *_rope_parts(rope_ref), q_s)

    for src_prev, src, dst in ((kprev_ref, k_ref, kd_s), (vprev_ref, v_ref, vd_s)):
        dst[0, 0:WINDOW, :], dst[1, 0:WINDOW, :] = _both_orders(src_prev[0])
        dst[0, WINDOW:, :], dst[1, WINDOW:, :] = _both_orders(src[0])

    qi = lax.broadcasted_iota(jnp.int32, (WINDOW, 2 * WINDOW), 0)
    ki = lax.broadcasted_iota(jnp.int32, (WINDOW, 2 * WINDOW), 1)
    band = (ki > qi) & (ki <= qi + WINDOW)
    lane = lax.broadcasted_iota(jnp.int32, (WINDOW, LANES), 1)
    first = lane < HEAD_DIM

    def q_block(j, carry):
        r0 = pl.multiple_of(j * WINDOW, WINDOW)
        mask = band & (ki >= (1 - j) * WINDOW - t * tq)
        vals = [_drop_first_row(vd_s[order, pl.ds(r0, 2 * WINDOW), :], BF16_TILE_ROWS)
                for order in range(2)]
        for c in range(D_MODEL // LANES):
            g = (2 * c) // Q_PER_KV
            cols = slice(c * LANES, (c + 1) * LANES)
            qc = q_s[pl.ds(r0, WINDOW), cols]
            outs = []
            for par in range(2):
                order = _kv_order(par, g)
                qh = jnp.where(first, qc, 0) if par == 0 else jnp.where(first, 0, qc)
                s = _dot_nt(qh, kd_s[order, pl.ds(r0, 2 * WINDOW), :])
                pr, inv = _softmax_sink(s, mask, sinks_ref[2 * c + par])
                outs.append(_dot(pr, vals[order]) * inv)
            o_s[pl.ds(r0, WINDOW), cols] = jnp.where(first, outs[0], outs[1]).astype(BF16)
        return carry

    lax.fori_loop(0, tq // WINDOW, q_block, 0, unroll=ATTN_BLOCK_UNROLL)
    x4_ref[0] = x3 + _dot(o_s[...], wo_ref[...].astype(BF16))


def _attn_prompt(x3, k, v, w, rope):
    b, seq, d = x3.shape
    tq = ATTN_TILE
    nt = seq // tq
    per = tq // WINDOW
    tile = lambda bi, ti: (bi, ti, 0)
    prev = lambda bi, ti: (bi, jnp.maximum(ti * per - 1, 0), 0)
    const2 = lambda bi, ti: (0, 0)
    tab = lambda bi, ti: (ti, 0)
    return pl.pallas_call(
        _attn_prompt_kernel,
        grid=(b, nt),
        in_specs=[
            pl.BlockSpec((1, tq, d), tile),
            pl.BlockSpec((1, tq, LANES), tile),
            pl.BlockSpec((1, tq, LANES), tile),
            pl.BlockSpec((1, WINDOW, LANES), prev),
            pl.BlockSpec((1, WINDOW, LANES), prev),
            pl.BlockSpec((1, d), const2),
            pl.BlockSpec((d, d), const2),
            pl.BlockSpec((1, LANES), const2),
            pl.BlockSpec((tq, 3 * LANES), tab),
            pl.BlockSpec(memory_space=pltpu.SMEM),
            pl.BlockSpec((d, d), const2),
        ],
        out_specs=pl.BlockSpec((1, tq, d), tile),
        out_shape=jax.ShapeDtypeStruct((b, seq, d), F32),
        scratch_shapes=[
            pltpu.VMEM((tq, d), BF16),
            pltpu.VMEM((tq, d), BF16),
            pltpu.VMEM((N_KV_HEADS, tq + WINDOW, LANES), BF16),
            pltpu.VMEM((N_KV_HEADS, tq + WINDOW, LANES), BF16),
        ],
        compiler_params=pltpu.CompilerParams(
            dimension_semantics=("arbitrary", "arbitrary"), vmem_limit_bytes=VMEM_LIMIT),
        name="attn_prompt",
    )(x3, k, v, k, v, w["norm_mix1"], w["w_q"], w["q_norm"], rope, w["sinks"], w["w_o"])


def _attn_sample_kernel(x3_ref, knew_ref, vnew_ref, kc_ref, vc_ref, nmix_ref, wq_ref, qn_ref,
                        rope_ref, sinks_ref, wo_ref,
                        x4_ref, kwin_ref, vwin_ref, q_s, o_s):
    rows = x3_ref.shape[0]
    steps = knew_ref.shape[0] // kc_ref.shape[0]
    x3 = x3_ref[...]
    _project_q(x3, nmix_ref, wq_ref, qn_ref, *_rope_parts(rope_ref), q_s)

    stack = N_HEADS * steps
    ri = lax.broadcasted_iota(jnp.int32, (stack, 2 * WINDOW), 0)
    ki = lax.broadcasted_iota(jnp.int32, (stack, 2 * WINDOW), 1)
    tq = ri % steps
    mask = ((ki < WINDOW) & (ki > tq)) | ((ki >= WINDOW) & ((ki - WINDOW) <= tq))
    head_of_row = lax.broadcasted_iota(jnp.int32, (stack, 1), 0) // steps
    sink_col = jnp.zeros((stack, 1), F32)
    for hd in range(N_HEADS):
        sink_col = jnp.where(head_of_row == hd, sinks_ref[hd], sink_col)
    lane = lax.broadcasted_iota(jnp.int32, (steps, LANES), 1)
    first = lane < HEAD_DIM
    pad = jnp.zeros((WINDOW - steps, LANES), F32)

    def to_kv_half(slab, par, g):
        return slab if par == g else pltpu.roll(slab, HEAD_DIM, 1)

    def one_seq(b):
        r0 = pl.multiple_of(b * steps, steps)
        keys = jnp.concatenate([kc_ref[b], knew_ref[pl.ds(r0, steps), :], pad], axis=0).astype(BF16)
        vals = jnp.concatenate([_drop_first_row(vc_ref[b], SUBLANES), vnew_ref[pl.ds(r0, steps), :],
                                pad], axis=0).astype(BF16)
        slabs = []
        for hd in range(N_HEADS):
            par, g = hd % 2, hd // Q_PER_KV
            qc = q_s[pl.ds(r0, steps), (hd // 2) * LANES:(hd // 2 + 1) * LANES]
            own = jnp.where(first, qc, 0.0) if par == 0 else jnp.where(first, 0.0, qc)
            slabs.append(to_kv_half(own, par, g))
        s = _dot_nt(jnp.concatenate(slabs, axis=0).astype(BF16), keys)
        pr, inv = _softmax_sink(s, mask, sink_col)
        out = _dot(pr, vals) * inv
        for c in range(D_MODEL // LANES):
            g = (2 * c) // Q_PER_KV
            even = to_kv_half(out[(2 * c) * steps:(2 * c + 1) * steps], g, 0)
            odd = to_kv_half(out[(2 * c + 1) * steps:(2 * c + 2) * steps], g, 1)
            o_s[pl.ds(r0, steps), c * LANES:(c + 1) * LANES] = jnp.where(first, even, odd)

    def seq_pair(i, carry):
        for u in range(SAMPLE_SEQ_UNROLL):
            one_seq(i * SAMPLE_SEQ_UNROLL + u)
        return carry

    lax.fori_loop(0, rows // steps // SAMPLE_SEQ_UNROLL, seq_pair, 0)
    x4_ref[...] = x3 + _dot(o_s[...].astype(BF16), wo_ref[...].astype(BF16))

    seqs = kc_ref.shape[0]
    for cache_ref, new_ref, win_ref in ((kc_ref, knew_ref, kwin_ref), (vc_ref, vnew_ref, vwin_ref)):
        win_ref[:, 0:WINDOW - steps, :] = cache_ref[:, steps:, :]
        win_ref[:, WINDOW - steps:, :] = new_ref[...].reshape(seqs, steps, LANES)


def _attn_sample(x3, knew, vnew, kcache, vcache, w, rope, steps):
    n, d = x3.shape
    sb = SAMPLE_SEQ_BLOCK
    rows = sb * steps
    tok = lambda i: (i, 0)
    const2 = lambda i: (0, 0)
    return pl.pallas_call(
        _attn_sample_kernel,
        grid=(n // rows,),
        in_specs=[
            pl.BlockSpec((rows, d), tok),
            pl.BlockSpec((rows, LANES), tok),
            pl.BlockSpec((rows, LANES), tok),
            pl.BlockSpec((sb, WINDOW, LANES), lambda i: (i, 0, 0)),
            pl.BlockSpec((sb, WINDOW, LANES), lambda i: (i, 0, 0)),
            pl.BlockSpec((1, d), const2),
            pl.BlockSpec((d, d), const2),
            pl.BlockSpec((1, LANES), const2),
            pl.BlockSpec((rows, 3 * LANES), tok),
            pl.BlockSpec(memory_space=pltpu.SMEM),
            pl.BlockSpec((d, d), const2),
        ],
        out_specs=[
            pl.BlockSpec((rows, d), tok),
            pl.BlockSpec((sb, WINDOW, LANES), lambda i: (i, 0, 0)),
            pl.BlockSpec((sb, WINDOW, LANES), lambda i: (i, 0, 0)),
        ],
        out_shape=[
            jax.ShapeDtypeStruct((n, d), F32),
            jax.ShapeDtypeStruct(kcache.shape, F32),
            jax.ShapeDtypeStruct(vcache.shape, F32),
        ],
        scratch_shapes=[
            pltpu.VMEM((rows, d), F32),
            pltpu.VMEM((rows, d), F32),
        ],
        compiler_params=pltpu.CompilerParams(
            dimension_semantics=("arbitrary",), vmem_limit_bytes=VMEM_LIMIT),
        name="attn_sample",
    )(x3, knew, vnew, kcache, vcache, w["norm_mix1"], w["w_q"], w["q_norm"], rope,
      w["sinks"], w["w_o"])


def _rope_table(pos):
    half = ROPE_DIM // 2
    dim = jnp.arange(LANES, dtype=jnp.int32) % HEAD_DIM
    inv = jnp.float32(ROPE_THETA) ** (-((dim % half).astype(jnp.float32) * 2.0 / ROPE_DIM))
    inv = jnp.where(dim < ROPE_DIM, inv, 0.0)
    ang = pos.astype(jnp.float32)[:, None] * inv[None, :]
    cos, sin = jnp.cos(ang), jnp.sin(ang)
    s_next = jnp.where(dim < half, -sin, 0.0)
    s_prev = jnp.where((dim >= half) & (dim < ROPE_DIM), sin, 0.0)
    return jnp.concatenate([cos, s_next, s_prev], axis=1)


def kernel(x_prompt, x_sample, p_prompt, p_sample, state_pool, cache_k_win, cache_v_win, norm_mix, norm_ffn, norm_ple, pool_w, pool_scale, kv_norm, w_kv, k_norm, w_q, q_norm, sinks, w_o, router_g_w, router_g_b, router_e_w, router_e_b, exp_gate, exp_up, exp_down, ple_gate, ple_proj):
    b, seq, d = x_prompt.shape
    sb, steps, _ = x_sample.shape
    depth = norm_mix.shape[0]
    row = lambda v: v.reshape(1, -1)

    gap_w = jnp.zeros((depth, d, EXPERT_LANE0 - N_GROUPS), F32)
    tail_w = jnp.zeros((depth, d, LANES - EXPERT_LANE0 - N_EXPERTS), F32)
    router_w = jnp.concatenate(
        [router_g_w, gap_w, router_e_w.reshape(depth, d, N_EXPERTS), tail_w], axis=2)
    router_b = jnp.concatenate(
        [router_g_b, gap_w[:, 0], router_e_b.reshape(depth, N_EXPERTS), tail_w[:, 0]], axis=1)
    moe_w = {
        "norm_ffn": norm_ffn.reshape(depth, 1, d), "router_w": router_w,
        "router_b": router_b.reshape(depth, 1, LANES),
        "exp_gate": exp_gate.astype(BF16), "exp_up": exp_up.astype(BF16),
        "exp_down": exp_down.astype(BF16), "norm_ple": norm_ple.reshape(depth, 1, d),
        "ple_gate": ple_gate.astype(BF16), "ple_proj": ple_proj.astype(BF16),
    }
    two = lambda a: jnp.concatenate([a, a])
    attn_w = {
        "norm_mix1": row(norm_mix[1]), "w_q": w_q[0], "q_norm": row(two(q_norm[0])),
        "sinks": sinks[0], "w_o": w_o[0],
    }
    pw = pool_w[0]
    nmix0 = row(norm_mix[0])
    pscale = row(pool_scale[0])

    rope_p = _rope_table(jnp.arange(seq, dtype=jnp.int32))
    rope_s = jnp.tile(_rope_table(PAST_LEN + jnp.arange(steps, dtype=jnp.int32)), (sb, 1))
    kv_w = {"kv_norm": row(kv_norm), "w_kv": w_kv.astype(BF16), "k_norm": row(two(k_norm))}

    pp_all = p_prompt.reshape(depth, b * seq, D_PLE)
    x1p, pool16 = _mixer0_prompt(x_prompt, nmix0, pw, pscale)
    x3p, kp, vp = _moe_layer(x1p.reshape(b * seq, d), pp_all, 0, moe_w, dict(kv_w, rope=rope_p))
    kp = kp.reshape(b, seq, LANES)
    vp = vp.reshape(b, seq, LANES)
    x4p = _attn_prompt(x3p.reshape(b, seq, d), kp, vp, attn_w, rope_p)
    y_prompt = _moe_layer(x4p.reshape(b * seq, d), pp_all, 1, moe_w)[0].reshape(b, seq, d)
    pool_prompt = pool16[None, :, CTX_ROWS - POOL_CTX:, :]
    k_win_prompt = kp[:, seq - WINDOW:].reshape(b, WINDOW, N_KV_HEADS, HEAD_DIM)
    v_win_prompt = vp[:, seq - WINDOW:].reshape(b, WINDOW, N_KV_HEADS, HEAD_DIM)

    ps_all = p_sample.reshape(depth, sb * steps, D_PLE)
    x1s, pool_s = _mixer0_sample(x_sample, state_pool, nmix0, pw, pscale)
    x3s, ks, vs = _moe_layer(x1s.reshape(sb * steps, d), ps_all, 0, moe_w, dict(kv_w, rope=rope_s))
    kc = cache_k_win.reshape(sb, WINDOW, LANES)
    vc = cache_v_win.reshape(sb, WINDOW, LANES)
    x4s, kwin, vwin = _attn_sample(x3s, ks, vs, kc, vc, attn_w, rope_s, steps)
    y_sample = _moe_layer(x4s, ps_all, 1, moe_w)[0].reshape(sb, steps, d)
    pool_sample = pool_s[None]
    k_win_sample = kwin.reshape(cache_k_win.shape)
    v_win_sample = vwin.reshape(cache_v_win.shape)

    return (y_prompt, y_sample, pool_prompt, pool_sample,
            k_win_prompt, v_win_prompt, k_win_sample, v_win_sample)
```

```python
import functools

import jax
import jax.numpy as jnp
from jax import lax
from jax.experimental import pallas as pl
from jax.experimental.pallas import tpu as pltpu

D_MODEL = 1024
PAST_LEN = 8192
POOL_WINDOWS = (2, 4, 8, 16)
POOL_GROUP_DIM = D_MODEL // len(POOL_WINDOWS)
POOL_CTX = max(POOL_WINDOWS) - 1
assert all(win == 2 ** (g + 1) for g, win in enumerate(POOL_WINDOWS))
HEAD_DIM = 64
N_HEADS = D_MODEL // HEAD_DIM
N_KV_HEADS = 2
Q_PER_KV = N_HEADS // N_KV_HEADS
WINDOW = 128
ROPE_DIM = HEAD_DIM // 4
ROPE_THETA = 500000.0
N_GROUPS = 4
E_PER_GROUP = 4
N_EXPERTS = N_GROUPS * E_PER_GROUP
D_EXPERT = 256
D_PLE = 256
EPS = 1e-6

LANES = 128
SUBLANES = 8
CTX_ROWS = 16
TOKEN_TILE = 1024
ATTN_TILE = 1024
ATTN_BLOCK_UNROLL = 4
POOL_SEQ_BLOCK = 16
SAMPLE_SEQ_BLOCK = 32
SAMPLE_SEQ_UNROLL = 8
VMEM_LIMIT = 60 * 1024 * 1024
NEG_BIG = -1e30
LOG2_E = 1.4426950408889634
F32 = jnp.float32
BF16 = jnp.bfloat16

EXPERT_LANE0 = SUBLANES
MOE_SUBTILE = 256
BF16_TILE_ROWS = 2 * SUBLANES
SEG_ALIGN = BF16_TILE_ROWS
SORT_ROWS = 768
EXPERT_FIXED = 192
EXPERT_STEP = 32
EXPERTS_PER_STEP = 4
SORT_BLOCKS = SORT_ROWS // SEG_ALIGN
MOE_SUBTILES = TOKEN_TILE // MOE_SUBTILE
XE_DUMP_ROW = (2 * TOKEN_TILE + MOE_SUBTILES * N_EXPERTS * (SEG_ALIGN - 1)
               + N_EXPERTS * (EXPERT_STEP - SEG_ALIGN + EXPERT_FIXED))
META_BASE, META_ROWS = 64, 65
assert SORT_ROWS >= 2 * MOE_SUBTILE + N_EXPERTS * (SEG_ALIGN - 1) and SORT_ROWS % LANES == 0
assert TOKEN_TILE % MOE_SUBTILE == 0 and MOE_SUBTILES <= N_EXPERTS and SORT_BLOCKS <= META_BASE
assert EXPERT_FIXED % EXPERT_STEP == 0 and EXPERT_STEP % SEG_ALIGN == 0
assert N_EXPERTS % EXPERTS_PER_STEP == 0 and XE_DUMP_ROW % SEG_ALIGN == 0


def _rms(x, g):
    return x * lax.rsqrt(jnp.mean(x * x, axis=-1, keepdims=True) + EPS) * g


def _dot(a, b):
    return jnp.dot(a, b, preferred_element_type=F32)


def _dot_nt(a, b):
    return lax.dot_general(a, b, (((1,), (1,)), ((), ())), preferred_element_type=F32)


def _dot_tn(a, b):
    return lax.dot_general(a, b, (((0,), (0,)), ((), ())), preferred_element_type=F32)


def _split_bf16(x):
    hi = x.astype(BF16)
    return hi, (x - hi.astype(F32)).astype(BF16)


def _dot3(a_hi, a_lo, b):
    b_hi, b_lo = _split_bf16(b)
    return _dot(a_hi, b_hi) + _dot(a_lo, b_hi) + _dot(a_hi, b_lo)


def _sigmoid(x):
    return 1.0 / (1.0 + jnp.exp(-x))


def _rope_parts(rope_ref):
    return rope_ref[:, 0:LANES], rope_ref[:, LANES:2 * LANES], rope_ref[:, 2 * LANES:3 * LANES]


def _same_head():
    hi = lax.broadcasted_iota(jnp.int32, (LANES, LANES), 0) // HEAD_DIM
    hj = lax.broadcasted_iota(jnp.int32, (LANES, LANES), 1) // HEAD_DIM
    return jnp.where(hi == hj, 1.0, 0.0).astype(BF16)


def _head_norm_rope(x, gain, same_head, cos, sin_next, sin_prev, two_piece):
    sq_hi, sq_lo = _split_bf16(x * x)
    ssq = _dot(sq_hi, same_head)
    if two_piece:
        ssq = ssq + _dot(sq_lo, same_head)
    y = x * lax.rsqrt(ssq * (1.0 / HEAD_DIM) + EPS) * gain
    half = ROPE_DIM // 2
    return (y * cos + pltpu.roll(y, LANES - half, 1) * sin_next
            + pltpu.roll(y, half, 1) * sin_prev)


def _pool_mix(h, window_sum, pos, x, pw_ref, pscale_ref, store):
    for g, win in enumerate(POOL_WINDOWS):
        cols = slice(g * POOL_GROUP_DIM, (g + 1) * POOL_GROUP_DIM)
        hg = h[..., cols]
        inv_cnt = 1.0 / jnp.minimum(win, pos + 1).astype(F32)
        pooled = (window_sum(g, cols) * inv_cnt - hg).reshape(-1, POOL_GROUP_DIM)
        hi, lo = _split_bf16(pooled)
        mixed = _dot3(hi, lo, pw_ref[g])
        store(cols, x[..., cols] + mixed.reshape(hg.shape) * pscale_ref[:, cols])


def _mixer0_prompt_kernel(x_ref, xprev_ref, nmix_ref, pw_ref, pscale_ref,
                          x1_ref, pool_ref, hs_ref, ua_ref, ub_ref):
    t = pl.program_id(1)
    tm = x_ref.shape[1]
    x = x_ref[0]
    h = _rms(x, nmix_ref[...])
    hp = _rms(xprev_ref[0], nmix_ref[...])
    top = 2 * CTX_ROWS
    hs_ref[0:CTX_ROWS, :] = jnp.zeros((CTX_ROWS, x.shape[1]), F32)
    hs_ref[CTX_ROWS:top, :] = jnp.where(t > 0, hp, 0.0)
    hs_ref[top:, :] = h
    ua_ref[0:SUBLANES, :] = jnp.zeros((SUBLANES, x.shape[1]), F32)
    ub_ref[0:SUBLANES, :] = jnp.zeros((SUBLANES, x.shape[1]), F32)

    n = tm + top - SUBLANES
    src, level_of = hs_ref, []
    for g, win in enumerate(POOL_WINDOWS):
        dst = ua_ref if g % 2 == 0 else ub_ref
        c0 = g * POOL_GROUP_DIM
        back = win // 2
        dst[SUBLANES:, c0:] = src[SUBLANES:, c0:] + src[SUBLANES - back:SUBLANES - back + n, c0:]
        level_of.append(dst)
        src = dst

    def window_sum(g, cols):
        return level_of[g][top:, cols]

    def store(cols, val):
        x1_ref[0, :, cols] = val

    pos = t * tm + lax.broadcasted_iota(jnp.int32, (tm, 1), 0)
    _pool_mix(h, window_sum, pos, x, pw_ref, pscale_ref, store)

    @pl.when(t == pl.num_programs(1) - 1)
    def _():
        pool_ref[0] = hs_ref[tm + top - CTX_ROWS:tm + top, :]


def _mixer0_sample_kernel(x_ref, ctx_ref, nmix_ref, pw_ref, pscale_ref,
                          x1_ref, pool_ref, hs_ref):
    x = x_ref[...]
    steps = x.shape[1]
    h = _rms(x, nmix_ref[...])
    hs_ref[:, CTX_ROWS - POOL_CTX:CTX_ROWS, :] = ctx_ref[...]
    hs_ref[:, CTX_ROWS:, :] = h
    pool_ref[:, 0:POOL_CTX - steps, :] = ctx_ref[:, steps:, :]
    pool_ref[:, POOL_CTX - steps:, :] = h

    def window_sum(g, cols):
        s = h[..., cols]
        for j in range(1, POOL_WINDOWS[g]):
            s = s + hs_ref[:, CTX_ROWS - j:CTX_ROWS - j + steps, cols]
        return s

    def store(cols, val):
        x1_ref[:, :, cols] = val

    pos = PAST_LEN + lax.broadcasted_iota(jnp.int32, (1, steps, 1), 1)
    _pool_mix(h, window_sum, pos, x, pw_ref, pscale_ref, store)


def _mixer0_prompt(x, nmix, pw, pscale):
    b, seq, d = x.shape
    tm = TOKEN_TILE
    nt = seq // tm
    per = tm // CTX_ROWS
    const2 = lambda bi, ti: (0, 0)
    return pl.pallas_call(
        _mixer0_prompt_kernel,
        grid=(b, nt),
        in_specs=[
            pl.BlockSpec((1, tm, d), lambda bi, ti: (bi, ti, 0)),
            pl.BlockSpec((1, CTX_ROWS, d), lambda bi, ti: (bi, jnp.maximum(ti * per - 1, 0), 0)),
            pl.BlockSpec((1, d), const2),
            pl.BlockSpec(pw.shape, lambda bi, ti: (0, 0, 0)),
            pl.BlockSpec((1, d), const2),
        ],
        out_specs=[
            pl.BlockSpec((1, tm, d), lambda bi, ti: (bi, ti, 0)),
            pl.BlockSpec((1, CTX_ROWS, d), lambda bi, ti: (bi, 0, 0)),
        ],
        out_shape=[
            jax.ShapeDtypeStruct((b, seq, d), F32),
            jax.ShapeDtypeStruct((b, CTX_ROWS, d), F32),
        ],
        scratch_shapes=[pltpu.VMEM((tm + 2 * CTX_ROWS, d), F32)] * 3,
        compiler_params=pltpu.CompilerParams(
            dimension_semantics=("arbitrary", "arbitrary"), vmem_limit_bytes=VMEM_LIMIT),
        name="mixer0_prompt",
    )(x, x, nmix, pw, pscale)


def _mixer0_sample(x, ctx_layers, nmix, pw, pscale):
    b, steps, d = x.shape
    sb = POOL_SEQ_BLOCK
    const2 = lambda i: (0, 0)
    return pl.pallas_call(
        _mixer0_sample_kernel,
        grid=(b // sb,),
        in_specs=[
            pl.BlockSpec((sb, steps, d), lambda i: (i, 0, 0)),
            pl.BlockSpec((None, sb, POOL_CTX, d), lambda i: (0, i, 0, 0)),
            pl.BlockSpec((1, d), const2),
            pl.BlockSpec(pw.shape, lambda i: (0, 0, 0)),
            pl.BlockSpec((1, d), const2),
        ],
        out_specs=[
            pl.BlockSpec((sb, steps, d), lambda i: (i, 0, 0)),
            pl.BlockSpec((sb, POOL_CTX, d), lambda i: (i, 0, 0)),
        ],
        out_shape=[
            jax.ShapeDtypeStruct((b, steps, d), F32),
            jax.ShapeDtypeStruct((b, POOL_CTX, d), F32),
        ],
        scratch_shapes=[pltpu.VMEM((sb, CTX_ROWS + steps, d), F32)],
        compiler_params=pltpu.CompilerParams(
            dimension_semantics=("arbitrary",), vmem_limit_bytes=VMEM_LIMIT),
        name="mixer0_sample",
    )(x, ctx_layers, nmix, pw, pscale)


def _route_rows(glog, elog):
    sub8 = lax.broadcasted_iota(jnp.int32, glog.shape, 0).astype(F32)
    is_grp = sub8 < N_GROUPS
    gmax = jnp.max(jnp.where(is_grp, glog, NEG_BIG), axis=0, keepdims=True)
    gsum = jnp.sum(jnp.where(is_grp, jnp.exp(jnp.minimum(glog - gmax, 0.0)), 0.0),
                   axis=0, keepdims=True)
    g_w = 1.0 / gsum
    g_sel = jnp.min(jnp.where(is_grp & (glog == gmax), sub8, float(N_GROUPS)),
                    axis=0, keepdims=True)
    sub16 = lax.broadcasted_iota(jnp.int32, elog.shape, 0).astype(F32)
    lo = E_PER_GROUP * g_sel
    in_grp = (sub16 >= lo) & (sub16 < lo + E_PER_GROUP)
    none = float(N_EXPERTS)
    v1 = jnp.max(jnp.where(in_grp, elog, NEG_BIG), axis=0, keepdims=True)
    i1 = jnp.min(jnp.where(in_grp & (elog == v1), sub16, none), axis=0, keepdims=True)
    rest = in_grp & (sub16 != i1)
    v2 = jnp.max(jnp.where(rest, elog, NEG_BIG), axis=0, keepdims=True)
    i2 = jnp.min(jnp.where(rest & (elog == v2), sub16, none), axis=0, keepdims=True)
    e2 = jnp.exp(v2 - v1)
    den = 1.0 + e2
    return i1, i2, (1.0 / den) * g_w, (e2 / den) * g_w


def _dispatch(x1_ref, nffn_ref, rw_ref, rb_ref,
              xe_s, perm_s, gsort_s, meta_v, meta_sm, sem):
    tm = x1_ref.shape[0]
    sub = MOE_SUBTILE
    nsub = tm // sub
    hi, lo = _split_bf16(_rms(x1_ref[...], nffn_ref[...]))
    w_hi, w_lo = _split_bf16(rw_ref[...])
    wide = _dot(hi, jnp.concatenate([w_hi, w_lo], axis=1))
    logits = wide[:, :LANES] + wide[:, LANES:] + _dot(lo, w_hi) + rb_ref[...]
    lt = logits.T
    i1, i2, w1, w2 = _route_rows(lt[0:SUBLANES], lt[EXPERT_LANE0:EXPERT_LANE0 + N_EXPERTS])

    sub16 = lax.broadcasted_iota(jnp.int32, (N_EXPERTS, 2 * sub), 0).astype(F32)
    esub = lax.broadcasted_iota(jnp.int32, (N_EXPERTS, LANES), 0)
    mlane = lax.broadcasted_iota(jnp.int32, (N_EXPERTS, LANES), 1)

    def round_up(x, step):
        return jnp.floor((x + (step - 1)) * (1.0 / step)) * step

    def starts(sizes):
        inc, k = sizes, 1
        while k < N_EXPERTS:
            inc = inc + jnp.where(esub >= k, pltpu.roll(inc, k, 0), 0.0)
            k *= 2
        return inc - sizes

    onehots, sizes = [], []
    for s in range(nsub):
        sl = slice(s * sub, (s + 1) * sub)
        sel = jnp.concatenate([i1[:, sl], i2[:, sl]], axis=1)
        at = jnp.where(sub16 == sel, 1.0, 0.0)
        cnt = jnp.sum(at, axis=1, keepdims=True)
        onehots.append(at)
        sizes.append(jnp.broadcast_to(round_up(cnt, SEG_ALIGN), (N_EXPERTS, LANES)))

    region = jnp.maximum(round_up(sum(sizes), EXPERT_STEP), float(EXPERT_FIXED))
    base = starts(region)
    meta = jnp.where(mlane == META_BASE, base, jnp.where(mlane == META_ROWS, region, 0.0))
    blk0 = mlane.astype(F32) * SEG_ALIGN
    seg_starts, at_row = [], base
    for s in range(nsub):
        seg0 = starts(sizes[s])
        seg_starts.append(seg0)
        inside = (blk0 >= seg0) & (blk0 < seg0 + sizes[s])
        dst = jnp.sum(jnp.where(inside, at_row - seg0 + blk0, 0.0), axis=0, keepdims=True)
        used = jnp.sum(jnp.where(inside, 1.0, 0.0), axis=0, keepdims=True)
        dst = jnp.where(used > 0.0, dst, float(XE_DUMP_ROW))
        meta = jnp.where((esub == s) & (mlane < SORT_BLOCKS), dst, meta)
        at_row = at_row + sizes[s]
    meta_v[...] = meta.astype(jnp.int32)
    cp = pltpu.make_async_copy(meta_v, meta_sm, sem)
    cp.start()

    jr = lax.broadcasted_iota(jnp.int32, (2 * sub, 2 * sub), 0)
    jc = lax.broadcasted_iota(jnp.int32, (2 * sub, 2 * sub), 1)
    earlier = jnp.where(jr < jc, 1.0, 0.0).astype(BF16)
    rsub = lax.broadcasted_iota(jnp.int32, (SORT_ROWS, sub), 0).astype(F32)
    for s in range(nsub):
        sl = slice(s * sub, (s + 1) * sub)
        at = onehots[s]
        rank = _dot(at.astype(BF16), earlier)
        dest = jnp.sum(at * (seg_starts[s][:, 0:1] + rank), axis=0, keepdims=True)
        p1 = rsub == dest[:, :sub]
        p2 = rsub == dest[:, sub:]
        perm = jnp.where(p1 | p2, 1.0, 0.0).astype(BF16)
        gsort_s[s] = jnp.sum(jnp.where(p1, w1[:, sl], 0.0) + jnp.where(p2, w2[:, sl], 0.0),
                             axis=1, keepdims=True)
        perm_s[s] = perm
        xs = _dot(perm, hi[sl, :]).astype(BF16)
        if s == 0:
            cp.wait()
        for b in range(SORT_BLOCKS):
            row = pl.multiple_of(meta_sm[s, b], SEG_ALIGN)
            xe_s[pl.ds(row, SEG_ALIGN), :] = xs[b * SEG_ALIGN:(b + 1) * SEG_ALIGN, :]


def _expert_group(first, xe_s, act_s, meta_sm, wg_ref, wu_ref, wd_ref):
    def hidden(j, r0, m):
        x = xe_s[pl.ds(r0, m), :]
        gt = _dot(x, wg_ref[j])
        up = _dot(x, wu_ref[j])
        return ((gt * _sigmoid(gt)) * up).astype(BF16)

    bases = [pl.multiple_of(meta_sm[first + j, META_BASE], EXPERT_STEP)
             for j in range(EXPERTS_PER_STEP)]
    for j in range(EXPERTS_PER_STEP):
        act_s[j] = hidden(j, bases[j], EXPERT_FIXED)
    for j in range(EXPERTS_PER_STEP):
        xe_s[pl.ds(bases[j], EXPERT_FIXED), :] = _dot(act_s[j], wd_ref[j]).astype(BF16)
    for j in range(EXPERTS_PER_STEP):
        def piece(c, carry, j=j):
            r0 = pl.multiple_of(bases[j] + EXPERT_FIXED + c * EXPERT_STEP, EXPERT_STEP)
            xe_s[pl.ds(r0, EXPERT_STEP), :] = _dot(hidden(j, r0, EXPERT_STEP), wd_ref[j]).astype(BF16)
            return carry
        lax.fori_loop(0, (meta_sm[first + j, META_ROWS] - EXPERT_FIXED) // EXPERT_STEP, piece, 0)


def _moe_kernel(with_kv, *refs):
    if with_kv:
        (x1_ref, p_ref, nffn_ref, rw_ref, rb_ref, wg_ref, wu_ref, wd_ref,
         nple_ref, pg_ref, pp_ref, kvn_ref, wkv_ref, kn_ref, rope_ref,
         out_ref, k_ref, v_ref, xe_s, ys_s, act_s, perm_s, gsort_s, meta_v, meta_sm, sem) = refs
    else:
        (x1_ref, p_ref, nffn_ref, rw_ref, rb_ref, wg_ref, wu_ref, wd_ref,
         nple_ref, pg_ref, pp_ref,
         out_ref, xe_s, ys_s, act_s, perm_s, gsort_s, meta_v, meta_sm, sem) = refs
    i = pl.program_id(0)
    step = pl.program_id(1)
    nsub = x1_ref.shape[0] // MOE_SUBTILE

    @pl.when((i == 0) & (step == 0))
    def _():
        xe_s[...] = jnp.zeros_like(xe_s)

    @pl.when(step == 0)
    def _():
        _dispatch(x1_ref, nffn_ref, rw_ref, rb_ref,
                  xe_s, perm_s, gsort_s, meta_v, meta_sm, sem)

    _expert_group(step * EXPERTS_PER_STEP, xe_s, act_s, meta_sm, wg_ref, wu_ref, wd_ref)

    @pl.when(step == pl.num_programs(1) - 1)
    def _():
        for s in range(nsub):
            sl = slice(s * MOE_SUBTILE, (s + 1) * MOE_SUBTILE)
            for b in range(SORT_BLOCKS):
                rows = slice(b * SEG_ALIGN, (b + 1) * SEG_ALIGN)
                src = pl.multiple_of(meta_sm[s, b], SEG_ALIGN)
                blk = xe_s[pl.ds(src, SEG_ALIGN), :].astype(F32) * gsort_s[s, rows, :]
                ys_s[rows, :] = blk.astype(BF16)
            out_ref[sl, :] = x1_ref[sl, :] + _dot_tn(perm_s[s], ys_s[...])
        x2 = out_ref[...]
        hp = _rms(x2, nple_ref[...]).astype(BF16)
        gate = _sigmoid(_dot(hp, pg_ref[...]))
        proj = _dot(p_ref[...].astype(BF16), pp_ref[...])
        x3 = x2 + gate * proj
        out_ref[...] = x3
        if with_kv:
            hk = _rms(x3, kvn_ref[...]).astype(BF16)
            kv = _dot(hk, wkv_ref[...])
            k_ref[...] = _head_norm_rope(kv[:, :LANES], kn_ref[...], _same_head(),
                                         *_rope_parts(rope_ref), two_piece=True)
            v_ref[...] = kv[:, LANES:]


def _moe_layer(x1, p_all, layer, w, kv=None):
    n, d = x1.shape
    tm = TOKEN_TILE
    nt = n // tm
    nsub = tm // MOE_SUBTILE
    tok = lambda i, e: (i, 0)
    const2 = lambda i, e: (0, 0)
    lay3 = lambda i, e: (layer, 0, 0)
    exp4 = lambda i, e: (layer, e, 0, 0)
    in_specs = [
        pl.BlockSpec((tm, d), tok),
        pl.BlockSpec((None, tm, D_PLE), lambda i, e: (layer, i, 0)),
        pl.BlockSpec((None, 1, d), lay3),
        pl.BlockSpec((None, d, LANES), lay3),
        pl.BlockSpec((None, 1, LANES), lay3),
        pl.BlockSpec((None, EXPERTS_PER_STEP, d, D_EXPERT), exp4),
        pl.BlockSpec((None, EXPERTS_PER_STEP, d, D_EXPERT), exp4),
        pl.BlockSpec((None, EXPERTS_PER_STEP, D_EXPERT, d), exp4),
        pl.BlockSpec((None, 1, d), lay3),
        pl.BlockSpec((None, d, d), lay3),
        pl.BlockSpec((None, D_PLE, d), lay3),
    ]
    args = [x1, p_all, w["norm_ffn"], w["router_w"], w["router_b"],
            w["exp_gate"], w["exp_up"], w["exp_down"], w["norm_ple"], w["ple_gate"], w["ple_proj"]]
    out_specs = [pl.BlockSpec((tm, d), tok)]
    out_shape = [jax.ShapeDtypeStruct((n, d), F32)]
    if kv is not None:
        tab_blocks = kv["rope"].shape[0] // tm
        in_specs += [
            pl.BlockSpec((1, d), const2),
            pl.BlockSpec((d, 2 * LANES), const2),
            pl.BlockSpec((1, LANES), const2),
            pl.BlockSpec((tm, 3 * LANES), lambda i, e: (i % tab_blocks, 0)),
        ]
        args += [kv["kv_norm"], kv["w_kv"], kv["k_norm"], kv["rope"]]
        out_specs += [pl.BlockSpec((tm, LANES), tok), pl.BlockSpec((tm, LANES), tok)]
        out_shape += [jax.ShapeDtypeStruct((n, LANES), F32), jax.ShapeDtypeStruct((n, LANES), F32)]
    return pl.pallas_call(
        functools.partial(_moe_kernel, kv is not None),
        grid=(nt, N_EXPERTS // EXPERTS_PER_STEP),
        in_specs=in_specs,
        out_specs=out_specs,
        out_shape=out_shape,
        scratch_shapes=[
            pltpu.VMEM((XE_DUMP_ROW + SEG_ALIGN, d), BF16),
            pltpu.VMEM((SORT_ROWS, d), BF16),
            pltpu.VMEM((EXPERTS_PER_STEP, EXPERT_FIXED, D_EXPERT), BF16),
            pltpu.VMEM((nsub, SORT_ROWS, MOE_SUBTILE), BF16),
            pltpu.VMEM((nsub, SORT_ROWS, 1), F32),
            pltpu.VMEM((N_EXPERTS, LANES), jnp.int32),
            pltpu.SMEM((N_EXPERTS, LANES), jnp.int32),
            pltpu.SemaphoreType.DMA,
        ],
        compiler_params=pltpu.CompilerParams(
            dimension_semantics=("arbitrary", "arbitrary"), vmem_limit_bytes=VMEM_LIMIT),
        name="moe_kv" if kv is not None else "moe",
    )(*args)


def _project_q(x3, nmix_ref, wq_ref, qn_ref, cos, sin_next, sin_prev, q_s):
    h = _rms(x3, nmix_ref[...]).astype(BF16)
    same_head = _same_head()
    wide = 4 * LANES
    for blk in range(D_MODEL // wide):
        q = _dot(h, wq_ref[:, blk * wide:(blk + 1) * wide].astype(BF16))
        for c in range(wide // LANES):
            cols = slice(blk * wide + c * LANES, blk * wide + (c + 1) * LANES)
            qc = _head_norm_rope(q[:, c * LANES:(c + 1) * LANES], qn_ref[...], same_head,
                                 cos, sin_next, sin_prev, two_piece=False)
            q_s[:, cols] = (qc * (HEAD_DIM ** -0.5 * LOG2_E)).astype(q_s.dtype)


def _both_orders(x):
    return x.astype(BF16), pltpu.roll(x, HEAD_DIM, 1).astype(BF16)


def _kv_order(par, g):
    return 0 if par == g else 1


def _softmax_sink(s, mask, sink):
    col0 = lax.broadcasted_iota(jnp.int32, (s.shape[0], LANES), 1) == 0
    s = jnp.where(mask, s, NEG_BIG)
    s = jnp.concatenate([jnp.where(col0, sink * LOG2_E, s[:, :LANES]), s[:, LANES:]], axis=1)
    p = jnp.exp2(s - jnp.max(s, axis=-1, keepdims=True))
    return p.astype(BF16), 1.0 / jnp.sum(p, axis=-1, keepdims=True)


def _drop_first_row(vals, tile_rows):
    row = lax.broadcasted_iota(jnp.int32, (tile_rows, vals.shape[1]), 0)
    head = jnp.where(row == 0, jnp.zeros_like(vals[:tile_rows]), vals[:tile_rows])
    return jnp.concatenate([head, vals[tile_rows:]], axis=0)


def _attn_prompt_kernel(x3_ref, k_ref, v_ref, kprev_ref, vprev_ref, nmix_ref, wq_ref, qn_ref,
                        rope_ref, sinks_ref, wo_ref,
                        x4_ref, q_s, o_s, kd_s, vd_s):
    t = pl.program_id(1)
    tq = x3_ref.shape[1]
    x3 = x3_ref[0]
    _project_q(x3, nmix_ref, wq_ref, qn_ref, *_rope_parts(rope_ref), q_s)

    for src_prev, src, dst in ((kprev_ref, k_ref, kd_s), (vprev_ref, v_ref, vd_s)):
        dst[0, 0:WINDOW, :], dst[1, 0:WINDOW, :] = _both_orders(src_prev[0])
        dst[0, WINDOW:, :], dst[1, WINDOW:, :] = _both_orders(src[0])

    qi = lax.broadcasted_iota(jnp.int32, (WINDOW, 2 * WINDOW), 0)
    ki = lax.broadcasted_iota(jnp.int32, (WINDOW, 2 * WINDOW), 1)
    band = (ki > qi) & (ki <= qi + WINDOW)
    lane = lax.broadcasted_iota(jnp.int32, (WINDOW, LANES), 1)
    first = lane < HEAD_DIM

    def q_block(j, carry):
        r0 = pl.multiple_of(j * WINDOW, WINDOW)
        mask = band & (ki >= (1 - j) * WINDOW - t * tq)
        vals = [_drop_first_row(vd_s[order, pl.ds(r0, 2 * WINDOW), :], BF16_TILE_ROWS)
                for order in range(2)]
        for c in range(D_MODEL // LANES):
            g = (2 * c) // Q_PER_KV
            cols = slice(c * LANES, (c + 1) * LANES)
            qc = q_s[pl.ds(r0, WINDOW), cols]
            outs = []
            for par in range(2):
                order = _kv_order(par, g)
                qh = jnp.where(first, qc, 0) if par == 0 else jnp.where(first, 0, qc)
                s = _dot_nt(qh, kd_s[order, pl.ds(r0, 2 * WINDOW), :])
                pr, inv = _softmax_sink(s, mask, sinks_ref[2 * c + par])
                outs.append(_dot(pr, vals[order]) * inv)
            o_s[pl.ds(r0, WINDOW), cols] = jnp.where(first, outs[0], outs[1]).astype(BF16)
        return carry

    lax.fori_loop(0, tq // WINDOW, q_block, 0, unroll=ATTN_BLOCK_UNROLL)
    x4_ref[0] = x3 + _dot(o_s[...], wo_ref[...].astype(BF16))


def _attn_prompt(x3, k, v, w, rope):
    b, seq, d = x3.shape
    tq = ATTN_TILE
    nt = seq // tq
    per = tq // WINDOW
    tile = lambda bi, ti: (bi, ti, 0)
    prev = lambda bi, ti: (bi, jnp.maximum(ti * per - 1, 0), 0)
    const2 = lambda bi, ti: (0, 0)
    tab = lambda bi, ti: (ti, 0)
    return pl.pallas_call(
        _attn_prompt_kernel,
        grid=(b, nt),
        in_specs=[
            pl.BlockSpec((1, tq, d), tile),
            pl.BlockSpec((1, tq, LANES), tile),
            pl.BlockSpec((1, tq, LANES), tile),
            pl.BlockSpec((1, WINDOW, LANES), prev),
            pl.BlockSpec((1, WINDOW, LANES), prev),
            pl.BlockSpec((1, d), const2),
            pl.BlockSpec((d, d), const2),
            pl.BlockSpec((1, LANES), const2),
            pl.BlockSpec((tq, 3 * LANES), tab),
            pl.BlockSpec(memory_space=pltpu.SMEM),
            pl.BlockSpec((d, d), const2),
        ],
        out_specs=pl.BlockSpec((1, tq, d), tile),
        out_shape=jax.ShapeDtypeStruct((b, seq, d), F32),
        scratch_shapes=[
            pltpu.VMEM((tq, d), BF16),
            pltpu.VMEM((tq, d), BF16),
            pltpu.VMEM((N_KV_HEADS, tq + WINDOW, LANES), BF16),
            pltpu.VMEM((N_KV_HEADS, tq + WINDOW, LANES), BF16),
        ],
        compiler_params=pltpu.CompilerParams(
            dimension_semantics=("arbitrary", "arbitrary"), vmem_limit_bytes=VMEM_LIMIT),
        name="attn_prompt",
    )(x3, k, v, k, v, w["norm_mix1"], w["w_q"], w["q_norm"], rope, w["sinks"], w["w_o"])


def _attn_sample_kernel(x3_ref, knew_ref, vnew_ref, kc_ref, vc_ref, nmix_ref, wq_ref, qn_ref,
                        rope_ref, sinks_ref, wo_ref,
                        x4_ref, kwin_ref, vwin_ref, q_s, o_s):
    rows = x3_ref.shape[0]
    steps = knew_ref.shape[0] // kc_ref.shape[0]
    x3 = x3_ref[...]
    _project_q(x3, nmix_ref, wq_ref, qn_ref, *_rope_parts(rope_ref), q_s)

    stack = N_HEADS * steps
    ri = lax.broadcasted_iota(jnp.int32, (stack, 2 * WINDOW), 0)
    ki = lax.broadcasted_iota(jnp.int32, (stack, 2 * WINDOW), 1)
    tq = ri % steps
    mask = ((ki < WINDOW) & (ki > tq)) | ((ki >= WINDOW) & ((ki - WINDOW) <= tq))
    head_of_row = lax.broadcasted_iota(jnp.int32, (stack, 1), 0) // steps
    sink_col = jnp.zeros((stack, 1), F32)
    for hd in range(N_HEADS):
        sink_col = jnp.where(head_of_row == hd, sinks_ref[hd], sink_col)
    lane = lax.broadcasted_iota(jnp.int32, (steps, LANES), 1)
    first = lane < HEAD_DIM
    pad = jnp.zeros((WINDOW - steps, LANES), F32)

    def to_kv_half(slab, par, g):
        return slab if par == g else pltpu.roll(slab, HEAD_DIM, 1)

    def one_seq(b):
        r0 = pl.multiple_of(b * steps, steps)
        keys = jnp.concatenate([kc_ref[b], knew_ref[pl.ds(r0, steps), :], pad], axis=0).astype(BF16)
        vals = jnp.concatenate([_drop_first_row(vc_ref[b], SUBLANES), vnew_ref[pl.ds(r0, steps), :],
                                pad], axis=0).astype(BF16)
        slabs = []
        for hd in range(N_HEADS):
            par, g = hd % 2, hd // Q_PER_KV
            qc = q_s[pl.ds(r0, steps), (hd // 2) * LANES:(hd // 2 + 1) * LANES]
            own = jnp.where(first, qc, 0.0) if par == 0 else jnp.where(first, 0.0, qc)
            slabs.append(to_kv_half(own, par, g))
        s = _dot_nt(jnp.concatenate(slabs, axis=0).astype(BF16), keys)
        pr, inv = _softmax_sink(s, mask, sink_col)
        out = _dot(pr, vals) * inv
        for c in range(D_MODEL // LANES):
            g = (2 * c) // Q_PER_KV
            even = to_kv_half(out[(2 * c) * steps:(2 * c + 1) * steps], g, 0)
            odd = to_kv_half(out[(2 * c + 1) * steps:(2 * c + 2) * steps], g, 1)
            o_s[pl.ds(r0, steps), c * LANES:(c + 1) * LANES] = jnp.where(first, even, odd)

    def seq_pair(i, carry):
        for u in range(SAMPLE_SEQ_UNROLL):
            one_seq(i * SAMPLE_SEQ_UNROLL + u)
        return carry

    lax.fori_loop(0, rows // steps // SAMPLE_SEQ_UNROLL, seq_pair, 0)
    x4_ref[...] = x3 + _dot(o_s[...].astype(BF16), wo_ref[...].astype(BF16))

    seqs = kc_ref.shape[0]
    for cache_ref, new_ref, win_ref in ((kc_ref, knew_ref, kwin_ref), (vc_ref, vnew_ref, vwin_ref)):
        win_ref[:, 0:WINDOW - steps, :] = cache_ref[:, steps:, :]
        win_ref[:, WINDOW - steps:, :] = new_ref[...].reshape(seqs, steps, LANES)


def _attn_sample(x3, knew, vnew, kcache, vcache, w, rope, steps):
    n, d = x3.shape
    sb = SAMPLE_SEQ_BLOCK
    rows = sb * steps
    tok = lambda i: (i, 0)
    const2 = lambda i: (0, 0)
    return pl.pallas_call(
        _attn_sample_kernel,
        grid=(n // rows,),
        in_specs=[
            pl.BlockSpec((rows, d), tok),
            pl.BlockSpec((rows, LANES), tok),
            pl.BlockSpec((rows, LANES), tok),
            pl.BlockSpec((sb, WINDOW, LANES), lambda i: (i, 0, 0)),
            pl.BlockSpec((sb, WINDOW, LANES), lambda i: (i, 0, 0)),
            pl.BlockSpec((1, d), const2),
            pl.BlockSpec((d, d), const2),
            pl.BlockSpec((1, LANES), const2),
            pl.BlockSpec((rows, 3 * LANES), tok),
            pl.BlockSpec(memory_space=pltpu.SMEM),
            pl.BlockSpec((d, d), const2),
        ],
        out_specs=[
            pl.BlockSpec((rows, d), tok),
            pl.BlockSpec((sb, WINDOW, LANES), lambda i: (i, 0, 0)),
            pl.BlockSpec((sb, WINDOW, LANES), lambda i: (i, 0, 0)),
        ],
        out_shape=[
            jax.ShapeDtypeStruct((n, d), F32),
            jax.ShapeDtypeStruct(kcache.shape, F32),
            jax.ShapeDtypeStruct(vcache.shape, F32),
        ],
        scratch_shapes=[
            pltpu.VMEM((rows, d), F32),
            pltpu.VMEM((rows, d), F32),
        ],
        compiler_params=pltpu.CompilerParams(
            dimension_semantics=("arbitrary",), vmem_limit_bytes=VMEM_LIMIT),
        name="attn_sample",
    )(x3, knew, vnew, kcache, vcache, w["norm_mix1"], w["w_q"], w["q_norm"], rope,
      w["sinks"], w["w_o"])


def _rope_table(pos):
    half = ROPE_DIM // 2
    dim = jnp.arange(LANES, dtype=jnp.int32) % HEAD_DIM
    inv = jnp.float32(ROPE_THETA) ** (-((dim % half).astype(jnp.float32) * 2.0 / ROPE_DIM))
    inv = jnp.where(dim < ROPE_DIM, inv, 0.0)
    ang = pos.astype(jnp.float32)[:, None] * inv[None, :]
    cos, sin = jnp.cos(ang), jnp.sin(ang)
    s_next = jnp.where(dim < half, -sin, 0.0)
    s_prev = jnp.where((dim >= half) & (dim < ROPE_DIM), sin, 0.0)
    return jnp.concatenate([cos, s_next, s_prev], axis=1)


def kernel(x_prompt, x_sample, p_prompt, p_sample, state_pool, cache_k_win, cache_v_win, norm_mix, norm_ffn, norm_ple, pool_w, pool_scale, kv_norm, w_kv, k_norm, w_q, q_norm, sinks, w_o, router_g_w, router_g_b, router_e_w, router_e_b, exp_gate, exp_up, exp_down, ple_gate, ple_proj):
    b, seq, d = x_prompt.shape
    sb, steps, _ = x_sample.shape
    depth = norm_mix.shape[0]
    row = lambda v: v.reshape(1, -1)

    gap_w = jnp.zeros((depth, d, EXPERT_LANE0 - N_GROUPS), F32)
    tail_w = jnp.zeros((depth, d, LANES - EXPERT_LANE0 - N_EXPERTS), F32)
    router_w = jnp.concatenate(
        [router_g_w, gap_w, router_e_w.reshape(depth, d, N_EXPERTS), tail_w], axis=2)
    router_b = jnp.concatenate(
        [router_g_b, gap_w[:, 0], router_e_b.reshape(depth, N_EXPERTS), tail_w[:, 0]], axis=1)
    moe_w = {
        "norm_ffn": norm_ffn.reshape(depth, 1, d), "router_w": router_w,
        "router_b": router_b.reshape(depth, 1, LANES),
        "exp_gate": exp_gate.astype(BF16), "exp_up": exp_up.astype(BF16),
        "exp_down": exp_down.astype(BF16), "norm_ple": norm_ple.reshape(depth, 1, d),
        "ple_gate": ple_gate.astype(BF16), "ple_proj": ple_proj.astype(BF16),
    }
    two = lambda a: jnp.concatenate([a, a])
    attn_w = {
        "norm_mix1": row(norm_mix[1]), "w_q": w_q[0], "q_norm": row(two(q_norm[0])),
        "sinks": sinks[0], "w_o": w_o[0],
    }
    pw = pool_w[0]
    nmix0 = row(norm_mix[0])
    pscale = row(pool_scale[0])

    rope_p = _rope_table(jnp.arange(seq, dtype=jnp.int32))
    rope_s = jnp.tile(_rope_table(PAST_LEN + jnp.arange(steps, dtype=jnp.int32)), (sb, 1))
    kv_w = {"kv_norm": row(kv_norm), "w_kv": w_kv.astype(BF16), "k_norm": row(two(k_norm))}

    pp_all = p_prompt.reshape(depth, b * seq, D_PLE)
    x1p, pool16 = _mixer0_prompt(x_prompt, nmix0, pw, pscale)
    x3p, kp, vp = _moe_layer(x1p.reshape(b * seq, d), pp_all, 0, moe_w, dict(kv_w, rope=rope_p))
    kp = kp.reshape(b, seq, LANES)
    vp = vp.reshape(b, seq, LANES)
    x4p = _attn_prompt(x3p.reshape(b, seq, d), kp, vp, attn_w, rope_p)
    y_prompt = _moe_layer(x4p.reshape(b * seq, d), pp_all, 1, moe_w)[0].reshape(b, seq, d)
    pool_prompt = pool16[None, :, CTX_ROWS - POOL_CTX:, :]
    k_win_prompt = kp[:, seq - WINDOW:].reshape(b, WINDOW, N_KV_HEADS, HEAD_DIM)
    v_win_prompt = vp[:, seq - WINDOW:].reshape(b, WINDOW, N_KV_HEADS, HEAD_DIM)

    ps_all = p_sample.reshape(depth, sb * steps, D_PLE)
    x1s, pool_s = _mixer0_sample(x_sample, state_pool, nmix0, pw, pscale)
    x3s, ks, vs = _moe_layer(x1s.reshape(sb * steps, d), ps_all, 0, moe_w, dict(kv_w, rope=rope_s))
    kc = cache_k_win.reshape(sb, WINDOW, LANES)
    vc = cache_v_win.reshape(sb, WINDOW, LANES)
    x4s, kwin, vwin = _attn_sample(x3s, ks, vs, kc, vc, attn_w, rope_s, steps)
    y_sample = _moe_layer(x4s, ps_all, 1, moe_w)[0].reshape(sb, steps, d)
    pool_sample = pool_s[None]
    k_win_sample = kwin.reshape(cache_k_win.shape)
    v_win_sample = vwin.reshape(cache_v_win.shape)

    return (y_prompt, y_sample, pool_prompt, pool_sample,
            k_win_prompt, v_win_prompt, k_win_sample, v_win_sample)
```
